```python
import math
import jax, jax.numpy as jnp
from jax import lax
import numpy as np

D_MODEL = 1024
BATCH = 8
SEQ = 8192
DEPTH = 1
DEC_BATCH = 16
DEC_SEQ = 2048
PAST_LEN = 128

ATT_HEADS = 8
ATT_HD = 64
ATT_W = ATT_HEADS * ATT_HD
RET_HEADS = 4
RET_HD = 128
RET_W = RET_HEADS * RET_HD
MIX_W = ATT_W + RET_W
IN_W = 3 * ATT_W + 4 * RET_W
DILATED_PATTERNS = ((128, 1), (512, 4), (2048, 16))
RET_CHUNK = 128
ROPE_THETA = 10000.0
N_EXPERTS = 16
D_FF = 2048
CAPACITY_FACTOR = 2
NORM_EPS = 1e-6
NEG_INF = -1e30

kernel_name = "hybrid_dilated_attn_retention_ec_moe_encoder"


def rmsnorm(x, g):
    x32 = x.astype(jnp.float32)
    y = x32 * lax.rsqrt(jnp.mean(x32 * x32, axis=-1, keepdims=True) + NORM_EPS)
    return (y * g.astype(jnp.float32)).astype(x.dtype)


def rope(x, pos):
    hd = x.shape[-1]
    inv_freq = ROPE_THETA ** (-jnp.arange(0, hd, 2, dtype=jnp.float32) / hd)
    ang = pos.astype(jnp.float32)[:, None] * inv_freq[None, :]
    cos, sin = jnp.cos(ang), jnp.sin(ang)
    x32 = x.astype(jnp.float32)
    x1, x2 = x32[..., : hd // 2], x32[..., hd // 2:]
    out = jnp.concatenate([x1 * cos - x2 * sin, x2 * cos + x1 * sin], axis=-1)
    return out.astype(x.dtype)


def banded_softmax(q, k, v, half):
    lead = q.shape[:-2]
    L, hd = q.shape[-2], q.shape[-1]
    nb = -(-L // half)
    Lp = nb * half
    nlead = len(lead)
    qp = jnp.pad(q, [(0, 0)] * nlead + [(0, Lp - L), (0, 0)])
    kv_pad = [(0, 0)] * nlead + [(half, Lp - L + half), (0, 0)]
    kp, vp = jnp.pad(k, kv_pad), jnp.pad(v, kv_pad)
    qb = qp.reshape(*lead, nb, half, hd)

    def windows(t):
        tb = t.reshape(*lead, nb + 2, half, hd)
        return jnp.concatenate([tb[..., :-2, :, :], tb[..., 1:-1, :, :], tb[..., 2:, :, :]], axis=-2)

    kw, vw = windows(kp), windows(vp)
    t = jnp.arange(half)
    j = jnp.arange(3 * half)
    i = jnp.arange(nb)
    off = j[None, :] - half - t[:, None]
    kpos = (i[:, None] - 1) * half + j[None, :]
    mask = (jnp.abs(off) <= half)[None] & ((kpos >= 0) & (kpos < L))[:, None, :]
    s = jnp.einsum('...ntd,...nkd->...ntk', qb, kw).astype(jnp.float32)
    s = jnp.where(mask, s, NEG_INF)
    m = jnp.max(s, axis=-1, keepdims=True)
    p = jnp.exp(s - m)
    l = jnp.sum(p, axis=-1)
    o = jnp.einsum('...ntk,...nkd->...ntd', p, vw.astype(jnp.float32)) / l[..., None]
    lse = m[..., 0] + jnp.log(l)
    o = o.reshape(*lead, Lp, hd)[..., :L, :]
    lse = lse.reshape(*lead, Lp)[..., :L]
    return o, lse


def dilated_attention(q, k, v):
    B, H, S, hd = q.shape
    outs, lses = [], []
    for window, dil in DILATED_PATTERNS:
        half = window // (2 * dil)
        L = S // dil

        def split(t):
            return t.reshape(B, H, L, dil, hd).transpose(0, 1, 3, 2, 4)

        o, lse = banded_softmax(split(q), split(k), split(v), half)
        outs.append(o.transpose(0, 1, 3, 2, 4).reshape(B, H, S, hd))
        lses.append(lse.transpose(0, 1, 3, 2).reshape(B, H, S))
    w = jax.nn.softmax(jnp.stack(lses), axis=0)
    out = jnp.sum(w[..., None] * jnp.stack(outs), axis=0)
    return out.astype(q.dtype)


def retention_dir(q, k, v, log_g, include_diag):
    B, H, S, dk = q.shape
    dv = v.shape[-1]
    C = RET_CHUNK
    nc = S // C
    qc = q.reshape(B, H, nc, C, dk)
    kc = k.reshape(B, H, nc, C, dk)
    vc = v.reshape(B, H, nc, C, dv)
    n = jnp.arange(C, dtype=jnp.float32)
    diff = n[:, None] - n[None, :]
    mask = diff >= 0 if include_diag else diff > 0
    D = jnp.where(mask[None], jnp.exp(log_g[:, None, None] * jnp.maximum(diff, 0.0)[None]), 0.0)
    inner = jnp.einsum('bhntd,bhnsd->bhnts', qc, kc) * D[None, :, None]
    o_in = jnp.einsum('bhnts,bhnse->bhnte', inner, vc)
    k_dec = jnp.exp(log_g[:, None] * (C - 1 - n)[None, :])
    kv = jnp.einsum('bhnsd,bhnse->bhnde', kc * k_dec[None, :, None, :, None], vc)
    g_chunk = jnp.exp(log_g * C)[None, :, None, None]

    def step(R, kv_i):
        return R * g_chunk + kv_i, R

    _, R_prev = lax.scan(step, jnp.zeros((B, H, dk, dv), jnp.float32), jnp.moveaxis(kv, 2, 0))
    R_prev = jnp.moveaxis(R_prev, 0, 2)
    q_dec = jnp.exp(log_g[:, None] * (n + 1.0)[None, :])
    o_cross = jnp.einsum('bhntd,bhnde->bhnte', qc * q_dec[None, :, None, :, None], R_prev)
    return (o_in + o_cross).reshape(B, H, S, dv)


def expert_choice_ffn(h, w_router, w_gate, w_up, w_down):
    B, S, D = h.shape
    T = B * S
    hf = h.reshape(T, D)
    aff = jax.nn.softmax((hf @ w_router).astype(jnp.float32), axis=-1)
    cap = CAPACITY_FACTOR * T // N_EXPERTS
    gates, idx = lax.top_k(aff.T, cap)
    xin = hf[idx]
    a = jnp.einsum('ecd,edf->ecf', xin, w_gate)
    u = jnp.einsum('ecd,edf->ecf', xin, w_up)
    y = jnp.einsum('ecf,efd->ecd', jax.nn.silu(a) * u, w_down) * gates[..., None].astype(h.dtype)
    out = jnp.zeros((T, D), h.dtype).at[idx.reshape(-1)].add(y.reshape(-1, D))
    return out.reshape(B, S, D)


def encoder_layer(x, norm1_g, w_in, attn_qnorm_g, attn_knorm_g, ret_decay_fwd, ret_decay_bwd,
                  ret_norm_g, w_out, norm2_g, w_router, w_gate_e, w_up_e, w_down_e):
    B, S, _ = x.shape
    h = rmsnorm(x, norm1_g)
    proj = h @ w_in
    qa, ka, va, qr, kr, vr, gr = jnp.split(proj, 7, axis=-1)

    def heads(t, nh, hd):
        return t.reshape(B, S, nh, hd).transpose(0, 2, 1, 3)

    def merge(t):
        return t.transpose(0, 2, 1, 3).reshape(B, S, -1)

    pos = jnp.arange(S)
    qa = rope(rmsnorm(heads(qa, ATT_HEADS, ATT_HD), attn_qnorm_g), pos) * (ATT_HD ** -0.5)
    ka = rope(rmsnorm(heads(ka, ATT_HEADS, ATT_HD), attn_knorm_g), pos)
    att = merge(dilated_attention(qa, ka, heads(va, ATT_HEADS, ATT_HD)))

    qr = rope(heads(qr, RET_HEADS, RET_HD), pos).astype(jnp.float32)
    kr = rope(heads(kr, RET_HEADS, RET_HD), pos).astype(jnp.float32) * (RET_HD ** -0.5)
    vr = heads(vr, RET_HEADS, RET_HD).astype(jnp.float32)
    log_gf = -jnp.exp(ret_decay_fwd.astype(jnp.float32))
    log_gb = -jnp.exp(ret_decay_bwd.astype(jnp.float32))
    o_f = retention_dir(qr, kr, vr, log_gf, True)
    o_b = jnp.flip(retention_dir(jnp.flip(qr, 2), jnp.flip(kr, 2), jnp.flip(vr, 2), log_gb, False), 2)
    o = o_f + o_b
    mu = jnp.mean(o, axis=-1, keepdims=True)
    var = jnp.mean(jnp.square(o - mu), axis=-1, keepdims=True)
    o = (o - mu) * lax.rsqrt(var + NORM_EPS)
    ret = (merge(o) * ret_norm_g.astype(jnp.float32)).astype(x.dtype) * jax.nn.silu(gr)

    x = x + jnp.concatenate([att, ret], axis=-1) @ w_out
    x = x + expert_choice_ffn(rmsnorm(x, norm2_g), w_router, w_gate_e, w_up_e, w_down_e)
    return x


def setup_inputs(seed: int = 0) -> dict:
    key = jax.random.key(seed)
    ks = jax.random.split(key, 16)
    f32 = jnp.float32
    gam = 1.0 - 2.0 ** (-5.0 - np.arange(RET_HEADS, dtype=np.float32))
    base_decay = jnp.asarray(np.log(-np.log(gam)), f32)
    return {
        "x_prompt": jax.random.normal(ks[0], (BATCH, SEQ, D_MODEL), f32),
        "x_sample": jax.random.normal(ks[1], (DEC_BATCH, DEC_SEQ, D_MODEL), f32),
        "norm1_g": 1.0 + 0.02 * jax.random.normal(ks[2], (DEPTH, D_MODEL), f32),
        "w_in": jax.random.normal(ks[3], (DEPTH, D_MODEL, IN_W), f32) * D_MODEL ** -0.5,
        "attn_qnorm_g": 1.0 + 0.02 * jax.random.normal(ks[4], (DEPTH, ATT_HD), f32),
        "attn_knorm_g": 1.0 + 0.02 * jax.random.normal(ks[5], (DEPTH, ATT_HD), f32),
        "ret_decay_fwd": base_decay[None] + 0.05 * jax.random.normal(ks[6], (DEPTH, RET_HEADS), f32),
        "ret_decay_bwd": base_decay[None] + 0.05 * jax.random.normal(ks[7], (DEPTH, RET_HEADS), f32),
        "ret_norm_g": 1.0 + 0.02 * jax.random.normal(ks[8], (DEPTH, RET_W), f32),
        "w_out": jax.random.normal(ks[9], (DEPTH, MIX_W, D_MODEL), f32) * MIX_W ** -0.5,
        "norm2_g": 1.0 + 0.02 * jax.random.normal(ks[10], (DEPTH, D_MODEL), f32),
        "w_router": jax.random.normal(ks[11], (DEPTH, D_MODEL, N_EXPERTS), f32) * D_MODEL ** -0.5,
        "w_gate_e": jax.random.normal(ks[12], (DEPTH, N_EXPERTS, D_MODEL, D_FF), f32) * D_MODEL ** -0.5,
        "w_up_e": jax.random.normal(ks[13], (DEPTH, N_EXPERTS, D_MODEL, D_FF), f32) * D_MODEL ** -0.5,
        "w_down_e": jax.random.normal(ks[14], (DEPTH, N_EXPERTS, D_FF, D_MODEL), f32) * D_FF ** -0.5,
    }


def reference(x_prompt, x_sample, norm1_g, w_in, attn_qnorm_g, attn_knorm_g, ret_decay_fwd,
              ret_decay_bwd, ret_norm_g, w_out, norm2_g, w_router, w_gate_e, w_up_e, w_down_e):
    y_prompt = x_prompt
    y_sample = x_sample
    for l in range(DEPTH):
        layer_params = (norm1_g[l], w_in[l], attn_qnorm_g[l], attn_knorm_g[l], ret_decay_fwd[l],
                        ret_decay_bwd[l], ret_norm_g[l], w_out[l], norm2_g[l], w_router[l],
                        w_gate_e[l], w_up_e[l], w_down_e[l])
        y_prompt = encoder_layer(y_prompt, *layer_params)
        y_sample = encoder_layer(y_sample, *layer_params)
    return (y_prompt, y_sample)
```

```python
import functools

import jax
import jax.numpy as jnp
import numpy as np
from jax import lax
from jax.experimental import pallas as pl
from jax.experimental.pallas import tpu as pltpu

F32 = jnp.float32
BF16 = jnp.bfloat16

D_MODEL = 1024
ATT_HEADS, ATT_HD = 8, 64
RET_HEADS, RET_HD = 4, 128
ATT_W = ATT_HEADS * ATT_HD
RET_W = RET_HEADS * RET_HD
IN_W = 3 * ATT_W + 4 * RET_W
LANES = 128
N_SLABS = IN_W // LANES
GROUP_W = 512
SLABS_PER_GROUP = GROUP_W // LANES
RET_CHUNK = 128
ROPE_THETA = 10000.0
N_EXPERTS = 16
D_FF = 2048
CAPACITY_FACTOR = 2
NORM_EPS = 1e-6
NEG_INF = -1e30
ATT_HALF_SPAN = 1024
ATT_QBLK = 128
VMEM_LIMIT = 56 * 1024 * 1024


def _cparams(sem):
    return pltpu.CompilerParams(dimension_semantics=sem, vmem_limit_bytes=VMEM_LIMIT)


def _inproj_kernel(x_ref, g1_ref, w_ref, gq_ref, gk_ref, ca_ref, sa_ref, cr_ref, sr_ref, o_ref):
    x = x_ref[...]
    ms = jnp.mean(x * x, axis=-1, keepdims=True)
    h = (x * lax.rsqrt(ms + NORM_EPS) * g1_ref[...]).astype(BF16)
    tm = x.shape[0]
    lane = lax.broadcasted_iota(jnp.int32, (tm, LANES), 1)
    first = lane < ATT_HD
    low_half = (lane & (ATT_HD // 2)) == 0
    ca, sa, cr, sr = ca_ref[...], sa_ref[...], cr_ref[...], sr_ref[...]
    for grp in range(IN_W // GROUP_W):
        acc = jnp.dot(h, w_ref[:, grp * GROUP_W:(grp + 1) * GROUP_W], preferred_element_type=F32)
        for p in range(SLABS_PER_GROUP):
            a = acc[:, p * LANES:(p + 1) * LANES]
            if grp in (0, 1):
                sq = a * a
                s0 = jnp.sum(jnp.where(first, sq, 0.0), axis=-1, keepdims=True)
                s1 = jnp.sum(jnp.where(first, 0.0, sq), axis=-1, keepdims=True)
                ms2 = jnp.where(first, s0, s1) * (1.0 / ATT_HD)
                g = gq_ref[...] if grp == 0 else gk_ref[...]
                y = a * lax.rsqrt(ms2 + NORM_EPS) * g
                partner = jnp.where(low_half, pltpu.roll(y, LANES - ATT_HD // 2, 1),
                                    pltpu.roll(y, ATT_HD // 2, 1))
                r = y * ca + partner * sa
                if grp == 0:
                    r = r * (ATT_HD ** -0.5)
            elif grp in (3, 4):
                r = a * cr + pltpu.roll(a, RET_HD // 2, 1) * sr
                if grp == 4:
                    r = r * (RET_HD ** -0.5)
            else:
                r = a
            o_ref[grp * SLABS_PER_GROUP + p] = r.astype(BF16)


def _rope_tables(S):
    pos = jnp.arange(S, dtype=F32)

    def table(hd):
        inv_freq = ROPE_THETA ** (-jnp.arange(0, hd, 2, dtype=F32) / hd)
        ang = pos[:, None] * inv_freq[None, :]
        cos, sin = jnp.cos(ang), jnp.sin(ang)
        reps = LANES // hd
        cos_t = jnp.tile(jnp.concatenate([cos, cos], axis=-1), (1, reps))
        sin_t = jnp.tile(jnp.concatenate([-sin, sin], axis=-1), (1, reps))
        return cos_t, sin_t

    ca, sa = table(ATT_HD)
    cr, sr = table(RET_HD)
    return ca, sa, cr, sr


def _inproj(x2d, S, g1, w_in_bf, gq, gk, tables, tm):
    T = x2d.shape[0]
    n_pos_blk = S // tm
    tab_spec = pl.BlockSpec((tm, LANES), lambda i: (i % n_pos_blk, 0))
    full = lambda shape: pl.BlockSpec(shape, lambda i: (0,) * len(shape))
    return pl.pallas_call(
        _inproj_kernel,
        grid=(T // tm,),
        in_specs=[pl.BlockSpec((tm, D_MODEL), lambda i: (i, 0)), full((1, D_MODEL)),
                  full((D_MODEL, IN_W)), full((1, LANES)), full((1, LANES)),
                  tab_spec, tab_spec, tab_spec, tab_spec],
        out_specs=pl.BlockSpec((N_SLABS, tm, LANES), lambda i: (0, i, 0)),
        out_shape=jax.ShapeDtypeStruct((N_SLABS, T, LANES), BF16),
        compiler_params=_cparams(("parallel",)),
        name="inproj",
    )(x2d, g1, w_in_bf, gq, gk, *tables)


def _multiplicity_table():
    dt = np.arange(-16, 17)[:, None, None]
    i = np.arange(ATT_QBLK)[None, :, None]
    j = np.arange(LANES)[None, None, :]
    d = 128 * dt + j - i
    c = np.zeros(d.shape, np.float32)
    for window, dil in ((128, 1), (512, 4), (2048, 16)):
        c += ((np.abs(d) <= window // 2) & (d % dil == 0)).astype(np.float32)
    return jnp.asarray(c)


def _attn_kernel(q_ref, k_ref, v_ref, c_ref, o_ref, s_scr, *, S, W):
    nkt = W // LANES
    t0 = pl.program_id(2) * ATT_QBLK
    start = jnp.clip(t0 - ATT_HALF_SPAN, 0, S - W)
    dt0 = (start - t0 + 16 * LANES) // LANES
    q = q_ref[...]
    lane = lax.broadcasted_iota(jnp.int32, (ATT_QBLK, LANES), 1)
    first = lane < ATT_HD
    zero = jnp.zeros_like(q)
    qh = (jnp.where(first, q, zero), jnp.where(first, zero, q))
    nt_dims = (((1,), (1,)), ((), ()))

    def pass1(jt, carry):
        row = pl.multiple_of(start + jt * LANES, LANES)
        kt = k_ref[pl.ds(row, LANES), :]
        valid = c_ref[dt0 + jt] > 0.0
        out = []
        for h in range(2):
            s = lax.dot_general(qh[h], kt, nt_dims, preferred_element_type=F32)
            sm = jnp.where(valid, s, NEG_INF)
            s_scr[h, jt] = sm
            out.append(jnp.maximum(carry[h], sm))
        return tuple(out)

    neg = jnp.full((ATT_QBLK, LANES), NEG_INF, F32)
    m_el = lax.fori_loop(0, nkt, pass1, (neg, neg))
    m_row = [jnp.max(m, axis=-1, keepdims=True) for m in m_el]

    def pass2(jt, carry):
        row = pl.multiple_of(start + jt * LANES, LANES)
        vt = v_ref[pl.ds(row, LANES), :]
        c = c_ref[dt0 + jt]
        l0, l1, a0, a1 = carry
        ls, accs = [l0, l1], [a0, a1]
        for h in range(2):
            p = jnp.exp(s_scr[h, jt] - m_row[h]) * c
            ls[h] = ls[h] + jnp.sum(p, axis=-1, keepdims=True)
            accs[h] = accs[h] + jnp.dot(p.astype(BF16), vt, preferred_element_type=F32)
        return ls[0], ls[1], accs[0], accs[1]

    zl = jnp.zeros((ATT_QBLK, 1), F32)
    za = jnp.zeros((ATT_QBLK, LANES), F32)
    l0, l1, a0, a1 = lax.fori_loop(0, nkt, pass2, (zl, zl, za, za))
    o_ref[...] = jnp.where(first, a0 / l0, a1 / l1).astype(o_ref.dtype)


def _attention(qkv4, ctab, B, S):
    W = min(2 * ATT_HALF_SPAN + ATT_QBLK, S)
    n_pairs = ATT_W // LANES
    kv_spec = lambda off: pl.BlockSpec((None, None, S, LANES), lambda b, p, i: (off + p, b, 0, 0))
    return pl.pallas_call(
        functools.partial(_attn_kernel, S=S, W=W),
        grid=(B, n_pairs, S // ATT_QBLK),
        in_specs=[pl.BlockSpec((None, None, ATT_QBLK, LANES), lambda b, p, i: (p, b, i, 0)),
                  kv_spec(n_pairs), kv_spec(2 * n_pairs),
                  pl.BlockSpec(ctab.shape, lambda b, p, i: (0, 0, 0))],
        out_specs=pl.BlockSpec((None, ATT_QBLK, LANES), lambda b, p, i: (b, i, p)),
        out_shape=jax.ShapeDtypeStruct((B, S, ATT_W), BF16),
        scratch_shapes=[pltpu.VMEM((2, W // LANES, ATT_QBLK, LANES), F32)],
        compiler_params=_cparams(("parallel", "parallel", "arbitrary")),
        name="dilated_attn",
    )(qkv4, qkv4, qkv4, ctab)


def _ret_kernel(dec_ref, q_ref, k_ref, v_ref, g_ref, gn_ref, o_ref, of_scr, r_scr, *, S):
    C = RET_CHUNK
    nc = S // C
    h = pl.program_id(1)
    row = lax.broadcasted_iota(jnp.int32, (C, C), 0).astype(F32)
    col = lax.broadcasted_iota(jnp.int32, (C, C), 1).astype(F32)
    nt_dims = (((1,), (1,)), ((), ()))
    tn_dims = (((0,), (0,)), ((), ()))

    def consts(direction):
        log_g = -jnp.exp(jnp.full((C, C), dec_ref[direction, h], F32))
        if direction == 0:
            diff = row - col
            mask = diff >= 0.0
            k_dec, q_dec = jnp.exp(log_g * (C - 1.0 - row)), jnp.exp(log_g * (row + 1.0))
        else:
            diff = col - row
            mask = diff > 0.0
            k_dec, q_dec = jnp.exp(log_g * row), jnp.exp(log_g * (C - row))
        dmat = jnp.where(mask, jnp.exp(log_g * jnp.maximum(diff, 0.0)), 0.0)
        return dmat, k_dec, q_dec, jnp.exp(log_g * C)

    def chunk_out(c, dmat, k_dec, q_dec, g_chunk):
        sl = pl.ds(pl.multiple_of(c * C, C), C)
        qc, kc, vc = q_ref[sl, :], k_ref[sl, :], v_ref[sl, :]
        qk = lax.dot_general(qc, kc, nt_dims, preferred_element_type=F32)
        inner = (qk * dmat).astype(BF16)
        state = r_scr[...]
        o = jnp.dot(inner, vc, preferred_element_type=F32)
        o = o + jnp.dot((qc.astype(F32) * q_dec).astype(BF16), state.astype(BF16),
                        preferred_element_type=F32)
        kd = (kc.astype(F32) * k_dec).astype(BF16)
        kv = lax.dot_general(kd, vc, tn_dims, preferred_element_type=F32)
        r_scr[...] = state * g_chunk + kv
        return sl, o

    cf = consts(0)
    r_scr[...] = jnp.zeros((C, C), F32)

    def fwd(c, carry):
        sl, o = chunk_out(c, *cf)
        of_scr[sl, :] = o
        return carry

    lax.fori_loop(0, nc, fwd, 0)

    cb = consts(1)
    r_scr[...] = jnp.zeros((C, C), F32)
    gn = gn_ref[...]

    def bwd(i, carry):
        sl, o = chunk_out(nc - 1 - i, *cb)
        o = o + of_scr[sl, :]
        mu = jnp.mean(o, axis=-1, keepdims=True)
        var = jnp.mean(jnp.square(o - mu), axis=-1, keepdims=True)
        y = (o - mu) * lax.rsqrt(var + NORM_EPS) * gn
        g = g_ref[sl, :].astype(F32)
        o_ref[sl, :] = (y * (g * (1.0 / (1.0 + jnp.exp(-g))))).astype(o_ref.dtype)
        return carry

    lax.fori_loop(0, nc, bwd, 0)


def _retention(qkv4, decays, gn, B, S):
    base = 3 * (ATT_W // LANES)
    spec = lambda off: pl.BlockSpec((None, None, S, LANES), lambda b, h: (base + off + h, b, 0, 0))
    return pl.pallas_call(
        functools.partial(_ret_kernel, S=S),
        grid=(B, RET_HEADS),
        in_specs=[pl.BlockSpec(memory_space=pltpu.SMEM),
                  spec(0), spec(RET_HEADS), spec(2 * RET_HEADS), spec(3 * RET_HEADS),
                  pl.BlockSpec((1, LANES), lambda b, h: (0, h))],
        out_specs=pl.BlockSpec((None, S, LANES), lambda b, h: (b, 0, h)),
        out_shape=jax.ShapeDtypeStruct((B, S, RET_W), BF16),
        scratch_shapes=[pltpu.VMEM((S, LANES), F32), pltpu.VMEM((RET_CHUNK, RET_CHUNK), F32)],
        compiler_params=_cparams(("parallel", "arbitrary")),
        name="retention",
    )(decays, qkv4, qkv4, qkv4, qkv4, gn)


def _outproj_kernel(x_ref, att_ref, ret_ref, wo_ref, g2_ref, wr_ref, x1_ref, h2_ref, aff_ref):
    y = jnp.dot(att_ref[...], wo_ref[:ATT_W, :], preferred_element_type=F32)
    y = y + jnp.dot(ret_ref[...], wo_ref[ATT_W:, :], preferred_element_type=F32)
    x1 = x_ref[...] + y
    x1_ref[...] = x1
    ms = jnp.mean(x1 * x1, axis=-1, keepdims=True)
    h2 = x1 * lax.rsqrt(ms + NORM_EPS) * g2_ref[...]
    h2_ref[...] = h2.astype(h2_ref.dtype)
    logits = lax.dot_general(wr_ref[...], h2.astype(BF16), (((1,), (1,)), ((), ())),
                             preferred_element_type=F32)
    e = jnp.exp(logits - jnp.max(logits, axis=0, keepdims=True))
    aff_ref[...] = e / jnp.sum(e, axis=0, keepdims=True)


def _outproj(x2d, att, ret, wo_bf, g2, wr_t_bf, tm):
    T = x2d.shape[0]
    full = lambda shape: pl.BlockSpec(shape, lambda i: (0,) * len(shape))
    return pl.pallas_call(
        _outproj_kernel,
        grid=(T // tm,),
        in_specs=[pl.BlockSpec((tm, D_MODEL), lambda i: (i, 0)),
                  pl.BlockSpec((tm, ATT_W), lambda i: (i, 0)),
                  pl.BlockSpec((tm, RET_W), lambda i: (i, 0)),
                  full((ATT_W + RET_W, D_MODEL)), full((1, D_MODEL)), full((N_EXPERTS, D_MODEL))],
        out_specs=[pl.BlockSpec((tm, D_MODEL), lambda i: (i, 0)),
                   pl.BlockSpec((tm, D_MODEL), lambda i: (i, 0)),
                   pl.BlockSpec((N_EXPERTS, tm), lambda i: (0, i))],
        out_shape=[jax.ShapeDtypeStruct((T, D_MODEL), F32),
                   jax.ShapeDtypeStruct((T, D_MODEL), BF16),
                   jax.ShapeDtypeStruct((N_EXPERTS, T), F32)],
        compiler_params=_cparams(("parallel",)),
        name="outproj_router",
    )(x2d, att, ret, wo_bf, g2, wr_t_bf)


def _select_kernel(aff_ref, sel_ref, *, cap):
    bits = pltpu.bitcast(aff_ref[...], jnp.int32)
    E, T = bits.shape
    capf = jnp.float32(cap)

    def count(mask):
        return jnp.sum(jnp.where(mask, 1.0, 0.0), axis=1, keepdims=True)

    def thr_step(i, thr):
        cand = thr | jnp.left_shift(jnp.int32(1), 30 - i)
        return jnp.where(count(bits >= cand) >= capf, cand, thr)

    thr = lax.fori_loop(0, 31, thr_step, jnp.zeros((E, 1), jnp.int32))
    above = bits > thr
    ties = bits == thr
    need = capf - count(above)
    idx = lax.broadcasted_iota(jnp.int32, (E, T), 1)
    n_idx_bits = max(1, (T - 1).bit_length())

    def cut_step(i, cut):
        cand = cut | jnp.left_shift(jnp.int32(1), n_idx_bits - 1 - i)
        return jnp.where(count(ties & (idx < cand)) < need, cand, cut)

    cut = lax.fori_loop(0, n_idx_bits, cut_step, jnp.zeros((E, 1), jnp.int32))
    sel_ref[...] = jnp.where(above | (ties & (idx <= cut)), 1, 0).astype(jnp.int32)


def _select(aff_t, cap):
    E, T = aff_t.shape
    return pl.pallas_call(
        functools.partial(_select_kernel, cap=cap),
        grid=(1,),
        in_specs=[pl.BlockSpec((E, T), lambda i: (0, 0))],
        out_specs=pl.BlockSpec((E, T), lambda i: (0, 0)),
        out_shape=jax.ShapeDtypeStruct((E, T), jnp.int32),
        compiler_params=_cparams(("arbitrary",)),
        name="expert_select",
    )(aff_t)


FF_CHUNK = 512


def _ffn_kernel(xin_ref, wg_ref, wu_ref, wd_ref, gate_ref, y_ref):
    xin = xin_ref[...]
    acc = jnp.zeros(y_ref.shape, F32)
    for f in range(D_FF // FF_CHUNK):
        fs = slice(f * FF_CHUNK, (f + 1) * FF_CHUNK)
        a = jnp.dot(xin, wg_ref[:, fs], preferred_element_type=F32)
        u = jnp.dot(xin, wu_ref[:, fs], preferred_element_type=F32)
        hmid = (a * (1.0 / (1.0 + jnp.exp(-a))) * u).astype(BF16)
        acc = acc + jnp.dot(hmid, wd_ref[fs, :], preferred_element_type=F32)
    y_ref[...] = (acc * gate_ref[...]).astype(y_ref.dtype)


def _ffn(xin, wg_bf, wu_bf, wd_bf, gates, tm):
    E, cap, _ = xin.shape
    return pl.pallas_call(
        _ffn_kernel,
        grid=(E, cap // tm),
        in_specs=[pl.BlockSpec((None, tm, D_MODEL), lambda e, j: (e, j, 0)),
                  pl.BlockSpec((None, D_MODEL, D_FF), lambda e, j: (e, 0, 0)),
                  pl.BlockSpec((None, D_MODEL, D_FF), lambda e, j: (e, 0, 0)),
                  pl.BlockSpec((None, D_FF, D_MODEL), lambda e, j: (e, 0, 0)),
                  pl.BlockSpec((None, tm, 1), lambda e, j: (e, j, 0))],
        out_specs=pl.BlockSpec((None, tm, D_MODEL), lambda e, j: (e, j, 0)),
        out_shape=jax.ShapeDtypeStruct((E, cap, D_MODEL), F32),
        compiler_params=_cparams(("parallel", "arbitrary")),
        name="expert_ffn",
    )(xin, wg_bf, wu_bf, wd_bf, gates)


def _layer(x, p, ctab):
    B, S, _ = x.shape
    T = B * S
    tm = 512
    x2d = x.reshape(T, D_MODEL)
    qkv = _inproj(x2d, S, p["g1"], p["w_in"], p["gq"], p["gk"], _rope_tables(S), tm)
    qkv4 = qkv.reshape(N_SLABS, B, S, LANES)
    att = _attention(qkv4, ctab, B, S).reshape(T, ATT_W)
    ret = _retention(qkv4, p["decays"], p["gn"], B, S).reshape(T, RET_W)
    x1, h2, aff_t = _outproj(x2d, att, ret, p["w_out"], p["g2"], p["w_router_t"], tm)

    cap = CAPACITY_FACTOR * T // N_EXPERTS
    sel = _select(aff_t, cap)
    tok = jnp.arange(T, dtype=jnp.int32)[None, :]
    idx = jnp.sort(jnp.where(sel > 0, tok, T), axis=1)[:, :cap]
    gates = jnp.take_along_axis(aff_t, idx, axis=1)[..., None]
    xin = jnp.take(h2, idx.reshape(-1), axis=0).reshape(N_EXPERTS, cap, D_MODEL)
    y = _ffn(xin, p["w_gate"], p["w_up"], p["w_down"], gates, tm)
    out = x1.at[idx.reshape(-1)].add(y.reshape(-1, D_MODEL))
    return out.reshape(B, S, D_MODEL)


def kernel(x_prompt, x_sample, norm1_g, w_in, attn_qnorm_g, attn_knorm_g, ret_decay_fwd, ret_decay_bwd,
           ret_norm_g, w_out, norm2_g, w_router, w_gate_e, w_up_e, w_down_e):
    ctab = _multiplicity_table()
    y_prompt, y_sample = x_prompt, x_sample
    for l in range(norm1_g.shape[0]):
        p = {
            "g1": norm1_g[l][None, :],
            "w_in": w_in[l].astype(BF16),
            "gq": jnp.tile(attn_qnorm_g[l], LANES // ATT_HD)[None, :],
            "gk": jnp.tile(attn_knorm_g[l], LANES // ATT_HD)[None, :],
            "decays": jnp.stack([ret_decay_fwd[l], ret_decay_bwd[l]]).astype(F32),
            "gn": ret_norm_g[l][None, :].astype(F32),
            "w_out": w_out[l].astype(BF16),
            "g2": norm2_g[l][None, :],
            "w_router_t": w_router[l].T.astype(BF16),
            "w_gate": w_gate_e[l].astype(BF16),
            "w_up": w_up_e[l].astype(BF16),
            "w_down": w_down_e[l].astype(BF16),
        }
        y_prompt = _layer(y_prompt, p, ctab)
        y_sample = _layer(y_sample, p, ctab)
    return (y_prompt, y_sample)
```

```python
import functools

import jax
import jax.numpy as jnp
import numpy as np
from jax import lax
from jax.experimental import pallas as pl
from jax.experimental.pallas import tpu as pltpu

F32 = jnp.float32
BF16 = jnp.bfloat16

D_MODEL = 1024
ATT_HEADS, ATT_HD = 8, 64
RET_HEADS, RET_HD = 4, 128
ATT_W = ATT_HEADS * ATT_HD
RET_W = RET_HEADS * RET_HD
IN_W = 3 * ATT_W + 4 * RET_W
LANES = 128
N_SLABS = IN_W // LANES
GROUP_W = 512
SLABS_PER_GROUP = GROUP_W // LANES
N_ATT_SLABS = 3 * ATT_W // LANES
RET_CHUNK = 128
ROPE_THETA = 10000.0
N_EXPERTS = 16
D_FF = 2048
CAPACITY_FACTOR = 2
NORM_EPS = 1e-6
NEG_INF = -1e30
ATT_QBLK = 128
VMEM_LIMIT = 56 * 1024 * 1024


def _cparams(sem):
    return pltpu.CompilerParams(dimension_semantics=sem, vmem_limit_bytes=VMEM_LIMIT)


def _inproj_kernel(x_ref, g1_ref, w_ref, gq_ref, gk_ref, ca_ref, sa_ref, cr_ref, sr_ref, oa_ref, or_ref):
    x = x_ref[...]
    ms = jnp.mean(x * x, axis=-1, keepdims=True)
    h = (x * lax.rsqrt(ms + NORM_EPS) * g1_ref[...]).astype(BF16)
    tm = x.shape[0]
    lane = lax.broadcasted_iota(jnp.int32, (tm, LANES), 1)
    first = lane < ATT_HD
    low_half = (lane & (ATT_HD // 2)) == 0
    ca, sa, cr, sr = ca_ref[...], sa_ref[...], cr_ref[...], sr_ref[...]
    for grp in range(IN_W // GROUP_W):
        acc = jnp.dot(h, w_ref[:, grp * GROUP_W:(grp + 1) * GROUP_W], preferred_element_type=F32)
        for p in range(SLABS_PER_GROUP):
            a = acc[:, p * LANES:(p + 1) * LANES]
            if grp in (0, 1):
                sq = a * a
                s0 = jnp.sum(jnp.where(first, sq, 0.0), axis=-1, keepdims=True)
                s1 = jnp.sum(jnp.where(first, 0.0, sq), axis=-1, keepdims=True)
                ms2 = jnp.where(first, s0, s1) * (1.0 / ATT_HD)
                g = gq_ref[...] if grp == 0 else gk_ref[...]
                y = a * lax.rsqrt(ms2 + NORM_EPS) * g
                partner = jnp.where(low_half, pltpu.roll(y, LANES - ATT_HD // 2, 1),
                                    pltpu.roll(y, ATT_HD // 2, 1))
                r = y * ca + partner * sa
                if grp == 0:
                    r = r * (ATT_HD ** -0.5)
            elif grp in (3, 4):
                r = a * cr + pltpu.roll(a, RET_HD // 2, 1) * sr
                if grp == 4:
                    r = r * (RET_HD ** -0.5)
            else:
                r = a
            slab = grp * SLABS_PER_GROUP + p
            if slab < N_ATT_SLABS:
                oa_ref[slab] = r
            else:
                or_ref[slab - N_ATT_SLABS] = r.astype(BF16)


def _rope_tables(S):
    pos = jnp.arange(S, dtype=F32)

    def table(hd):
        inv_freq = ROPE_THETA ** (-jnp.arange(0, hd, 2, dtype=F32) / hd)
        ang = pos[:, None] * inv_freq[None, :]
        cos, sin = jnp.cos(ang), jnp.sin(ang)
        reps = LANES // hd
        cos_t = jnp.tile(jnp.concatenate([cos, cos], axis=-1), (1, reps))
        sin_t = jnp.tile(jnp.concatenate([-sin, sin], axis=-1), (1, reps))
        return cos_t, sin_t

    ca, sa = table(ATT_HD)
    cr, sr = table(RET_HD)
    return ca, sa, cr, sr


def _inproj(x2d, S, g1, w_in_bf, gq, gk, tables, tm):
    T = x2d.shape[0]
    n_pos_blk = S // tm
    tab_spec = pl.BlockSpec((tm, LANES), lambda i: (i % n_pos_blk, 0))
    full = lambda shape: pl.BlockSpec(shape, lambda i: (0,) * len(shape))
    return pl.pallas_call(
        _inproj_kernel,
        grid=(T // tm,),
        in_specs=[pl.BlockSpec((tm, D_MODEL), lambda i: (i, 0)), full((1, D_MODEL)),
                  full((D_MODEL, IN_W)), full((1, LANES)), full((1, LANES)),
                  tab_spec, tab_spec, tab_spec, tab_spec],
        out_specs=[pl.BlockSpec((N_ATT_SLABS, tm, LANES), lambda i: (0, i, 0)),
                   pl.BlockSpec((N_SLABS - N_ATT_SLABS, tm, LANES), lambda i: (0, i, 0))],
        out_shape=[jax.ShapeDtypeStruct((N_ATT_SLABS, T, LANES), F32),
                   jax.ShapeDtypeStruct((N_SLABS - N_ATT_SLABS, T, LANES), BF16)],
        compiler_params=_cparams(("parallel",)),
        name="inproj",
    )(x2d, g1, w_in_bf, gq, gk, *tables)


DILATED_PATTERNS = ((128, 1), (512, 4), (2048, 16))
ATT_HALF = 64
ATT_SB = 2048
ATT_NB = ATT_SB // ATT_QBLK


def _attn_kernel(q_ref, k_ref, v_ref, o_ref, o_scr, lse_scr, *, S):
    t0 = pl.program_id(2) * ATT_SB
    lane = lax.broadcasted_iota(jnp.int32, (ATT_QBLK, LANES), 1)
    first = lane < ATT_HD
    nt_dims = (((1,), (1,)), ((), ()))

    def band(qv, kv, vv, off):
        kw = kv.shape[0]
        rel = (lax.broadcasted_iota(jnp.int32, (ATT_QBLK, kw), 0)
               - lax.broadcasted_iota(jnp.int32, (ATT_QBLK, kw), 1)) + off
        valid = jnp.abs(rel) <= ATT_HALF
        zero = jnp.zeros_like(qv)
        outs, lses = [], []
        for h in range(2):
            qm = jnp.where(first, qv, zero) if h == 0 else jnp.where(first, zero, qv)
            s = lax.dot_general(qm, kv, nt_dims, preferred_element_type=F32)
            s = jnp.where(valid, s, NEG_INF)
            m = jnp.max(s, axis=-1, keepdims=True)
            p = jnp.exp(s - m)
            l = jnp.sum(p, axis=-1, keepdims=True)
            acc = jnp.dot(p.astype(BF16), vv, preferred_element_type=F32)
            outs.append(acc * (1.0 / l))
            lses.append(m + jnp.log(l))
        return jnp.where(first, outs[0], outs[1]), jnp.where(first, lses[0], lses[1])

    def step(n, carry):
        for pi, (window, dil) in enumerate(DILATED_PATTERNS):
            L = S // dil
            kw = min(2 * ATT_QBLK, L)
            per_class = ATT_SB // dil // ATT_QBLK
            r = n // per_class
            bi = n % per_class
            lq0 = t0 // dil + bi * ATT_QBLK
            lk0 = jnp.clip(lq0 - ATT_HALF, 0, L - kw)
            qrow = dil * ATT_QBLK * bi + r
            krow = dil * lk0 + r
            if dil == 1:
                qs = pl.ds(pl.multiple_of(qrow, ATT_QBLK), ATT_QBLK)
                ks = pl.ds(pl.multiple_of(krow, 8), kw)
            else:
                qs = pl.ds(qrow, ATT_QBLK, stride=dil)
                ks = pl.ds(krow, kw, stride=dil)
            o, lse = band(q_ref[qs, :].astype(BF16), k_ref[ks, :].astype(BF16), v_ref[ks, :].astype(BF16),
                          lq0 - lk0)
            o_scr[pi, qs, :] = o
            lse_scr[pi, qs, :] = lse
        return carry

    lax.fori_loop(0, ATT_NB, step, 0)

    def merge(c, carry):
        sl = pl.ds(pl.multiple_of(c * ATT_QBLK, ATT_QBLK), ATT_QBLK)
        lses = [lse_scr[pi, sl, :] for pi in range(3)]
        mx = jnp.maximum(jnp.maximum(lses[0], lses[1]), lses[2])
        ws = [jnp.exp(x - mx) for x in lses]
        num = ws[0] * o_scr[0, sl, :] + ws[1] * o_scr[1, sl, :] + ws[2] * o_scr[2, sl, :]
        o_ref[sl, :] = (num / (ws[0] + ws[1] + ws[2])).astype(o_ref.dtype)
        return carry

    lax.fori_loop(0, ATT_NB, merge, 0)


def _attention(qkv4, B, S):
    n_pairs = ATT_W // LANES
    kv_spec = lambda off: pl.BlockSpec((None, None, S, LANES), lambda b, p, i: (off + p, b, 0, 0))
    return pl.pallas_call(
        functools.partial(_attn_kernel, S=S),
        grid=(B, n_pairs, S // ATT_SB),
        in_specs=[pl.BlockSpec((None, None, ATT_SB, LANES), lambda b, p, i: (p, b, i, 0)),
                  kv_spec(n_pairs), kv_spec(2 * n_pairs)],
        out_specs=pl.BlockSpec((None, ATT_SB, LANES), lambda b, p, i: (b, i, p)),
        out_shape=jax.ShapeDtypeStruct((B, S, ATT_W), BF16),
        scratch_shapes=[pltpu.VMEM((3, ATT_SB, LANES), F32), pltpu.VMEM((3, ATT_SB, LANES), F32)],
        compiler_params=_cparams(("parallel", "parallel", "arbitrary")),
        name="dilated_attn",
    )(qkv4, qkv4, qkv4)


def _ret_kernel(dec_ref, q_ref, k_ref, v_ref, g_ref, gn_ref, o_ref, of_scr, *, S):
    C = RET_CHUNK
    nc = S // C
    h = pl.program_id(1)
    row = lax.broadcasted_iota(jnp.int32, (C, C), 0).astype(F32)
    col = lax.broadcasted_iota(jnp.int32, (C, C), 1).astype(F32)
    nt_dims = (((1,), (1,)), ((), ()))
    tn_dims = (((0,), (0,)), ((), ()))

    def consts(direction):
        log_g = -jnp.exp(jnp.full((C, C), dec_ref[direction, h], F32))
        if direction == 0:
            diff = row - col
            mask = diff >= 0.0
            k_dec, q_dec = jnp.exp(log_g * (C - 1.0 - row)), jnp.exp(log_g * (row + 1.0))
        else:
            diff = col - row
            mask = diff > 0.0
            k_dec, q_dec = jnp.exp(log_g * row), jnp.exp(log_g * (C - row))
        dmat = jnp.where(mask, jnp.exp(log_g * jnp.maximum(diff, 0.0)), 0.0)
        return dmat, k_dec, q_dec, jnp.exp(log_g * C)

    def chunk_out(c, state, dmat, k_dec, q_dec, g_chunk):
        sl = pl.ds(pl.multiple_of(c * C, C), C)
        qc, kc, vc = q_ref[sl, :], k_ref[sl, :], v_ref[sl, :]
        qk = lax.dot_general(qc, kc, nt_dims, preferred_element_type=F32)
        inner = (qk * dmat).astype(BF16)
        o = jnp.dot(inner, vc, preferred_element_type=F32)
        o = o + jnp.dot((qc.astype(F32) * q_dec).astype(BF16), state.astype(BF16),
                        preferred_element_type=F32)
        kd = (kc.astype(F32) * k_dec).astype(BF16)
        kv = lax.dot_general(kd, vc, tn_dims, preferred_element_type=F32)
        return sl, o, state * g_chunk + kv

    cf = consts(0)

    def fwd(c, state):
        sl, o, state = chunk_out(c, state, *cf)
        of_scr[sl, :] = o
        return state

    lax.fori_loop(0, nc, fwd, jnp.zeros((C, C), F32), unroll=2)

    cb = consts(1)
    gn = gn_ref[...]

    def bwd(i, state):
        sl, o, state = chunk_out(nc - 1 - i, state, *cb)
        o = o + of_scr[sl, :]
        mu = jnp.mean(o, axis=-1, keepdims=True)
        var = jnp.mean(jnp.square(o - mu), axis=-1, keepdims=True)
        y = (o - mu) * lax.rsqrt(var + NORM_EPS) * gn
        g = g_ref[sl, :].astype(F32)
        o_ref[sl, :] = (y * (g * (1.0 / (1.0 + jnp.exp(-g))))).astype(o_ref.dtype)
        return state

    lax.fori_loop(0, nc, bwd, jnp.zeros((C, C), F32), unroll=2)


def _retention(qkv4, decays, gn, B, S):
    spec = lambda off: pl.BlockSpec((None, None, S, LANES), lambda b, h: (off + h, b, 0, 0))
    return pl.pallas_call(
        functools.partial(_ret_kernel, S=S),
        grid=(B, RET_HEADS),
        in_specs=[pl.BlockSpec(memory_space=pltpu.SMEM),
                  spec(0), spec(RET_HEADS), spec(2 * RET_HEADS), spec(3 * RET_HEADS),
                  pl.BlockSpec((1, LANES), lambda b, h: (0, h))],
        out_specs=pl.BlockSpec((None, S, LANES), lambda b, h: (b, 0, h)),
        out_shape=jax.ShapeDtypeStruct((B, S, RET_W), BF16),
        scratch_shapes=[pltpu.VMEM((S, LANES), F32)],
        compiler_params=_cparams(("parallel", "arbitrary")),
        name="retention",
    )(decays, qkv4, qkv4, qkv4, qkv4, gn)


def _outproj_kernel(x_ref, att_ref, ret_ref, wo_ref, g2_ref, wr_ref, x1_ref, h2_ref, aff_ref):
    y = jnp.dot(att_ref[...], wo_ref[:ATT_W, :], preferred_element_type=F32)
    y = y + jnp.dot(ret_ref[...], wo_ref[ATT_W:, :], preferred_element_type=F32)
    x1 = x_ref[...] + y
    x1_ref[...] = x1
    ms = jnp.mean(x1 * x1, axis=-1, keepdims=True)
    h2 = x1 * lax.rsqrt(ms + NORM_EPS) * g2_ref[...]
    h2_ref[...] = h2.astype(h2_ref.dtype)
    logits = lax.dot_general(wr_ref[...], h2.astype(BF16), (((1,), (1,)), ((), ())),
                             preferred_element_type=F32)
    e = jnp.exp(logits - jnp.max(logits, axis=0, keepdims=True))
    aff_ref[...] = e / jnp.sum(e, axis=0, keepdims=True)


def _outproj(x2d, att, ret, wo_bf, g2, wr_t_bf, tm):
    T = x2d.shape[0]
    full = lambda shape: pl.BlockSpec(shape, lambda i: (0,) * len(shape))
    return pl.pallas_call(
        _outproj_kernel,
        grid=(T // tm,),
        in_specs=[pl.BlockSpec((tm, D_MODEL), lambda i: (i, 0)),
                  pl.BlockSpec((tm, ATT_W), lambda i: (i, 0)),
                  pl.BlockSpec((tm, RET_W), lambda i: (i, 0)),
                  full((ATT_W + RET_W, D_MODEL)), full((1, D_MODEL)), full((N_EXPERTS, D_MODEL))],
        out_specs=[pl.BlockSpec((tm, D_MODEL), lambda i: (i, 0)),
                   pl.BlockSpec((tm, D_MODEL), lambda i: (i, 0)),
                   pl.BlockSpec((N_EXPERTS, tm), lambda i: (0, i))],
        out_shape=[jax.ShapeDtypeStruct((T, D_MODEL), F32),
                   jax.ShapeDtypeStruct((T, D_MODEL), BF16),
                   jax.ShapeDtypeStruct((N_EXPERTS, T), F32)],
        compiler_params=_cparams(("parallel",)),
        name="outproj_router",
    )(x2d, att, ret, wo_bf, g2, wr_t_bf)


def _select_kernel(aff_ref, sel_ref, *, cap):
    bits = pltpu.bitcast(aff_ref[...], jnp.int32)
    E, T = bits.shape
    capf = jnp.float32(cap)

    def count(mask):
        return jnp.sum(jnp.where(mask, 1.0, 0.0), axis=1, keepdims=True)

    def thr_step(i, thr):
        cand = thr | jnp.left_shift(jnp.int32(1), 30 - i)
        return jnp.where(count(bits >= cand) >= capf, cand, thr)

    thr = lax.fori_loop(0, 31, thr_step, jnp.zeros((E, 1), jnp.int32))
    above = bits > thr
    ties = bits == thr
    need = capf - count(above)
    idx = lax.broadcasted_iota(jnp.int32, (E, T), 1)
    n_idx_bits = max(1, (T - 1).bit_length())

    def cut_step(i, cut):
        cand = cut | jnp.left_shift(jnp.int32(1), n_idx_bits - 1 - i)
        return jnp.where(count(ties & (idx < cand)) < need, cand, cut)

    cut = lax.fori_loop(0, n_idx_bits, cut_step, jnp.zeros((E, 1), jnp.int32))
    sel_ref[...] = jnp.where(above | (ties & (idx <= cut)), 1, 0).astype(jnp.int32)


def _select(aff_t, cap):
    E, T = aff_t.shape
    return pl.pallas_call(
        functools.partial(_select_kernel, cap=cap),
        grid=(1,),
        in_specs=[pl.BlockSpec((E, T), lambda i: (0, 0))],
        out_specs=pl.BlockSpec((E, T), lambda i: (0, 0)),
        out_shape=jax.ShapeDtypeStruct((E, T), jnp.int32),
        compiler_params=_cparams(("arbitrary",)),
        name="expert_select",
    )(aff_t)


FF_CHUNK = 512


def _ffn_kernel(xin_ref, wg_ref, wu_ref, wd_ref, gate_ref, y_ref):
    xin = xin_ref[...]
    acc = jnp.zeros(y_ref.shape, F32)
    for f in range(D_FF // FF_CHUNK):
        fs = slice(f * FF_CHUNK, (f + 1) * FF_CHUNK)
        a = jnp.dot(xin, wg_ref[:, fs], preferred_element_type=F32)
        u = jnp.dot(xin, wu_ref[:, fs], preferred_element_type=F32)
        hmid = (a * (1.0 / (1.0 + jnp.exp(-a))) * u).astype(BF16)
        acc = acc + jnp.dot(hmid, wd_ref[fs, :], preferred_element_type=F32)
    y_ref[...] = (acc * gate_ref[...]).astype(y_ref.dtype)


def _ffn(xin, wg_bf, wu_bf, wd_bf, gates, tm):
    E, cap, _ = xin.shape
    return pl.pallas_call(
        _ffn_kernel,
        grid=(E, cap // tm),
        in_specs=[pl.BlockSpec((None, tm, D_MODEL), lambda e, j: (e, j, 0)),
                  pl.BlockSpec((None, D_MODEL, D_FF), lambda e, j: (e, 0, 0)),
                  pl.BlockSpec((None, D_MODEL, D_FF), lambda e, j: (e, 0, 0)),
                  pl.BlockSpec((None, D_FF, D_MODEL), lambda e, j: (e, 0, 0)),
                  pl.BlockSpec((None, tm, 1), lambda e, j: (e, j, 0))],
        out_specs=pl.BlockSpec((None, tm, D_MODEL), lambda e, j: (e, j, 0)),
        out_shape=jax.ShapeDtypeStruct((E, cap, D_MODEL), F32),
        compiler_params=_cparams(("parallel", "arbitrary")),
        name="expert_ffn",
    )(xin, wg_bf, wu_bf, wd_bf, gates)


def _layer(x, p):
    B, S, _ = x.shape
    T = B * S
    tm = 512
    x2d = x.reshape(T, D_MODEL)
    qkv_att, qkv_ret = _inproj(x2d, S, p["g1"], p["w_in"], p["gq"], p["gk"], _rope_tables(S), tm)
    att = _attention(qkv_att.reshape(N_ATT_SLABS, B, S, LANES), B, S).reshape(T, ATT_W)
    ret = _retention(qkv_ret.reshape(N_SLABS - N_ATT_SLABS, B, S, LANES), p["decays"], p["gn"], B, S).reshape(T, RET_W)
    x1, h2, aff_t = _outproj(x2d, att, ret, p["w_out"], p["g2"], p["w_router_t"], tm)

    cap = CAPACITY_FACTOR * T // N_EXPERTS
    sel = _select(aff_t, cap)
    tok = jnp.arange(T, dtype=jnp.int32)[None, :]
    idx = jnp.sort(jnp.where(sel > 0, tok, T), axis=1)[:, :cap]
    gates = jnp.take_along_axis(aff_t, idx, axis=1)[..., None]
    xin = jnp.take(h2, idx.reshape(-1), axis=0).reshape(N_EXPERTS, cap, D_MODEL)
    y = _ffn(xin, p["w_gate"], p["w_up"], p["w_down"], gates, tm)
    out = x1.at[idx.reshape(-1)].add(y.reshape(-1, D_MODEL))
    return out.reshape(B, S, D_MODEL)


def kernel(x_prompt, x_sample, norm1_g, w_in, attn_qnorm_g, attn_knorm_g, ret_decay_fwd, ret_decay_bwd,
           ret_norm_g, w_out, norm2_g, w_router, w_gate_e, w_up_e, w_down_e):
    y_prompt, y_sample = x_prompt, x_sample
    for l in range(norm1_g.shape[0]):
        p = {
            "g1": norm1_g[l][None, :],
            "w_in": w_in[l].astype(BF16),
            "gq": jnp.tile(attn_qnorm_g[l], LANES // ATT_HD)[None, :],
            "gk": jnp.tile(attn_knorm_g[l], LANES // ATT_HD)[None, :],
            "decays": jnp.stack([ret_decay_fwd[l], ret_decay_bwd[l]]).astype(F32),
            "gn": ret_norm_g[l][None, :].astype(F32),
            "w_out": w_out[l].astype(BF16),
            "g2": norm2_g[l][None, :],
            "w_router_t": w_router[l].T.astype(BF16),
            "w_gate": w_gate_e[l].astype(BF16),
            "w_up": w_up_e[l].astype(BF16),
            "w_down": w_down_e[l].astype(BF16),
        }
        y_prompt = _layer(y_prompt, p)
        y_sample = _layer(y_sample, p)
    return (y_prompt, y_sample)
```

```python
import functools

import jax
import jax.numpy as jnp
import numpy as np
from jax import lax
from jax.experimental import pallas as pl
from jax.experimental.pallas import tpu as pltpu

F32 = jnp.float32
BF16 = jnp.bfloat16

D_MODEL = 1024
ATT_HEADS, ATT_HD = 8, 64
RET_HEADS, RET_HD = 4, 128
ATT_W = ATT_HEADS * ATT_HD
RET_W = RET_HEADS * RET_HD
IN_W = 3 * ATT_W + 4 * RET_W
LANES = 128
N_SLABS = IN_W // LANES
GROUP_W = 512
SLABS_PER_GROUP = GROUP_W // LANES
N_ATT_SLABS = 3 * ATT_W // LANES
RET_CHUNK = 128
ROPE_THETA = 10000.0
N_EXPERTS = 16
D_FF = 2048
CAPACITY_FACTOR = 2
NORM_EPS = 1e-6
NEG_INF = -1e30
ATT_QBLK = 128
VMEM_LIMIT = 56 * 1024 * 1024


def _cparams(sem):
    return pltpu.CompilerParams(dimension_semantics=sem, vmem_limit_bytes=VMEM_LIMIT)


def _inproj_kernel(x_ref, g1_ref, w_ref, gq_ref, gk_ref, ca_ref, sa_ref, cr_ref, sr_ref, oa_ref, or_ref):
    x = x_ref[...]
    ms = jnp.mean(x * x, axis=-1, keepdims=True)
    h = (x * lax.rsqrt(ms + NORM_EPS) * g1_ref[...]).astype(BF16)
    tm = x.shape[0]
    lane = lax.broadcasted_iota(jnp.int32, (tm, LANES), 1)
    first = lane < ATT_HD
    low_half = (lane & (ATT_HD // 2)) == 0
    ca, sa, cr, sr = ca_ref[...], sa_ref[...], cr_ref[...], sr_ref[...]
    for grp in range(IN_W // GROUP_W):
        acc = jnp.dot(h, w_ref[:, grp * GROUP_W:(grp + 1) * GROUP_W], preferred_element_type=F32)
        for p in range(SLABS_PER_GROUP):
            a = acc[:, p * LANES:(p + 1) * LANES]
            if grp in (0, 1):
                sq = a * a
                s0 = jnp.sum(jnp.where(first, sq, 0.0), axis=-1, keepdims=True)
                s1 = jnp.sum(jnp.where(first, 0.0, sq), axis=-1, keepdims=True)
                ms2 = jnp.where(first, s0, s1) * (1.0 / ATT_HD)
                g = gq_ref[...] if grp == 0 else gk_ref[...]
                y = a * lax.rsqrt(ms2 + NORM_EPS) * g
                partner = jnp.where(low_half, pltpu.roll(y, LANES - ATT_HD // 2, 1),
                                    pltpu.roll(y, ATT_HD // 2, 1))
                r = y * ca + partner * sa
                if grp == 0:
                    r = r * (ATT_HD ** -0.5)
            elif grp in (3, 4):
                r = a * cr + pltpu.roll(a, RET_HD // 2, 1) * sr
                if grp == 4:
                    r = r * (RET_HD ** -0.5)
            else:
                r = a
            slab = grp * SLABS_PER_GROUP + p
            if slab < N_ATT_SLABS:
                oa_ref[slab] = r
            else:
                or_ref[slab - N_ATT_SLABS] = r.astype(BF16)


def _rope_tables(S):
    pos = jnp.arange(S, dtype=F32)

    def table(hd):
        inv_freq = ROPE_THETA ** (-jnp.arange(0, hd, 2, dtype=F32) / hd)
        ang = pos[:, None] * inv_freq[None, :]
        cos, sin = jnp.cos(ang), jnp.sin(ang)
        reps = LANES // hd
        cos_t = jnp.tile(jnp.concatenate([cos, cos], axis=-1), (1, reps))
        sin_t = jnp.tile(jnp.concatenate([-sin, sin], axis=-1), (1, reps))
        return cos_t, sin_t

    ca, sa = table(ATT_HD)
    cr, sr = table(RET_HD)
    return ca, sa, cr, sr


def _inproj(x2d, S, g1, w_in_bf, gq, gk, tables, tm):
    T = x2d.shape[0]
    n_pos_blk = S // tm
    tab_spec = pl.BlockSpec((tm, LANES), lambda i: (i % n_pos_blk, 0))
    full = lambda shape: pl.BlockSpec(shape, lambda i: (0,) * len(shape))
    return pl.pallas_call(
        _inproj_kernel,
        grid=(T // tm,),
        in_specs=[pl.BlockSpec((tm, D_MODEL), lambda i: (i, 0)), full((1, D_MODEL)),
                  full((D_MODEL, IN_W)), full((1, LANES)), full((1, LANES)),
                  tab_spec, tab_spec, tab_spec, tab_spec],
        out_specs=[pl.BlockSpec((N_ATT_SLABS, tm, LANES), lambda i: (0, i, 0)),
                   pl.BlockSpec((N_SLABS - N_ATT_SLABS, tm, LANES), lambda i: (0, i, 0))],
        out_shape=[jax.ShapeDtypeStruct((N_ATT_SLABS, T, LANES), F32),
                   jax.ShapeDtypeStruct((N_SLABS - N_ATT_SLABS, T, LANES), BF16)],
        compiler_params=_cparams(("parallel",)),
        name="inproj",
    )(x2d, g1, w_in_bf, gq, gk, *tables)


DILATED_PATTERNS = ((128, 1), (512, 4), (2048, 16))
ATT_HALF = 64
ATT_SB = 2048
ATT_NB = ATT_SB // ATT_QBLK


def _attn_kernel(q_ref, k_ref, v_ref, o_ref, o_scr, lse_scr, *, S):
    t0 = pl.program_id(2) * ATT_SB
    lane = lax.broadcasted_iota(jnp.int32, (ATT_QBLK, LANES), 1)
    first = lane < ATT_HD
    nt_dims = (((1,), (1,)), ((), ()))

    def band(qv, kv, vv, off):
        kw = kv.shape[0]
        rel = (lax.broadcasted_iota(jnp.int32, (ATT_QBLK, kw), 0)
               - lax.broadcasted_iota(jnp.int32, (ATT_QBLK, kw), 1)) + off
        valid = jnp.abs(rel) <= ATT_HALF
        zero = jnp.zeros_like(qv)
        outs, lses = [], []
        for h in range(2):
            qm = jnp.where(first, qv, zero) if h == 0 else jnp.where(first, zero, qv)
            s = lax.dot_general(qm, kv, nt_dims, preferred_element_type=F32)
            s = jnp.where(valid, s, NEG_INF)
            m = jnp.max(s, axis=-1, keepdims=True)
            p = jnp.exp(s - m)
            l = jnp.sum(p, axis=-1, keepdims=True)
            acc = jnp.dot(p.astype(BF16), vv, preferred_element_type=F32)
            outs.append(acc * (1.0 / l))
            lses.append(m + jnp.log(l))
        return jnp.where(first, outs[0], outs[1]), jnp.where(first, lses[0], lses[1])

    def step(n, carry):
        for pi, (window, dil) in enumerate(DILATED_PATTERNS):
            L = S // dil
            kw = min(2 * ATT_QBLK, L)
            per_class = ATT_SB // dil // ATT_QBLK
            r = n // per_class
            bi = n % per_class
            lq0 = t0 // dil + bi * ATT_QBLK
            lk0 = jnp.clip(lq0 - ATT_HALF, 0, L - kw)
            qrow = dil * ATT_QBLK * bi + r
            krow = dil * lk0 + r
            if dil == 1:
                qs = pl.ds(pl.multiple_of(qrow, ATT_QBLK), ATT_QBLK)
                ks = pl.ds(pl.multiple_of(krow, 8), kw)
            else:
                qs = pl.ds(qrow, ATT_QBLK, stride=dil)
                ks = pl.ds(krow, kw, stride=dil)
            o, lse = band(q_ref[qs, :].astype(BF16), k_ref[ks, :].astype(BF16), v_ref[ks, :].astype(BF16),
                          lq0 - lk0)
            o_scr[pi, qs, :] = o
            lse_scr[pi, qs, :] = lse
        return carry

    lax.fori_loop(0, ATT_NB, step, 0)

    def merge(c, carry):
        sl = pl.ds(pl.multiple_of(c * ATT_QBLK, ATT_QBLK), ATT_QBLK)
        lses = [lse_scr[pi, sl, :] for pi in range(3)]
        mx = jnp.maximum(jnp.maximum(lses[0], lses[1]), lses[2])
        ws = [jnp.exp(x - mx) for x in lses]
        num = ws[0] * o_scr[0, sl, :] + ws[1] * o_scr[1, sl, :] + ws[2] * o_scr[2, sl, :]
        o_ref[sl, :] = (num / (ws[0] + ws[1] + ws[2])).astype(o_ref.dtype)
        return carry

    lax.fori_loop(0, ATT_NB, merge, 0)


def _attention(qkv4, B, S):
    n_pairs = ATT_W // LANES
    kv_spec = lambda off: pl.BlockSpec((None, None, S, LANES), lambda b, p, i: (off + p, b, 0, 0))
    return pl.pallas_call(
        functools.partial(_attn_kernel, S=S),
        grid=(B, n_pairs, S // ATT_SB),
        in_specs=[pl.BlockSpec((None, None, ATT_SB, LANES), lambda b, p, i: (p, b, i, 0)),
                  kv_spec(n_pairs), kv_spec(2 * n_pairs)],
        out_specs=pl.BlockSpec((None, ATT_SB, LANES), lambda b, p, i: (b, i, p)),
        out_shape=jax.ShapeDtypeStruct((B, S, ATT_W), BF16),
        scratch_shapes=[pltpu.VMEM((3, ATT_SB, LANES), F32), pltpu.VMEM((3, ATT_SB, LANES), F32)],
        compiler_params=_cparams(("parallel", "parallel", "arbitrary")),
        name="dilated_attn",
    )(qkv4, qkv4, qkv4)


def _ret_kernel(dec_ref, q_ref, k_ref, v_ref, g_ref, gn_ref, o_ref, of_scr, *, S):
    C = RET_CHUNK
    nc = S // C
    h = pl.program_id(1)
    row = lax.broadcasted_iota(jnp.int32, (C, C), 0).astype(F32)
    col = lax.broadcasted_iota(jnp.int32, (C, C), 1).astype(F32)
    nt_dims = (((1,), (1,)), ((), ()))
    tn_dims = (((0,), (0,)), ((), ()))

    def consts(direction):
        log_g = -jnp.exp(jnp.full((C, C), dec_ref[direction, h], F32))
        if direction == 0:
            diff = row - col
            mask = diff >= 0.0
            k_dec, q_dec = jnp.exp(log_g * (C - 1.0 - row)), jnp.exp(log_g * (row + 1.0))
        else:
            diff = col - row
            mask = diff > 0.0
            k_dec, q_dec = jnp.exp(log_g * row), jnp.exp(log_g * (C - row))
        dmat = jnp.where(mask, jnp.exp(log_g * jnp.maximum(diff, 0.0)), 0.0)
        return dmat, k_dec, q_dec, jnp.exp(log_g * C)

    def chunk_out(c, state, dmat, k_dec, q_dec, g_chunk):
        sl = pl.ds(pl.multiple_of(c * C, C), C)
        qc, kc, vc = q_ref[sl, :], k_ref[sl, :], v_ref[sl, :]
        qk = lax.dot_general(qc, kc, nt_dims, preferred_element_type=F32)
        inner = (qk * dmat).astype(BF16)
        o = jnp.dot(inner, vc, preferred_element_type=F32)
        o = o + jnp.dot((qc.astype(F32) * q_dec).astype(BF16), state.astype(BF16),
                        preferred_element_type=F32)
        kd = (kc.astype(F32) * k_dec).astype(BF16)
        kv = lax.dot_general(kd, vc, tn_dims, preferred_element_type=F32)
        return sl, o, state * g_chunk + kv

    cf = consts(0)

    def fwd(c, state):
        sl, o, state = chunk_out(c, state, *cf)
        of_scr[sl, :] = o
        return state

    lax.fori_loop(0, nc, fwd, jnp.zeros((C, C), F32), unroll=2)

    cb = consts(1)
    gn = gn_ref[...]

    def bwd(i, state):
        sl, o, state = chunk_out(nc - 1 - i, state, *cb)
        o = o + of_scr[sl, :]
        mu = jnp.mean(o, axis=-1, keepdims=True)
        var = jnp.mean(jnp.square(o - mu), axis=-1, keepdims=True)
        y = (o - mu) * lax.rsqrt(var + NORM_EPS) * gn
        g = g_ref[sl, :].astype(F32)
        o_ref[sl, :] = (y * (g * (1.0 / (1.0 + jnp.exp(-g))))).astype(o_ref.dtype)
        return state

    lax.fori_loop(0, nc, bwd, jnp.zeros((C, C), F32), unroll=2)


def _retention(qkv4, decays, gn, B, S):
    spec = lambda off: pl.BlockSpec((None, None, S, LANES), lambda b, h: (off + h, b, 0, 0))
    return pl.pallas_call(
        functools.partial(_ret_kernel, S=S),
        grid=(B, RET_HEADS),
        in_specs=[pl.BlockSpec(memory_space=pltpu.SMEM),
                  spec(0), spec(RET_HEADS), spec(2 * RET_HEADS), spec(3 * RET_HEADS),
                  pl.BlockSpec((1, LANES), lambda b, h: (0, h))],
        out_specs=pl.BlockSpec((None, S, LANES), lambda b, h: (b, 0, h)),
        out_shape=jax.ShapeDtypeStruct((B, S, RET_W), BF16),
        scratch_shapes=[pltpu.VMEM((S, LANES), F32)],
        compiler_params=_cparams(("parallel", "arbitrary")),
        name="retention",
    )(decays, qkv4, qkv4, qkv4, qkv4, gn)


def _outproj_kernel(x_ref, att_ref, ret_ref, wo_ref, g2_ref, wr_ref, x1_ref, h2_ref, aff_ref):
    y = jnp.dot(att_ref[...], wo_ref[:ATT_W, :], preferred_element_type=F32)
    y = y + jnp.dot(ret_ref[...], wo_ref[ATT_W:, :], preferred_element_type=F32)
    x1 = x_ref[...] + y
    x1_ref[...] = x1
    ms = jnp.mean(x1 * x1, axis=-1, keepdims=True)
    h2 = x1 * lax.rsqrt(ms + NORM_EPS) * g2_ref[...]
    h2_ref[...] = h2.astype(h2_ref.dtype)
    logits = lax.dot_general(wr_ref[...], h2.astype(BF16), (((1,), (1,)), ((), ())),
                             preferred_element_type=F32)
    e = jnp.exp(logits - jnp.max(logits, axis=0, keepdims=True))
    aff_ref[...] = e / jnp.sum(e, axis=0, keepdims=True)


def _outproj(x2d, att, ret, wo_bf, g2, wr_t_bf, tm):
    T = x2d.shape[0]
    full = lambda shape: pl.BlockSpec(shape, lambda i: (0,) * len(shape))
    return pl.pallas_call(
        _outproj_kernel,
        grid=(T // tm,),
        in_specs=[pl.BlockSpec((tm, D_MODEL), lambda i: (i, 0)),
                  pl.BlockSpec((tm, ATT_W), lambda i: (i, 0)),
                  pl.BlockSpec((tm, RET_W), lambda i: (i, 0)),
                  full((ATT_W + RET_W, D_MODEL)), full((1, D_MODEL)), full((N_EXPERTS, D_MODEL))],
        out_specs=[pl.BlockSpec((tm, D_MODEL), lambda i: (i, 0)),
                   pl.BlockSpec((tm, D_MODEL), lambda i: (i, 0)),
                   pl.BlockSpec((N_EXPERTS, tm), lambda i: (0, i))],
        out_shape=[jax.ShapeDtypeStruct((T, D_MODEL), F32),
                   jax.ShapeDtypeStruct((T, D_MODEL), BF16),
                   jax.ShapeDtypeStruct((N_EXPERTS, T), F32)],
        compiler_params=_cparams(("parallel",)),
        name="outproj_router",
    )(x2d, att, ret, wo_bf, g2, wr_t_bf)


def _select_kernel(aff_ref, sel_ref, *, cap):
    bits = pltpu.bitcast(aff_ref[...], jnp.int32)
    E, T = bits.shape
    capf = jnp.float32(cap)

    def count(mask):
        return jnp.sum(jnp.where(mask, 1.0, 0.0), axis=1, keepdims=True)

    def thr_step(i, thr):
        cand = thr | jnp.left_shift(jnp.int32(1), 30 - i)
        return jnp.where(count(bits >= cand) >= capf, cand, thr)

    thr = lax.fori_loop(0, 31, thr_step, jnp.zeros((E, 1), jnp.int32))
    above = bits > thr
    ties = bits == thr
    need = capf - count(above)
    idx = lax.broadcasted_iota(jnp.int32, (E, T), 1)
    n_idx_bits = max(1, (T - 1).bit_length())

    def cut_step(i, cut):
        cand = cut | jnp.left_shift(jnp.int32(1), n_idx_bits - 1 - i)
        return jnp.where(count(ties & (idx < cand)) < need, cand, cut)

    cut = lax.fori_loop(0, n_idx_bits, cut_step, jnp.zeros((E, 1), jnp.int32))
    sel_ref[...] = jnp.where(above | (ties & (idx <= cut)), 1, 0).astype(jnp.int32)


def _select(aff_t, cap):
    E, T = aff_t.shape
    return pl.pallas_call(
        functools.partial(_select_kernel, cap=cap),
        grid=(1,),
        in_specs=[pl.BlockSpec((E, T), lambda i: (0, 0))],
        out_specs=pl.BlockSpec((E, T), lambda i: (0, 0)),
        out_shape=jax.ShapeDtypeStruct((E, T), jnp.int32),
        compiler_params=_cparams(("arbitrary",)),
        name="expert_select",
    )(aff_t)


FF_CHUNK = 512


def _ffn_kernel(xin_ref, wg_ref, wu_ref, wd_ref, y_ref):
    xin = xin_ref[...]
    acc = jnp.zeros(y_ref.shape, F32)
    for f in range(D_FF // FF_CHUNK):
        fs = slice(f * FF_CHUNK, (f + 1) * FF_CHUNK)
        a = jnp.dot(xin, wg_ref[:, fs], preferred_element_type=F32)
        u = jnp.dot(xin, wu_ref[:, fs], preferred_element_type=F32)
        hmid = (a * (1.0 / (1.0 + jnp.exp(-a))) * u).astype(BF16)
        acc = acc + jnp.dot(hmid, wd_ref[fs, :], preferred_element_type=F32)
    y_ref[...] = acc.astype(y_ref.dtype)


def _ffn(xin, wg_bf, wu_bf, wd_bf, tm):
    E, cap, _ = xin.shape
    return pl.pallas_call(
        _ffn_kernel,
        grid=(E, cap // tm),
        in_specs=[pl.BlockSpec((None, tm, D_MODEL), lambda e, j: (e, j, 0)),
                  pl.BlockSpec((None, D_MODEL, D_FF), lambda e, j: (e, 0, 0)),
                  pl.BlockSpec((None, D_MODEL, D_FF), lambda e, j: (e, 0, 0)),
                  pl.BlockSpec((None, D_FF, D_MODEL), lambda e, j: (e, 0, 0))],
        out_specs=pl.BlockSpec((None, tm, D_MODEL), lambda e, j: (e, j, 0)),
        out_shape=jax.ShapeDtypeStruct((E, cap, D_MODEL), BF16),
        compiler_params=_cparams(("parallel", "arbitrary")),
        name="expert_ffn",
    )(xin, wg_bf, wu_bf, wd_bf)


CMB_TOK = 256
CMB_ROWS = 128
BF16_SUBLANES = 16


def _combine_kernel(excl_ref, cnt_ref, x1_ref, aff_ref, pos_ref, y_hbm, o_ref, ybuf, xbuf, sem, xsem, *, cap, nblk):
    b = pl.program_id(0)
    slot = b % 2
    lane = lax.broadcasted_iota(jnp.int32, (CMB_TOK, CMB_ROWS), 1)

    def first_row(e, blk):
        return (excl_ref[e * nblk + blk] // BF16_SUBLANES) * BF16_SUBLANES

    def fetch(e, row, dst, s):
        row = pl.multiple_of(jnp.minimum(row, cap - CMB_ROWS), BF16_SUBLANES)
        return pltpu.make_async_copy(y_hbm.at[e, pl.ds(row, CMB_ROWS), :], dst, s)

    def start_block(blk, sl):
        for e in range(N_EXPERTS):
            fetch(e, first_row(e, blk), ybuf.at[sl, e], sem.at[sl, e]).start()

    @pl.when(b == 0)
    def _():
        start_block(0, 0)

    @pl.when(b + 1 < nblk)
    def _():
        start_block(b + 1, 1 - slot)

    def spread(e, pos_col, gate_col, rows, nominal):
        base = jnp.minimum(nominal, cap - CMB_ROWS)
        hit = ((pos_col - base) == lane) & (pos_col >= nominal)
        onehot = jnp.where(hit, 1.0, 0.0).astype(BF16)
        return jnp.dot(onehot, rows, preferred_element_type=F32) * gate_col

    o_ref[...] = x1_ref[...]
    for e in range(N_EXPERTS):
        nominal0 = first_row(e, b)
        need_end = excl_ref[e * nblk + b] + cnt_ref[e * nblk + b]
        pos_col = pos_ref[:, e:e + 1]
        gate_col = aff_ref[:, e:e + 1]
        fetch(e, nominal0, ybuf.at[slot, e], sem.at[slot, e]).wait()
        o_ref[...] += spread(e, pos_col, gate_col, ybuf[slot, e], nominal0)
        n_extra = jnp.maximum((need_end - nominal0 + CMB_ROWS - 1) // CMB_ROWS - 1, 0)

        def extra(k, carry):
            nominal = nominal0 + (k + 1) * CMB_ROWS
            cp = fetch(e, nominal, xbuf, xsem)
            cp.start()
            cp.wait()
            o_ref[...] += spread(e, pos_col, gate_col, xbuf[...], nominal)
            return carry

        lax.fori_loop(0, n_extra, extra, 0)


def _combine(x1, aff_tok, pos_tok, y, excl, cnt):
    T = x1.shape[0]
    E, cap, _ = y.shape
    nblk = T // CMB_TOK
    grid_spec = pltpu.PrefetchScalarGridSpec(
        num_scalar_prefetch=2,
        grid=(nblk,),
        in_specs=[pl.BlockSpec((CMB_TOK, D_MODEL), lambda b, *_: (b, 0)),
                  pl.BlockSpec((CMB_TOK, E), lambda b, *_: (b, 0)),
                  pl.BlockSpec((CMB_TOK, E), lambda b, *_: (b, 0)),
                  pl.BlockSpec(memory_space=pl.ANY)],
        out_specs=pl.BlockSpec((CMB_TOK, D_MODEL), lambda b, *_: (b, 0)),
        scratch_shapes=[pltpu.VMEM((2, E, CMB_ROWS, D_MODEL), BF16),
                        pltpu.VMEM((CMB_ROWS, D_MODEL), BF16),
                        pltpu.SemaphoreType.DMA((2, E)),
                        pltpu.SemaphoreType.DMA(())],
    )
    return pl.pallas_call(
        functools.partial(_combine_kernel, cap=cap, nblk=nblk),
        grid_spec=grid_spec,
        out_shape=jax.ShapeDtypeStruct((T, D_MODEL), F32),
        compiler_params=_cparams(("arbitrary",)),
        name="moe_combine",
    )(excl, cnt, x1, aff_tok, pos_tok, y)


def _layer(x, p):
    B, S, _ = x.shape
    T = B * S
    tm = 512
    x2d = x.reshape(T, D_MODEL)
    qkv_att, qkv_ret = _inproj(x2d, S, p["g1"], p["w_in"], p["gq"], p["gk"], _rope_tables(S), tm)
    att = _attention(qkv_att.reshape(N_ATT_SLABS, B, S, LANES), B, S).reshape(T, ATT_W)
    ret = _retention(qkv_ret.reshape(N_SLABS - N_ATT_SLABS, B, S, LANES), p["decays"], p["gn"], B, S).reshape(T, RET_W)
    x1, h2, aff_t = _outproj(x2d, att, ret, p["w_out"], p["g2"], p["w_router_t"], tm)

    cap = CAPACITY_FACTOR * T // N_EXPERTS
    sel = _select(aff_t, cap)
    tok = jnp.arange(T, dtype=jnp.int32)[None, :]
    idx = jnp.sort(jnp.where(sel > 0, tok, T), axis=1)[:, :cap]
    xin = jnp.take(h2, idx.reshape(-1), axis=0).reshape(N_EXPERTS, cap, D_MODEL)
    y = _ffn(xin, p["w_gate"], p["w_up"], p["w_down"], tm)

    csum = jnp.cumsum(sel, axis=1)
    pos_tok = jnp.where(sel > 0, csum - 1, -1).T
    ends = csum[:, CMB_TOK - 1::CMB_TOK]
    excl = jnp.concatenate([jnp.zeros((N_EXPERTS, 1), jnp.int32), ends[:, :-1]], axis=1)
    out = _combine(x1, aff_t.T, pos_tok, y, excl.reshape(-1), (ends - excl).reshape(-1))
    return out.reshape(B, S, D_MODEL)


def kernel(x_prompt, x_sample, norm1_g, w_in, attn_qnorm_g, attn_knorm_g, ret_decay_fwd, ret_decay_bwd,
           ret_norm_g, w_out, norm2_g, w_router, w_gate_e, w_up_e, w_down_e):
    y_prompt, y_sample = x_prompt, x_sample
    for l in range(norm1_g.shape[0]):
        p = {
            "g1": norm1_g[l][None, :],
            "w_in": w_in[l].astype(BF16),
            "gq": jnp.tile(attn_qnorm_g[l], LANES // ATT_HD)[None, :],
            "gk": jnp.tile(attn_knorm_g[l], LANES // ATT_HD)[None, :],
            "decays": jnp.stack([ret_decay_fwd[l], ret_decay_bwd[l]]).astype(F32),
            "gn": ret_norm_g[l][None, :].astype(F32),
            "w_out": w_out[l].astype(BF16),
            "g2": norm2_g[l][None, :],
            "w_router_t": w_router[l].T.astype(BF16),
            "w_gate": w_gate_e[l].astype(BF16),
            "w_up": w_up_e[l].astype(BF16),
            "w_down": w_down_e[l].astype(BF16),
        }
        y_prompt = _layer(y_prompt, p)
        y_sample = _layer(y_sample, p)
    return (y_prompt, y_sample)
```

```python
import functools

import jax
import jax.numpy as jnp
import numpy as np
from jax import lax
from jax.experimental import pallas as pl
from jax.experimental.pallas import tpu as pltpu

F32 = jnp.float32
BF16 = jnp.bfloat16

D_MODEL = 1024
ATT_HEADS, ATT_HD = 8, 64
RET_HEADS, RET_HD = 4, 128
ATT_W = ATT_HEADS * ATT_HD
RET_W = RET_HEADS * RET_HD
IN_W = 3 * ATT_W + 4 * RET_W
LANES = 128
N_SLABS = IN_W // LANES
GROUP_W = 512
SLABS_PER_GROUP = GROUP_W // LANES
N_ATT_SLABS = 3 * ATT_W // LANES
RET_CHUNK = 128
ROPE_THETA = 10000.0
N_EXPERTS = 16
D_FF = 2048
CAPACITY_FACTOR = 2
NORM_EPS = 1e-6
NEG_INF = -1e30
ATT_QBLK = 128
VMEM_LIMIT = 56 * 1024 * 1024


def _cparams(sem):
    return pltpu.CompilerParams(dimension_semantics=sem, vmem_limit_bytes=VMEM_LIMIT)


def _inproj_kernel(x_ref, g1_ref, w_ref, gq_ref, gk_ref, ca_ref, sa_ref, cr_ref, sr_ref, oa_ref, or_ref):
    x = x_ref[...]
    ms = jnp.mean(x * x, axis=-1, keepdims=True)
    h = (x * lax.rsqrt(ms + NORM_EPS) * g1_ref[...]).astype(BF16)
    tm = x.shape[0]
    lane = lax.broadcasted_iota(jnp.int32, (tm, LANES), 1)
    first = lane < ATT_HD
    low_half = (lane & (ATT_HD // 2)) == 0
    ca, sa, cr, sr = ca_ref[...], sa_ref[...], cr_ref[...], sr_ref[...]
    for grp in range(IN_W // GROUP_W):
        acc = jnp.dot(h, w_ref[:, grp * GROUP_W:(grp + 1) * GROUP_W], preferred_element_type=F32)
        for p in range(SLABS_PER_GROUP):
            a = acc[:, p * LANES:(p + 1) * LANES]
            if grp in (0, 1):
                sq = a * a
                s0 = jnp.sum(jnp.where(first, sq, 0.0), axis=-1, keepdims=True)
                s1 = jnp.sum(jnp.where(first, 0.0, sq), axis=-1, keepdims=True)
                ms2 = jnp.where(first, s0, s1) * (1.0 / ATT_HD)
                g = gq_ref[...] if grp == 0 else gk_ref[...]
                y = a * lax.rsqrt(ms2 + NORM_EPS) * g
                partner = jnp.where(low_half, pltpu.roll(y, LANES - ATT_HD // 2, 1),
                                    pltpu.roll(y, ATT_HD // 2, 1))
                r = y * ca + partner * sa
                if grp == 0:
                    r = r * (ATT_HD ** -0.5)
            elif grp in (3, 4):
                r = a * cr + pltpu.roll(a, RET_HD // 2, 1) * sr
                if grp == 4:
                    r = r * (RET_HD ** -0.5)
            else:
                r = a
            slab = grp * SLABS_PER_GROUP + p
            if slab < N_ATT_SLABS:
                oa_ref[slab] = r
            else:
                or_ref[slab - N_ATT_SLABS] = r.astype(BF16)


def _rope_tables(S):
    pos = jnp.arange(S, dtype=F32)

    def table(hd):
        inv_freq = ROPE_THETA ** (-jnp.arange(0, hd, 2, dtype=F32) / hd)
        ang = pos[:, None] * inv_freq[None, :]
        cos, sin = jnp.cos(ang), jnp.sin(ang)
        reps = LANES // hd
        cos_t = jnp.tile(jnp.concatenate([cos, cos], axis=-1), (1, reps))
        sin_t = jnp.tile(jnp.concatenate([-sin, sin], axis=-1), (1, reps))
        return cos_t, sin_t

    ca, sa = table(ATT_HD)
    cr, sr = table(RET_HD)
    return ca, sa, cr, sr


def _inproj(x2d, S, g1, w_in_bf, gq, gk, tables, tm):
    T = x2d.shape[0]
    n_pos_blk = S // tm
    tab_spec = pl.BlockSpec((tm, LANES), lambda i: (i % n_pos_blk, 0))
    full = lambda shape: pl.BlockSpec(shape, lambda i: (0,) * len(shape))
    return pl.pallas_call(
        _inproj_kernel,
        grid=(T // tm,),
        in_specs=[pl.BlockSpec((tm, D_MODEL), lambda i: (i, 0)), full((1, D_MODEL)),
                  full((D_MODEL, IN_W)), full((1, LANES)), full((1, LANES)),
                  tab_spec, tab_spec, tab_spec, tab_spec],
        out_specs=[pl.BlockSpec((N_ATT_SLABS, tm, LANES), lambda i: (0, i, 0)),
                   pl.BlockSpec((N_SLABS - N_ATT_SLABS, tm, LANES), lambda i: (0, i, 0))],
        out_shape=[jax.ShapeDtypeStruct((N_ATT_SLABS, T, LANES), F32),
                   jax.ShapeDtypeStruct((N_SLABS - N_ATT_SLABS, T, LANES), BF16)],
        compiler_params=_cparams(("parallel",)),
        name="inproj",
    )(x2d, g1, w_in_bf, gq, gk, *tables)


DILATED_PATTERNS = ((128, 1), (512, 4), (2048, 16))
ATT_HALF = 64
ATT_SB = 2048
ATT_NB = ATT_SB // ATT_QBLK


def _attn_kernel(q_ref, k_ref, v_ref, o_ref, o_scr, lse_scr, *, S):
    t0 = pl.program_id(2) * ATT_SB
    lane = lax.broadcasted_iota(jnp.int32, (ATT_QBLK, LANES), 1)
    first = lane < ATT_HD
    nt_dims = (((1,), (1,)), ((), ()))

    def band(qv, kv, vv, off):
        kw = kv.shape[0]
        rel = (lax.broadcasted_iota(jnp.int32, (ATT_QBLK, kw), 0)
               - lax.broadcasted_iota(jnp.int32, (ATT_QBLK, kw), 1)) + off
        valid = jnp.abs(rel) <= ATT_HALF
        zero = jnp.zeros_like(qv)
        outs, lses = [], []
        for h in range(2):
            qm = jnp.where(first, qv, zero) if h == 0 else jnp.where(first, zero, qv)
            s = lax.dot_general(qm, kv, nt_dims, preferred_element_type=F32)
            s = jnp.where(valid, s, NEG_INF)
            m = jnp.max(s, axis=-1, keepdims=True)
            p = jnp.exp(s - m)
            l = jnp.sum(p, axis=-1, keepdims=True)
            acc = jnp.dot(p.astype(BF16), vv, preferred_element_type=F32)
            outs.append(acc * (1.0 / l))
            lses.append(m + jnp.log(l))
        return jnp.where(first, outs[0], outs[1]), jnp.where(first, lses[0], lses[1])

    def step(n, carry):
        for pi, (window, dil) in enumerate(DILATED_PATTERNS):
            L = S // dil
            kw = min(2 * ATT_QBLK, L)
            per_class = ATT_SB // dil // ATT_QBLK
            r = n // per_class
            bi = n % per_class
            lq0 = t0 // dil + bi * ATT_QBLK
            lk0 = jnp.clip(lq0 - ATT_HALF, 0, L - kw)
            qrow = dil * ATT_QBLK * bi + r
            krow = dil * lk0 + r
            if dil == 1:
                qs = pl.ds(pl.multiple_of(qrow, ATT_QBLK), ATT_QBLK)
                ks = pl.ds(pl.multiple_of(krow, 8), kw)
            else:
                qs = pl.ds(qrow, ATT_QBLK, stride=dil)
                ks = pl.ds(krow, kw, stride=dil)
            o, lse = band(q_ref[qs, :].astype(BF16), k_ref[ks, :].astype(BF16), v_ref[ks, :].astype(BF16),
                          lq0 - lk0)
            o_scr[pi, qs, :] = o
            lse_scr[pi, qs, :] = lse
        return carry

    lax.fori_loop(0, ATT_NB, step, 0, unroll=2)

    def merge(c, carry):
        sl = pl.ds(pl.multiple_of(c * ATT_QBLK, ATT_QBLK), ATT_QBLK)
        lses = [lse_scr[pi, sl, :] for pi in range(3)]
        mx = jnp.maximum(jnp.maximum(lses[0], lses[1]), lses[2])
        ws = [jnp.exp(x - mx) for x in lses]
        num = ws[0] * o_scr[0, sl, :] + ws[1] * o_scr[1, sl, :] + ws[2] * o_scr[2, sl, :]
        o_ref[sl, :] = (num / (ws[0] + ws[1] + ws[2])).astype(o_ref.dtype)
        return carry

    lax.fori_loop(0, ATT_NB, merge, 0)


def _attention(qkv4, B, S):
    n_pairs = ATT_W // LANES
    kv_spec = lambda off: pl.BlockSpec((None, None, S, LANES), lambda b, p, i: (off + p, b, 0, 0))
    return pl.pallas_call(
        functools.partial(_attn_kernel, S=S),
        grid=(B, n_pairs, S // ATT_SB),
        in_specs=[pl.BlockSpec((None, None, ATT_SB, LANES), lambda b, p, i: (p, b, i, 0)),
                  kv_spec(n_pairs), kv_spec(2 * n_pairs)],
        out_specs=pl.BlockSpec((None, ATT_SB, LANES), lambda b, p, i: (b, i, p)),
        out_shape=jax.ShapeDtypeStruct((B, S, ATT_W), BF16),
        scratch_shapes=[pltpu.VMEM((3, ATT_SB, LANES), F32), pltpu.VMEM((3, ATT_SB, LANES), F32)],
        compiler_params=_cparams(("parallel", "parallel", "arbitrary")),
        name="dilated_attn",
    )(qkv4, qkv4, qkv4)


def _ret_kernel(dec_ref, q_ref, k_ref, v_ref, g_ref, gn_ref, o_ref, of_scr, ob_scr, *, S):
    C = RET_CHUNK
    nc = S // C
    h = pl.program_id(1)
    row = lax.broadcasted_iota(jnp.int32, (C, C), 0).astype(F32)
    col = lax.broadcasted_iota(jnp.int32, (C, C), 1).astype(F32)
    nt_dims = (((1,), (1,)), ((), ()))
    tn_dims = (((0,), (0,)), ((), ()))

    def consts(direction):
        log_g = -jnp.exp(jnp.full((C, C), dec_ref[direction, h], F32))
        if direction == 0:
            diff = row - col
            mask = diff >= 0.0
            k_dec, q_dec = jnp.exp(log_g * (C - 1.0 - row)), jnp.exp(log_g * (row + 1.0))
        else:
            diff = col - row
            mask = diff > 0.0
            k_dec, q_dec = jnp.exp(log_g * row), jnp.exp(log_g * (C - row))
        dmat = jnp.where(mask, jnp.exp(log_g * jnp.maximum(diff, 0.0)), 0.0)
        return dmat, k_dec, q_dec, jnp.exp(log_g * C)

    def chunk_out(c, state, dmat, k_dec, q_dec, g_chunk):
        sl = pl.ds(pl.multiple_of(c * C, C), C)
        qc, kc, vc = q_ref[sl, :], k_ref[sl, :], v_ref[sl, :]
        qk = lax.dot_general(qc, kc, nt_dims, preferred_element_type=F32)
        inner = (qk * dmat).astype(BF16)
        o = jnp.dot(inner, vc, preferred_element_type=F32)
        o = o + jnp.dot((qc.astype(F32) * q_dec).astype(BF16), state.astype(BF16),
                        preferred_element_type=F32)
        kd = (kc.astype(F32) * k_dec).astype(BF16)
        kv = lax.dot_general(kd, vc, tn_dims, preferred_element_type=F32)
        return sl, o, state * g_chunk + kv

    cf, cb = consts(0), consts(1)

    def scan(i, states):
        sl_f, o_f, st_f = chunk_out(i, states[0], *cf)
        of_scr[sl_f, :] = o_f
        sl_b, o_b, st_b = chunk_out(nc - 1 - i, states[1], *cb)
        ob_scr[sl_b, :] = o_b
        return st_f, st_b

    zero = jnp.zeros((C, C), F32)
    lax.fori_loop(0, nc, scan, (zero, zero), unroll=2)

    gn = gn_ref[...]

    def finish(c, carry):
        sl = pl.ds(pl.multiple_of(c * C, C), C)
        o = of_scr[sl, :] + ob_scr[sl, :]
        mu = jnp.mean(o, axis=-1, keepdims=True)
        var = jnp.mean(jnp.square(o - mu), axis=-1, keepdims=True)
        y = (o - mu) * lax.rsqrt(var + NORM_EPS) * gn
        g = g_ref[sl, :].astype(F32)
        o_ref[sl, :] = (y * (g * (1.0 / (1.0 + jnp.exp(-g))))).astype(o_ref.dtype)
        return carry

    lax.fori_loop(0, nc, finish, 0, unroll=2)


def _retention(qkv4, decays, gn, B, S):
    spec = lambda off: pl.BlockSpec((None, None, S, LANES), lambda b, h: (off + h, b, 0, 0))
    return pl.pallas_call(
        functools.partial(_ret_kernel, S=S),
        grid=(B, RET_HEADS),
        in_specs=[pl.BlockSpec(memory_space=pltpu.SMEM),
                  spec(0), spec(RET_HEADS), spec(2 * RET_HEADS), spec(3 * RET_HEADS),
                  pl.BlockSpec((1, LANES), lambda b, h: (0, h))],
        out_specs=pl.BlockSpec((None, S, LANES), lambda b, h: (b, 0, h)),
        out_shape=jax.ShapeDtypeStruct((B, S, RET_W), BF16),
        scratch_shapes=[pltpu.VMEM((S, LANES), F32), pltpu.VMEM((S, LANES), F32)],
        compiler_params=_cparams(("parallel", "arbitrary")),
        name="retention",
    )(decays, qkv4, qkv4, qkv4, qkv4, gn)


def _outproj_kernel(x_ref, att_ref, ret_ref, wo_ref, g2_ref, wr_ref, x1_ref, h2_ref, aff_ref):
    y = jnp.dot(att_ref[...], wo_ref[:ATT_W, :], preferred_element_type=F32)
    y = y + jnp.dot(ret_ref[...], wo_ref[ATT_W:, :], preferred_element_type=F32)
    x1 = x_ref[...] + y
    x1_ref[...] = x1
    ms = jnp.mean(x1 * x1, axis=-1, keepdims=True)
    h2 = x1 * lax.rsqrt(ms + NORM_EPS) * g2_ref[...]
    h2_ref[...] = h2.astype(h2_ref.dtype)
    logits = lax.dot_general(wr_ref[...], h2.astype(BF16), (((1,), (1,)), ((), ())),
                             preferred_element_type=F32)
    e = jnp.exp(logits - jnp.max(logits, axis=0, keepdims=True))
    aff_ref[...] = e / jnp.sum(e, axis=0, keepdims=True)


def _outproj(x2d, att, ret, wo_bf, g2, wr_t_bf, tm):
    T = x2d.shape[0]
    full = lambda shape: pl.BlockSpec(shape, lambda i: (0,) * len(shape))
    return pl.pallas_call(
        _outproj_kernel,
        grid=(T // tm,),
        in_specs=[pl.BlockSpec((tm, D_MODEL), lambda i: (i, 0)),
                  pl.BlockSpec((tm, ATT_W), lambda i: (i, 0)),
                  pl.BlockSpec((tm, RET_W), lambda i: (i, 0)),
                  full((ATT_W + RET_W, D_MODEL)), full((1, D_MODEL)), full((N_EXPERTS, D_MODEL))],
        out_specs=[pl.BlockSpec((tm, D_MODEL), lambda i: (i, 0)),
                   pl.BlockSpec((tm, D_MODEL), lambda i: (i, 0)),
                   pl.BlockSpec((N_EXPERTS, tm), lambda i: (0, i))],
        out_shape=[jax.ShapeDtypeStruct((T, D_MODEL), F32),
                   jax.ShapeDtypeStruct((T, D_MODEL), BF16),
                   jax.ShapeDtypeStruct((N_EXPERTS, T), F32)],
        compiler_params=_cparams(("parallel",)),
        name="outproj_router",
    )(x2d, att, ret, wo_bf, g2, wr_t_bf)


def _select_kernel(aff_ref, sel_ref, *, cap):
    bits = pltpu.bitcast(aff_ref[...], jnp.int32)
    E, T = bits.shape
    capf = jnp.float32(cap)

    def count(mask):
        return jnp.sum(jnp.where(mask, 1.0, 0.0), axis=1, keepdims=True)

    def thr_step(i, thr):
        cand = thr | jnp.left_shift(jnp.int32(1), 30 - i)
        return jnp.where(count(bits >= cand) >= capf, cand, thr)

    thr = lax.fori_loop(0, 31, thr_step, jnp.zeros((E, 1), jnp.int32))
    above = bits > thr
    ties = bits == thr
    need = capf - count(above)
    idx = lax.broadcasted_iota(jnp.int32, (E, T), 1)
    n_idx_bits = max(1, (T - 1).bit_length())

    def cut_step(i, cut):
        cand = cut | jnp.left_shift(jnp.int32(1), n_idx_bits - 1 - i)
        return jnp.where(count(ties & (idx < cand)) < need, cand, cut)

    cut = lax.fori_loop(0, n_idx_bits, cut_step, jnp.zeros((E, 1), jnp.int32))
    sel_ref[...] = jnp.where(above | (ties & (idx <= cut)), 1, 0).astype(jnp.int32)


def _select(aff_t, cap):
    E, T = aff_t.shape
    return pl.pallas_call(
        functools.partial(_select_kernel, cap=cap),
        grid=(1,),
        in_specs=[pl.BlockSpec((E, T), lambda i: (0, 0))],
        out_specs=pl.BlockSpec((E, T), lambda i: (0, 0)),
        out_shape=jax.ShapeDtypeStruct((E, T), jnp.int32),
        compiler_params=_cparams(("arbitrary",)),
        name="expert_select",
    )(aff_t)


FF_CHUNK = 512


def _ffn_kernel(xin_ref, wg_ref, wu_ref, wd_ref, gate_ref, y_ref):
    xin = xin_ref[...]
    acc = jnp.zeros(y_ref.shape, F32)
    for f in range(D_FF // FF_CHUNK):
        fs = slice(f * FF_CHUNK, (f + 1) * FF_CHUNK)
        a = jnp.dot(xin, wg_ref[:, fs], preferred_element_type=F32)
        u = jnp.dot(xin, wu_ref[:, fs], preferred_element_type=F32)
        hmid = (a * (1.0 / (1.0 + jnp.exp(-a))) * u).astype(BF16)
        acc = acc + jnp.dot(hmid, wd_ref[fs, :], preferred_element_type=F32)
    y_ref[...] = (acc * gate_ref[...]).astype(y_ref.dtype)


def _ffn(xin, wg_bf, wu_bf, wd_bf, gates, tm):
    E, cap, _ = xin.shape
    return pl.pallas_call(
        _ffn_kernel,
        grid=(E, cap // tm),
        in_specs=[pl.BlockSpec((None, tm, D_MODEL), lambda e, j: (e, j, 0)),
                  pl.BlockSpec((None, D_MODEL, D_FF), lambda e, j: (e, 0, 0)),
                  pl.BlockSpec((None, D_MODEL, D_FF), lambda e, j: (e, 0, 0)),
                  pl.BlockSpec((None, D_FF, D_MODEL), lambda e, j: (e, 0, 0)),
                  pl.BlockSpec((None, tm, 1), lambda e, j: (e, j, 0))],
        out_specs=pl.BlockSpec((None, tm, D_MODEL), lambda e, j: (e, j, 0)),
        out_shape=jax.ShapeDtypeStruct((E, cap, D_MODEL), BF16),
        compiler_params=_cparams(("parallel", "arbitrary")),
        name="expert_ffn",
    )(xin, wg_bf, wu_bf, wd_bf, gates)


CMB_TOK = 512
CMB_ROWS = 128
BF16_SUBLANES = 16


def _combine_kernel(excl_ref, cnt_ref, x1_ref, pos_ref, y_hbm, o_ref, ybuf, xbuf, sem, xsem, *, cap, nblk):
    b = pl.program_id(0)
    slot = b % 2
    lane = lax.broadcasted_iota(jnp.int32, (CMB_TOK, CMB_ROWS), 1)

    def first_row(e, blk):
        return (excl_ref[e * nblk + blk] // BF16_SUBLANES) * BF16_SUBLANES

    def fetch(e, row, dst, s):
        row = pl.multiple_of(jnp.minimum(row, cap - CMB_ROWS), BF16_SUBLANES)
        return pltpu.make_async_copy(y_hbm.at[e, pl.ds(row, CMB_ROWS), :], dst, s)

    def start_block(blk, sl):
        for e in range(N_EXPERTS):
            fetch(e, first_row(e, blk), ybuf.at[sl, e], sem.at[sl, e]).start()

    @pl.when(b == 0)
    def _():
        start_block(0, 0)

    @pl.when(b + 1 < nblk)
    def _():
        start_block(b + 1, 1 - slot)

    def onehot(pos_col, nominal):
        hit = ((pos_col - jnp.minimum(nominal, cap - CMB_ROWS)) == lane) & (pos_col >= nominal)
        return jnp.where(hit, 1.0, 0.0).astype(BF16)

    parts = []
    for e in range(N_EXPERTS):
        nominal0 = first_row(e, b)
        fetch(e, nominal0, ybuf.at[slot, e], sem.at[slot, e]).wait()
        parts.append(onehot(pos_ref[:, e:e + 1], nominal0))
    rows = ybuf[slot].reshape(N_EXPERTS * CMB_ROWS, D_MODEL)
    o_ref[...] = x1_ref[...] + jnp.dot(jnp.concatenate(parts, axis=1), rows, preferred_element_type=F32)

    for e in range(N_EXPERTS):
        nominal0 = first_row(e, b)
        need_end = excl_ref[e * nblk + b] + cnt_ref[e * nblk + b]
        n_extra = jnp.maximum((need_end - nominal0 + CMB_ROWS - 1) // CMB_ROWS - 1, 0)

        def extra(k, carry):
            nominal = nominal0 + (k + 1) * CMB_ROWS
            cp = fetch(e, nominal, xbuf, xsem)
            cp.start()
            cp.wait()
            o_ref[...] += jnp.dot(onehot(pos_ref[:, e:e + 1], nominal), xbuf[...], preferred_element_type=F32)
            return carry

        lax.fori_loop(0, n_extra, extra, 0)


def _combine(x1, pos_tok, y, excl, cnt):
    T = x1.shape[0]
    E, cap, _ = y.shape
    nblk = T // CMB_TOK
    grid_spec = pltpu.PrefetchScalarGridSpec(
        num_scalar_prefetch=2,
        grid=(nblk,),
        in_specs=[pl.BlockSpec((CMB_TOK, D_MODEL), lambda b, *_: (b, 0)),
                  pl.BlockSpec((CMB_TOK, E), lambda b, *_: (b, 0)),
                  pl.BlockSpec(memory_space=pl.ANY)],
        out_specs=pl.BlockSpec((CMB_TOK, D_MODEL), lambda b, *_: (b, 0)),
        scratch_shapes=[pltpu.VMEM((2, E, CMB_ROWS, D_MODEL), BF16),
                        pltpu.VMEM((CMB_ROWS, D_MODEL), BF16),
                        pltpu.SemaphoreType.DMA((2, E)),
                        pltpu.SemaphoreType.DMA(())],
    )
    return pl.pallas_call(
        functools.partial(_combine_kernel, cap=cap, nblk=nblk),
        grid_spec=grid_spec,
        out_shape=jax.ShapeDtypeStruct((T, D_MODEL), F32),
        compiler_params=_cparams(("arbitrary",)),
        name="moe_combine",
    )(excl, cnt, x1, pos_tok, y)


def _layer(x, p):
    B, S, _ = x.shape
    T = B * S
    tm = 512
    x2d = x.reshape(T, D_MODEL)
    qkv_att, qkv_ret = _inproj(x2d, S, p["g1"], p["w_in"], p["gq"], p["gk"], _rope_tables(S), tm)
    att = _attention(qkv_att.reshape(N_ATT_SLABS, B, S, LANES), B, S).reshape(T, ATT_W)
    ret = _retention(qkv_ret.reshape(N_SLABS - N_ATT_SLABS, B, S, LANES), p["decays"], p["gn"], B, S).reshape(T, RET_W)
    x1, h2, aff_t = _outproj(x2d, att, ret, p["w_out"], p["g2"], p["w_router_t"], tm)

    cap = CAPACITY_FACTOR * T // N_EXPERTS
    sel = _select(aff_t, cap)
    tok = jnp.arange(T, dtype=jnp.int32)[None, :]
    idx = jnp.sort(jnp.where(sel > 0, tok, T), axis=1)[:, :cap]
    xin = jnp.take(h2, idx.reshape(-1), axis=0).reshape(N_EXPERTS, cap, D_MODEL)
    gates = jnp.take_along_axis(aff_t, idx, axis=1)[..., None]
    y = _ffn(xin, p["w_gate"], p["w_up"], p["w_down"], gates, tm)

    csum = jnp.cumsum(sel, axis=1)
    pos_tok = jnp.where(sel > 0, csum - 1, -1).T
    ends = csum[:, CMB_TOK - 1::CMB_TOK]
    excl = jnp.concatenate([jnp.zeros((N_EXPERTS, 1), jnp.int32), ends[:, :-1]], axis=1)
    out = _combine(x1, pos_tok, y, excl.reshape(-1), (ends - excl).reshape(-1))
    return out.reshape(B, S, D_MODEL)


def kernel(x_prompt, x_sample, norm1_g, w_in, attn_qnorm_g, attn_knorm_g, ret_decay_fwd, ret_decay_bwd,
           ret_norm_g, w_out, norm2_g, w_router, w_gate_e, w_up_e, w_down_e):
    y_prompt, y_sample = x_prompt, x_sample
    for l in range(norm1_g.shape[0]):
        p = {
            "g1": norm1_g[l][None, :],
            "w_in": w_in[l].astype(BF16),
            "gq": jnp.tile(attn_qnorm_g[l], LANES // ATT_HD)[None, :],
            "gk": jnp.tile(attn_knorm_g[l], LANES // ATT_HD)[None, :],
            "decays": jnp.stack([ret_decay_fwd[l], ret_decay_bwd[l]]).astype(F32),
            "gn": ret_norm_g[l][None, :].astype(F32),
            "w_out": w_out[l].astype(BF16),
            "g2": norm2_g[l][None, :],
            "w_router_t": w_router[l].T.astype(BF16),
            "w_gate": w_gate_e[l].astype(BF16),
            "w_up": w_up_e[l].astype(BF16),
            "w_down": w_down_e[l].astype(BF16),
        }
        y_prompt = _layer(y_prompt, p)
        y_sample = _layer(y_sample, p)
    return (y_prompt, y_sample)
```

```python
import functools

import jax
import jax.numpy as jnp
import numpy as np
from jax import lax
from jax.experimental import pallas as pl
from jax.experimental.pallas import tpu as pltpu

F32 = jnp.float32
BF16 = jnp.bfloat16

D_MODEL = 1024
ATT_HEADS, ATT_HD = 8, 64
RET_HEADS, RET_HD = 4, 128
ATT_W = ATT_HEADS * ATT_HD
RET_W = RET_HEADS * RET_HD
IN_W = 3 * ATT_W + 4 * RET_W
LANES = 128
N_SLABS = IN_W // LANES
GROUP_W = 512
SLABS_PER_GROUP = GROUP_W // LANES
N_ATT_SLABS = 3 * ATT_W // LANES
RET_CHUNK = 256
ROPE_THETA = 10000.0
N_EXPERTS = 16
D_FF = 2048
CAPACITY_FACTOR = 2
NORM_EPS = 1e-6
NEG_INF = -1e30
ATT_QBLK = 128
VMEM_LIMIT = 56 * 1024 * 1024


def _cparams(sem):
    return pltpu.CompilerParams(dimension_semantics=sem, vmem_limit_bytes=VMEM_LIMIT)


def _inproj_kernel(x_ref, g1_ref, w_ref, gq_ref, gk_ref, ca_ref, sa_ref, cr_ref, sr_ref, oa_ref, or_ref):
    x = x_ref[...]
    ms = jnp.mean(x * x, axis=-1, keepdims=True)
    h = (x * lax.rsqrt(ms + NORM_EPS) * g1_ref[...]).astype(BF16)
    tm = x.shape[0]
    lane = lax.broadcasted_iota(jnp.int32, (tm, LANES), 1)
    first = lane < ATT_HD
    low_half = (lane & (ATT_HD // 2)) == 0
    ca, sa, cr, sr = ca_ref[...], sa_ref[...], cr_ref[...], sr_ref[...]
    for grp in range(IN_W // GROUP_W):
        acc = jnp.dot(h, w_ref[:, grp * GROUP_W:(grp + 1) * GROUP_W], preferred_element_type=F32)
        for p in range(SLABS_PER_GROUP):
            a = acc[:, p * LANES:(p + 1) * LANES]
            if grp in (0, 1):
                sq = a * a
                s0 = jnp.sum(jnp.where(first, sq, 0.0), axis=-1, keepdims=True)
                s1 = jnp.sum(jnp.where(first, 0.0, sq), axis=-1, keepdims=True)
                ms2 = jnp.where(first, s0, s1) * (1.0 / ATT_HD)
                g = gq_ref[...] if grp == 0 else gk_ref[...]
                y = a * lax.rsqrt(ms2 + NORM_EPS) * g
                partner = jnp.where(low_half, pltpu.roll(y, LANES - ATT_HD // 2, 1),
                                    pltpu.roll(y, ATT_HD // 2, 1))
                r = y * ca + partner * sa
                if grp == 0:
                    r = r * (ATT_HD ** -0.5)
            elif grp in (3, 4):
                r = a * cr + pltpu.roll(a, RET_HD // 2, 1) * sr
                if grp == 4:
                    r = r * (RET_HD ** -0.5)
            else:
                r = a
            slab = grp * SLABS_PER_GROUP + p
            if slab < N_ATT_SLABS:
                oa_ref[slab] = r
            else:
                or_ref[slab - N_ATT_SLABS] = r.astype(BF16)


def _rope_tables(S):
    pos = jnp.arange(S, dtype=F32)

    def table(hd):
        inv_freq = ROPE_THETA ** (-jnp.arange(0, hd, 2, dtype=F32) / hd)
        ang = pos[:, None] * inv_freq[None, :]
        cos, sin = jnp.cos(ang), jnp.sin(ang)
        reps = LANES // hd
        cos_t = jnp.tile(jnp.concatenate([cos, cos], axis=-1), (1, reps))
        sin_t = jnp.tile(jnp.concatenate([-sin, sin], axis=-1), (1, reps))
        return cos_t, sin_t

    ca, sa = table(ATT_HD)
    cr, sr = table(RET_HD)
    return ca, sa, cr, sr


def _inproj(x2d, S, g1, w_in_bf, gq, gk, tables, tm):
    T = x2d.shape[0]
    n_pos_blk = S // tm
    tab_spec = pl.BlockSpec((tm, LANES), lambda i: (i % n_pos_blk, 0))
    full = lambda shape: pl.BlockSpec(shape, lambda i: (0,) * len(shape))
    return pl.pallas_call(
        _inproj_kernel,
        grid=(T // tm,),
        in_specs=[pl.BlockSpec((tm, D_MODEL), lambda i: (i, 0)), full((1, D_MODEL)),
                  full((D_MODEL, IN_W)), full((1, LANES)), full((1, LANES)),
                  tab_spec, tab_spec, tab_spec, tab_spec],
        out_specs=[pl.BlockSpec((N_ATT_SLABS, tm, LANES), lambda i: (0, i, 0)),
                   pl.BlockSpec((N_SLABS - N_ATT_SLABS, tm, LANES), lambda i: (0, i, 0))],
        out_shape=[jax.ShapeDtypeStruct((N_ATT_SLABS, T, LANES), F32),
                   jax.ShapeDtypeStruct((N_SLABS - N_ATT_SLABS, T, LANES), BF16)],
        compiler_params=_cparams(("parallel",)),
        name="inproj",
    )(x2d, g1, w_in_bf, gq, gk, *tables)


DILATED_PATTERNS = ((128, 1), (512, 4), (2048, 16))
ATT_HALF = 64
ATT_SB = 2048
ATT_NB = ATT_SB // ATT_QBLK
ATT_UNROLL = 2


def _attn_kernel(q_ref, k_ref, v_ref, o_ref, o_scr, lse_scr, bias_scr, *, S):
    t0 = pl.program_id(2) * ATT_SB
    lane = lax.broadcasted_iota(jnp.int32, (ATT_QBLK, LANES), 1)
    first = lane < ATT_HD
    nt_dims = (((1,), (1,)), ((), ()))

    rel = (lax.broadcasted_iota(jnp.int32, (ATT_QBLK, 2 * ATT_QBLK), 0)
           - lax.broadcasted_iota(jnp.int32, (ATT_QBLK, 2 * ATT_QBLK), 1))
    for oi in range(3):
        bias_scr[oi] = jnp.where(jnp.abs(rel + oi * ATT_HALF) <= ATT_HALF, 0.0, NEG_INF)

    def band(qv, kv, vv, off):
        kw = kv.shape[0]
        bias = bias_scr[off // ATT_HALF][:, :kw]
        zero = jnp.zeros_like(qv)
        q2 = jnp.concatenate([jnp.where(first, qv, zero), jnp.where(first, zero, qv)], axis=0)
        s = lax.dot_general(q2, kv, nt_dims, preferred_element_type=F32) + jnp.concatenate([bias, bias], axis=0)
        m = jnp.max(s, axis=-1, keepdims=True)
        p = jnp.exp(s - m)
        l = jnp.sum(p, axis=-1, keepdims=True)
        acc = jnp.dot(p.astype(BF16), vv, preferred_element_type=F32) * (1.0 / l)
        lse = m + jnp.log(l)
        return (jnp.where(first, acc[:ATT_QBLK], acc[ATT_QBLK:]),
                jnp.where(first, lse[:ATT_QBLK], lse[ATT_QBLK:]))

    def step(n, carry):
        for pi, (window, dil) in enumerate(DILATED_PATTERNS):
            L = S // dil
            kw = min(2 * ATT_QBLK, L)
            per_class = ATT_SB // dil // ATT_QBLK
            r = n // per_class
            bi = n % per_class
            lq0 = t0 // dil + bi * ATT_QBLK
            lk0 = jnp.clip(lq0 - ATT_HALF, 0, L - kw)
            qrow = dil * ATT_QBLK * bi + r
            krow = dil * lk0 + r
            if dil == 1:
                qs = pl.ds(pl.multiple_of(qrow, ATT_QBLK), ATT_QBLK)
                ks = pl.ds(pl.multiple_of(krow, 8), kw)
            else:
                qs = pl.ds(qrow, ATT_QBLK, stride=dil)
                ks = pl.ds(krow, kw, stride=dil)
            o, lse = band(q_ref[qs, :].astype(BF16), k_ref[ks, :].astype(BF16), v_ref[ks, :].astype(BF16),
                          lq0 - lk0)
            o_scr[pi, qs, :] = o
            lse_scr[pi, qs, :] = lse
        return carry

    lax.fori_loop(0, ATT_NB, step, 0, unroll=ATT_UNROLL)

    def merge(c, carry):
        sl = pl.ds(pl.multiple_of(c * ATT_QBLK, ATT_QBLK), ATT_QBLK)
        lses = [lse_scr[pi, sl, :] for pi in range(3)]
        mx = jnp.maximum(jnp.maximum(lses[0], lses[1]), lses[2])
        ws = [jnp.exp(x - mx) for x in lses]
        num = ws[0] * o_scr[0, sl, :] + ws[1] * o_scr[1, sl, :] + ws[2] * o_scr[2, sl, :]
        o_ref[sl, :] = (num / (ws[0] + ws[1] + ws[2])).astype(o_ref.dtype)
        return carry

    lax.fori_loop(0, ATT_NB, merge, 0)


def _attention(qkv4, B, S):
    n_pairs = ATT_W // LANES
    kv_spec = lambda off: pl.BlockSpec((None, None, S, LANES), lambda b, p, i: (off + p, b, 0, 0))
    return pl.pallas_call(
        functools.partial(_attn_kernel, S=S),
        grid=(B, n_pairs, S // ATT_SB),
        in_specs=[pl.BlockSpec((None, None, ATT_SB, LANES), lambda b, p, i: (p, b, i, 0)),
                  kv_spec(n_pairs), kv_spec(2 * n_pairs)],
        out_specs=pl.BlockSpec((None, ATT_SB, LANES), lambda b, p, i: (b, i, p)),
        out_shape=jax.ShapeDtypeStruct((B, S, ATT_W), BF16),
        scratch_shapes=[pltpu.VMEM((3, ATT_SB, LANES), F32), pltpu.VMEM((3, ATT_SB, LANES), F32),
                        pltpu.VMEM((3, ATT_QBLK, 2 * ATT_QBLK), F32)],
        compiler_params=_cparams(("parallel", "parallel", "arbitrary")),
        name="dilated_attn",
    )(qkv4, qkv4, qkv4)


def _ret_kernel(dec_ref, q_ref, k_ref, v_ref, g_ref, gn_ref, o_ref, of_scr, ob_scr, *, S):
    C = RET_CHUNK
    nc = S // C
    h = pl.program_id(1)
    nt_dims = (((1,), (1,)), ((), ()))
    tn_dims = (((0,), (0,)), ((), ()))

    def consts(direction):
        def log_g(shape):
            return -jnp.exp(jnp.full(shape, dec_ref[direction, h], F32))

        row = lax.broadcasted_iota(jnp.int32, (C, C), 0).astype(F32)
        col = lax.broadcasted_iota(jnp.int32, (C, C), 1).astype(F32)
        n = lax.broadcasted_iota(jnp.int32, (C, RET_HD), 0).astype(F32)
        lg = log_g((C, RET_HD))
        if direction == 0:
            diff = row - col
            mask = diff >= 0.0
            k_dec, q_dec = jnp.exp(lg * (C - 1.0 - n)), jnp.exp(lg * (n + 1.0))
        else:
            diff = col - row
            mask = diff > 0.0
            k_dec, q_dec = jnp.exp(lg * n), jnp.exp(lg * (C - n))
        dmat = jnp.where(mask, jnp.exp(log_g((C, C)) * jnp.maximum(diff, 0.0)), 0.0)
        return dmat, k_dec, q_dec, jnp.exp(log_g((RET_HD, RET_HD)) * C)

    def chunk_out(c, state, dmat, k_dec, q_dec, g_chunk):
        sl = pl.ds(pl.multiple_of(c * C, C), C)
        qc, kc, vc = q_ref[sl, :], k_ref[sl, :], v_ref[sl, :]
        qk = lax.dot_general(qc, kc, nt_dims, preferred_element_type=F32)
        inner = (qk * dmat).astype(BF16)
        o = jnp.dot(inner, vc, preferred_element_type=F32)
        o = o + jnp.dot((qc.astype(F32) * q_dec).astype(BF16), state.astype(BF16),
                        preferred_element_type=F32)
        kd = (kc.astype(F32) * k_dec).astype(BF16)
        kv = lax.dot_general(kd, vc, tn_dims, preferred_element_type=F32)
        return sl, o, state * g_chunk + kv

    cf, cb = consts(0), consts(1)

    def scan(i, states):
        sl_f, o_f, st_f = chunk_out(i, states[0], *cf)
        of_scr[sl_f, :] = o_f
        sl_b, o_b, st_b = chunk_out(nc - 1 - i, states[1], *cb)
        ob_scr[sl_b, :] = o_b
        return st_f, st_b

    zero = jnp.zeros((RET_HD, RET_HD), F32)
    lax.fori_loop(0, nc, scan, (zero, zero), unroll=2)

    gn = gn_ref[...]

    def finish(c, carry):
        sl = pl.ds(pl.multiple_of(c * C, C), C)
        o = of_scr[sl, :] + ob_scr[sl, :]
        mu = jnp.mean(o, axis=-1, keepdims=True)
        var = jnp.mean(jnp.square(o - mu), axis=-1, keepdims=True)
        y = (o - mu) * lax.rsqrt(var + NORM_EPS) * gn
        g = g_ref[sl, :].astype(F32)
        o_ref[sl, :] = (y * (g * (1.0 / (1.0 + jnp.exp(-g))))).astype(o_ref.dtype)
        return carry

    lax.fori_loop(0, nc, finish, 0, unroll=2)


def _retention(qkv4, decays, gn, B, S):
    spec = lambda off: pl.BlockSpec((None, None, S, LANES), lambda b, h: (off + h, b, 0, 0))
    return pl.pallas_call(
        functools.partial(_ret_kernel, S=S),
        grid=(B, RET_HEADS),
        in_specs=[pl.BlockSpec(memory_space=pltpu.SMEM),
                  spec(0), spec(RET_HEADS), spec(2 * RET_HEADS), spec(3 * RET_HEADS),
                  pl.BlockSpec((1, LANES), lambda b, h: (0, h))],
        out_specs=pl.BlockSpec((None, S, LANES), lambda b, h: (b, 0, h)),
        out_shape=jax.ShapeDtypeStruct((B, S, RET_W), BF16),
        scratch_shapes=[pltpu.VMEM((S, LANES), F32), pltpu.VMEM((S, LANES), F32)],
        compiler_params=_cparams(("parallel", "arbitrary")),
        name="retention",
    )(decays, qkv4, qkv4, qkv4, qkv4, gn)


def _outproj_kernel(x_ref, att_ref, ret_ref, wo_ref, g2_ref, wr_ref, x1_ref, h2_ref, aff_ref):
    y = jnp.dot(att_ref[...], wo_ref[:ATT_W, :], preferred_element_type=F32)
    y = y + jnp.dot(ret_ref[...], wo_ref[ATT_W:, :], preferred_element_type=F32)
    x1 = x_ref[...] + y
    x1_ref[...] = x1
    ms = jnp.mean(x1 * x1, axis=-1, keepdims=True)
    h2 = x1 * lax.rsqrt(ms + NORM_EPS) * g2_ref[...]
    h2_ref[...] = h2.astype(h2_ref.dtype)
    logits = lax.dot_general(wr_ref[...], h2.astype(BF16), (((1,), (1,)), ((), ())),
                             preferred_element_type=F32)
    e = jnp.exp(logits - jnp.max(logits, axis=0, keepdims=True))
    aff_ref[...] = e / jnp.sum(e, axis=0, keepdims=True)


def _outproj(x2d, att, ret, wo_bf, g2, wr_t_bf, tm):
    T = x2d.shape[0]
    full = lambda shape: pl.BlockSpec(shape, lambda i: (0,) * len(shape))
    return pl.pallas_call(
        _outproj_kernel,
        grid=(T // tm,),
        in_specs=[pl.BlockSpec((tm, D_MODEL), lambda i: (i, 0)),
                  pl.BlockSpec((tm, ATT_W), lambda i: (i, 0)),
                  pl.BlockSpec((tm, RET_W), lambda i: (i, 0)),
                  full((ATT_W + RET_W, D_MODEL)), full((1, D_MODEL)), full((N_EXPERTS, D_MODEL))],
        out_specs=[pl.BlockSpec((tm, D_MODEL), lambda i: (i, 0)),
                   pl.BlockSpec((tm, D_MODEL), lambda i: (i, 0)),
                   pl.BlockSpec((N_EXPERTS, tm), lambda i: (0, i))],
        out_shape=[jax.ShapeDtypeStruct((T, D_MODEL), F32),
                   jax.ShapeDtypeStruct((T, D_MODEL), BF16),
                   jax.ShapeDtypeStruct((N_EXPERTS, T), F32)],
        compiler_params=_cparams(("parallel",)),
        name="outproj_router",
    )(x2d, att, ret, wo_bf, g2, wr_t_bf)


def _select_kernel(aff_ref, sel_ref, *, cap):
    aff = aff_ref[...]
    E, T = aff.shape
    capf = jnp.float32(cap)

    def count(mask):
        return jnp.sum(jnp.where(mask, 1.0, 0.0), axis=1, keepdims=True)

    def thr_step(i, thr_bits):
        cand = thr_bits | jnp.left_shift(jnp.int32(1), 30 - i)
        return jnp.where(count(aff >= pltpu.bitcast(cand, F32)) >= capf, cand, thr_bits)

    thr = pltpu.bitcast(lax.fori_loop(0, 31, thr_step, jnp.zeros((E, 1), jnp.int32)), F32)
    above = aff > thr
    ties = aff == thr
    need = capf - count(above)
    idx = lax.broadcasted_iota(jnp.int32, (E, T), 1)
    n_idx_bits = max(1, (T - 1).bit_length())

    def cut_step(i, cut):
        cand = cut | jnp.left_shift(jnp.int32(1), n_idx_bits - 1 - i)
        return jnp.where(count(ties & (idx < cand)) < need, cand, cut)

    cut = lax.fori_loop(0, n_idx_bits, cut_step, jnp.zeros((E, 1), jnp.int32))
    sel_ref[...] = jnp.where(above | (ties & (idx <= cut)), 1, 0).astype(jnp.int32)


def _select(aff_t, cap):
    E, T = aff_t.shape
    return pl.pallas_call(
        functools.partial(_select_kernel, cap=cap),
        grid=(1,),
        in_specs=[pl.BlockSpec((E, T), lambda i: (0, 0))],
        out_specs=pl.BlockSpec((E, T), lambda i: (0, 0)),
        out_shape=jax.ShapeDtypeStruct((E, T), jnp.int32),
        compiler_params=_cparams(("arbitrary",)),
        name="expert_select",
    )(aff_t)


CMB_TOK = 512


def _compact_kernel(excl_ref, cnt_ref, pos_ref, idx_ref, *, cap, nblk):
    b = pl.program_id(0)
    ntiles = cap // LANES
    width = 2 * LANES

    @pl.when(b == 0)
    def _():
        idx_ref[...] = jnp.zeros(idx_ref.shape, F32)

    lane = lax.broadcasted_iota(jnp.int32, (CMB_TOK, width), 1)
    tok = (b * CMB_TOK + lax.broadcasted_iota(jnp.int32, (CMB_TOK, width), 0)).astype(F32)

    def window(e, tile_nominal):
        tile = jnp.minimum(tile_nominal, ntiles - 2)
        pos_col = pos_ref[:, e:e + 1]
        hit = ((pos_col - tile * LANES) == lane) & (pos_col >= tile_nominal * LANES)
        row = jnp.sum(jnp.where(hit, tok, 0.0), axis=0, keepdims=True)
        for j in range(2):
            idx_ref[e, pl.ds(tile + j, 1), :] += row[:, j * LANES:(j + 1) * LANES]

    for e in range(N_EXPERTS):
        first = excl_ref[e * nblk + b]
        tile0 = first // LANES
        window(e, tile0)
        need_end = first + cnt_ref[e * nblk + b]
        n_extra = jnp.maximum((need_end - tile0 * LANES + width - 1) // width - 1, 0)

        def extra(k, carry):
            window(e, tile0 + 2 * (k + 1))
            return carry

        lax.fori_loop(0, n_extra, extra, 0)


def _compact(pos_tok, excl, cnt, cap):
    T, E = pos_tok.shape
    nblk = T // CMB_TOK
    grid_spec = pltpu.PrefetchScalarGridSpec(
        num_scalar_prefetch=2,
        grid=(nblk,),
        in_specs=[pl.BlockSpec((CMB_TOK, E), lambda b, *_: (b, 0))],
        out_specs=pl.BlockSpec((E, cap // LANES, LANES), lambda b, *_: (0, 0, 0)),
    )
    idx = pl.pallas_call(
        functools.partial(_compact_kernel, cap=cap, nblk=nblk),
        grid_spec=grid_spec,
        out_shape=jax.ShapeDtypeStruct((E, cap // LANES, LANES), F32),
        compiler_params=_cparams(("arbitrary",)),
        name="expert_lists",
    )(excl, cnt, pos_tok)
    return idx.reshape(E, cap).astype(jnp.int32)


FF_CHUNK = 512


def _ffn_kernel(xin_ref, wg_ref, wu_ref, wd_ref, gate_ref, y_ref):
    xin = xin_ref[...]
    acc = jnp.zeros(y_ref.shape, F32)
    for f in range(D_FF // FF_CHUNK):
        fs = slice(f * FF_CHUNK, (f + 1) * FF_CHUNK)
        a = jnp.dot(xin, wg_ref[:, fs], preferred_element_type=F32)
        u = jnp.dot(xin, wu_ref[:, fs], preferred_element_type=F32)
        hmid = (a * (1.0 / (1.0 + jnp.exp(-a))) * u).astype(BF16)
        acc = acc + jnp.dot(hmid, wd_ref[fs, :], preferred_element_type=F32)
    y_ref[...] = (acc * gate_ref[...]).astype(y_ref.dtype)


def _ffn(xin, wg_bf, wu_bf, wd_bf, gates, tm):
    E, cap, _ = xin.shape
    return pl.pallas_call(
        _ffn_kernel,
        grid=(E, cap // tm),
        in_specs=[pl.BlockSpec((None, tm, D_MODEL), lambda e, j: (e, j, 0)),
                  pl.BlockSpec((None, D_MODEL, D_FF), lambda e, j: (e, 0, 0)),
                  pl.BlockSpec((None, D_MODEL, D_FF), lambda e, j: (e, 0, 0)),
                  pl.BlockSpec((None, D_FF, D_MODEL), lambda e, j: (e, 0, 0)),
                  pl.BlockSpec((None, tm, 1), lambda e, j: (e, j, 0))],
        out_specs=pl.BlockSpec((None, tm, D_MODEL), lambda e, j: (e, j, 0)),
        out_shape=jax.ShapeDtypeStruct((E, cap, D_MODEL), BF16),
        compiler_params=_cparams(("parallel", "arbitrary")),
        name="expert_ffn",
    )(xin, wg_bf, wu_bf, wd_bf, gates)


CMB_ROWS = 128
BF16_SUBLANES = 16


def _combine_kernel(excl_ref, cnt_ref, x1_ref, pos_ref, y_hbm, o_ref, ybuf, xbuf, sem, xsem, *, cap, nblk):
    b = pl.program_id(0)
    slot = b % 2
    lane = lax.broadcasted_iota(jnp.int32, (CMB_TOK, CMB_ROWS), 1)

    def first_row(e, blk):
        return (excl_ref[e * nblk + blk] // BF16_SUBLANES) * BF16_SUBLANES

    def fetch(e, row, dst, s):
        row = pl.multiple_of(jnp.minimum(row, cap - CMB_ROWS), BF16_SUBLANES)
        return pltpu.make_async_copy(y_hbm.at[e, pl.ds(row, CMB_ROWS), :], dst, s)

    def start_block(blk, sl):
        for e in range(N_EXPERTS):
            fetch(e, first_row(e, blk), ybuf.at[sl, e], sem.at[sl, e]).start()

    @pl.when(b == 0)
    def _():
        start_block(0, 0)

    @pl.when(b + 1 < nblk)
    def _():
        start_block(b + 1, 1 - slot)

    def onehot(pos_col, nominal):
        hit = ((pos_col - jnp.minimum(nominal, cap - CMB_ROWS)) == lane) & (pos_col >= nominal)
        return jnp.where(hit, 1.0, 0.0).astype(BF16)

    parts = []
    for e in range(N_EXPERTS):
        nominal0 = first_row(e, b)
        fetch(e, nominal0, ybuf.at[slot, e], sem.at[slot, e]).wait()
        parts.append(onehot(pos_ref[:, e:e + 1], nominal0))
    rows = ybuf[slot].reshape(N_EXPERTS * CMB_ROWS, D_MODEL)
    o_ref[...] = x1_ref[...] + jnp.dot(jnp.concatenate(parts, axis=1), rows, preferred_element_type=F32)

    for e in range(N_EXPERTS):
        nominal0 = first_row(e, b)
        need_end = excl_ref[e * nblk + b] + cnt_ref[e * nblk + b]
        n_extra = jnp.maximum((need_end - nominal0 + CMB_ROWS - 1) // CMB_ROWS - 1, 0)

        def extra(k, carry):
            nominal = nominal0 + (k + 1) * CMB_ROWS
            cp = fetch(e, nominal, xbuf, xsem)
            cp.start()
            cp.wait()
            o_ref[...] += jnp.dot(onehot(pos_ref[:, e:e + 1], nominal), xbuf[...], preferred_element_type=F32)
            return carry

        lax.fori_loop(0, n_extra, extra, 0)


def _combine(x1, pos_tok, y, excl, cnt):
    T = x1.shape[0]
    E, cap, _ = y.shape
    nblk = T // CMB_TOK
    grid_spec = pltpu.PrefetchScalarGridSpec(
        num_scalar_prefetch=2,
        grid=(nblk,),
        in_specs=[pl.BlockSpec((CMB_TOK, D_MODEL), lambda b, *_: (b, 0)),
                  pl.BlockSpec((CMB_TOK, E), lambda b, *_: (b, 0)),
                  pl.BlockSpec(memory_space=pl.ANY)],
        out_specs=pl.BlockSpec((CMB_TOK, D_MODEL), lambda b, *_: (b, 0)),
        scratch_shapes=[pltpu.VMEM((2, E, CMB_ROWS, D_MODEL), BF16),
                        pltpu.VMEM((CMB_ROWS, D_MODEL), BF16),
                        pltpu.SemaphoreType.DMA((2, E)),
                        pltpu.SemaphoreType.DMA(())],
    )
    return pl.pallas_call(
        functools.partial(_combine_kernel, cap=cap, nblk=nblk),
        grid_spec=grid_spec,
        out_shape=jax.ShapeDtypeStruct((T, D_MODEL), F32),
        compiler_params=_cparams(("arbitrary",)),
        name="moe_combine",
    )(excl, cnt, x1, pos_tok, y)


def _layer(x, p):
    B, S, _ = x.shape
    T = B * S
    tm = 512
    x2d = x.reshape(T, D_MODEL)
    qkv_att, qkv_ret = _inproj(x2d, S, p["g1"], p["w_in"], p["gq"], p["gk"], _rope_tables(S), tm)
    att = _attention(qkv_att.reshape(N_ATT_SLABS, B, S, LANES), B, S).reshape(T, ATT_W)
    ret = _retention(qkv_ret.reshape(N_SLABS - N_ATT_SLABS, B, S, LANES), p["decays"], p["gn"], B, S).reshape(T, RET_W)
    x1, h2, aff_t = _outproj(x2d, att, ret, p["w_out"], p["g2"], p["w_router_t"], tm)

    cap = CAPACITY_FACTOR * T // N_EXPERTS
    sel = _select(aff_t, cap)

    csum = jnp.cumsum(sel, axis=1)
    pos_tok = jnp.where(sel > 0, csum - 1, -1).T
    ends = csum[:, CMB_TOK - 1::CMB_TOK]
    excl = jnp.concatenate([jnp.zeros((N_EXPERTS, 1), jnp.int32), ends[:, :-1]], axis=1).reshape(-1)
    cnt = ends.reshape(-1) - excl

    idx = _compact(pos_tok, excl, cnt, cap)
    xin = jnp.take(h2, idx.reshape(-1), axis=0).reshape(N_EXPERTS, cap, D_MODEL)
    gates = jnp.take_along_axis(aff_t, idx, axis=1)[..., None]
    y = _ffn(xin, p["w_gate"], p["w_up"], p["w_down"], gates, tm)
    out = _combine(x1, pos_tok, y, excl, cnt)
    return out.reshape(B, S, D_MODEL)


def kernel(x_prompt, x_sample, norm1_g, w_in, attn_qnorm_g, attn_knorm_g, ret_decay_fwd, ret_decay_bwd,
           ret_norm_g, w_out, norm2_g, w_router, w_gate_e, w_up_e, w_down_e):
    y_prompt, y_sample = x_prompt, x_sample
    for l in range(norm1_g.shape[0]):
        p = {
            "g1": norm1_g[l][None, :],
            "w_in": w_in[l].astype(BF16),
            "gq": jnp.tile(attn_qnorm_g[l], LANES // ATT_HD)[None, :],
            "gk": jnp.tile(attn_knorm_g[l], LANES // ATT_HD)[None, :],
            "decays": jnp.stack([ret_decay_fwd[l], ret_decay_bwd[l]]).astype(F32),
            "gn": ret_norm_g[l][None, :].astype(F32),
            "w_out": w_out[l].astype(BF16),
            "g2": norm2_g[l][None, :],
            "w_router_t": w_router[l].T.astype(BF16),
            "w_gate": w_gate_e[l].astype(BF16),
            "w_up": w_up_e[l].astype(BF16),
            "w_down": w_down_e[l].astype(BF16),
        }
        y_prompt = _layer(y_prompt, p)
        y_sample = _layer(y_sample, p)
    return (y_prompt, y_sample)
```

```python
import functools

import jax
import jax.numpy as jnp
import numpy as np
from jax import lax
from jax.experimental import pallas as pl
from jax.experimental.pallas import tpu as pltpu
from jax.experimental.pallas import tpu_sc as plsc

F32 = jnp.float32
BF16 = jnp.bfloat16

D_MODEL = 1024
ATT_HEADS, ATT_HD = 8, 64
RET_HEADS, RET_HD = 4, 128
ATT_W = ATT_HEADS * ATT_HD
RET_W = RET_HEADS * RET_HD
IN_W = 3 * ATT_W + 4 * RET_W
LANES = 128
N_SLABS = IN_W // LANES
GROUP_W = 512
SLABS_PER_GROUP = GROUP_W // LANES
N_ATT_SLABS = 3 * ATT_W // LANES
RET_CHUNK = 256
ROPE_THETA = 10000.0
N_EXPERTS = 16
D_FF = 2048
CAPACITY_FACTOR = 2
NORM_EPS = 1e-6
NEG_INF = -1e30
ATT_QBLK = 128
VMEM_LIMIT = 56 * 1024 * 1024


def _cparams(sem):
    return pltpu.CompilerParams(dimension_semantics=sem, vmem_limit_bytes=VMEM_LIMIT)


def _inproj_kernel(x_ref, g1_ref, w_ref, gq_ref, gk_ref, ca_ref, sa_ref, cr_ref, sr_ref, oa_ref, or_ref):
    x = x_ref[...]
    ms = jnp.mean(x * x, axis=-1, keepdims=True)
    h = (x * lax.rsqrt(ms + NORM_EPS) * g1_ref[...]).astype(BF16)
    tm = x.shape[0]
    lane = lax.broadcasted_iota(jnp.int32, (tm, LANES), 1)
    first = lane < ATT_HD
    low_half = (lane & (ATT_HD // 2)) == 0
    ca, sa, cr, sr = ca_ref[...], sa_ref[...], cr_ref[...], sr_ref[...]
    for grp in range(IN_W // GROUP_W):
        acc = jnp.dot(h, w_ref[:, grp * GROUP_W:(grp + 1) * GROUP_W], preferred_element_type=F32)
        for p in range(SLABS_PER_GROUP):
            a = acc[:, p * LANES:(p + 1) * LANES]
            if grp in (0, 1):
                sq = a * a
                s0 = jnp.sum(jnp.where(first, sq, 0.0), axis=-1, keepdims=True)
                s1 = jnp.sum(jnp.where(first, 0.0, sq), axis=-1, keepdims=True)
                ms2 = jnp.where(first, s0, s1) * (1.0 / ATT_HD)
                g = gq_ref[...] if grp == 0 else gk_ref[...]
                y = a * lax.rsqrt(ms2 + NORM_EPS) * g
                partner = jnp.where(low_half, pltpu.roll(y, LANES - ATT_HD // 2, 1),
                                    pltpu.roll(y, ATT_HD // 2, 1))
                r = y * ca + partner * sa
                if grp == 0:
                    r = r * (ATT_HD ** -0.5)
            elif grp in (3, 4):
                r = a * cr + pltpu.roll(a, RET_HD // 2, 1) * sr
                if grp == 4:
                    r = r * (RET_HD ** -0.5)
            else:
                r = a
            slab = grp * SLABS_PER_GROUP + p
            if slab < N_ATT_SLABS:
                oa_ref[slab] = r
            else:
                or_ref[slab - N_ATT_SLABS] = r.astype(BF16)


def _rope_tables(S):
    pos = jnp.arange(S, dtype=F32)

    def table(hd):
        inv_freq = ROPE_THETA ** (-jnp.arange(0, hd, 2, dtype=F32) / hd)
        ang = pos[:, None] * inv_freq[None, :]
        cos, sin = jnp.cos(ang), jnp.sin(ang)
        reps = LANES // hd
        cos_t = jnp.tile(jnp.concatenate([cos, cos], axis=-1), (1, reps))
        sin_t = jnp.tile(jnp.concatenate([-sin, sin], axis=-1), (1, reps))
        return cos_t, sin_t

    ca, sa = table(ATT_HD)
    cr, sr = table(RET_HD)
    return ca, sa, cr, sr


def _inproj(x2d, S, g1, w_in_bf, gq, gk, tables, tm):
    T = x2d.shape[0]
    n_pos_blk = S // tm
    tab_spec = pl.BlockSpec((tm, LANES), lambda i: (i % n_pos_blk, 0))
    full = lambda shape: pl.BlockSpec(shape, lambda i: (0,) * len(shape))
    return pl.pallas_call(
        _inproj_kernel,
        grid=(T // tm,),
        in_specs=[pl.BlockSpec((tm, D_MODEL), lambda i: (i, 0)), full((1, D_MODEL)),
                  full((D_MODEL, IN_W)), full((1, LANES)), full((1, LANES)),
                  tab_spec, tab_spec, tab_spec, tab_spec],
        out_specs=[pl.BlockSpec((N_ATT_SLABS, tm, LANES), lambda i: (0, i, 0)),
                   pl.BlockSpec((N_SLABS - N_ATT_SLABS, tm, LANES), lambda i: (0, i, 0))],
        out_shape=[jax.ShapeDtypeStruct((N_ATT_SLABS, T, LANES), F32),
                   jax.ShapeDtypeStruct((N_SLABS - N_ATT_SLABS, T, LANES), BF16)],
        compiler_params=_cparams(("parallel",)),
        name="inproj",
    )(x2d, g1, w_in_bf, gq, gk, *tables)


DILATED_PATTERNS = ((128, 1), (512, 4), (2048, 16))
ATT_HALF = 64
ATT_SB = 2048
ATT_NB = ATT_SB // ATT_QBLK
ATT_UNROLL = 2


def _attn_kernel(q_ref, k_ref, v_ref, o_ref, o_scr, lse_scr, bias_scr, *, S):
    t0 = pl.program_id(2) * ATT_SB
    lane = lax.broadcasted_iota(jnp.int32, (ATT_QBLK, LANES), 1)
    first = lane < ATT_HD
    nt_dims = (((1,), (1,)), ((), ()))

    rel = (lax.broadcasted_iota(jnp.int32, (ATT_QBLK, 2 * ATT_QBLK), 0)
           - lax.broadcasted_iota(jnp.int32, (ATT_QBLK, 2 * ATT_QBLK), 1))
    for oi in range(3):
        bias_scr[oi] = jnp.where(jnp.abs(rel + oi * ATT_HALF) <= ATT_HALF, 0.0, NEG_INF)

    def band(qv, kv, vv, off):
        kw = kv.shape[0]
        bias = bias_scr[off // ATT_HALF][:, :kw]
        zero = jnp.zeros_like(qv)
        q2 = jnp.concatenate([jnp.where(first, qv, zero), jnp.where(first, zero, qv)], axis=0)
        s = lax.dot_general(q2, kv, nt_dims, preferred_element_type=F32) + jnp.concatenate([bias, bias], axis=0)
        m = jnp.max(s, axis=-1, keepdims=True)
        p = jnp.exp(s - m)
        l = jnp.sum(p, axis=-1, keepdims=True)
        acc = jnp.dot(p.astype(BF16), vv, preferred_element_type=F32) * (1.0 / l)
        lse = m + jnp.log(l)
        return (jnp.where(first, acc[:ATT_QBLK], acc[ATT_QBLK:]),
                jnp.where(first, lse[:ATT_QBLK], lse[ATT_QBLK:]))

    def step(n, carry):
        for pi, (window, dil) in enumerate(DILATED_PATTERNS):
            L = S // dil
            kw = min(2 * ATT_QBLK, L)
            per_class = ATT_SB // dil // ATT_QBLK
            r = n // per_class
            bi = n % per_class
            lq0 = t0 // dil + bi * ATT_QBLK
            lk0 = jnp.clip(lq0 - ATT_HALF, 0, L - kw)
            qrow = dil * ATT_QBLK * bi + r
            krow = dil * lk0 + r
            if dil == 1:
                qs = pl.ds(pl.multiple_of(qrow, ATT_QBLK), ATT_QBLK)
                ks = pl.ds(pl.multiple_of(krow, 8), kw)
            else:
                qs = pl.ds(qrow, ATT_QBLK, stride=dil)
                ks = pl.ds(krow, kw, stride=dil)
            o, lse = band(q_ref[qs, :].astype(BF16), k_ref[ks, :].astype(BF16), v_ref[ks, :].astype(BF16),
                          lq0 - lk0)
            o_scr[pi, qs, :] = o
            lse_scr[pi, qs, :] = lse
        return carry

    lax.fori_loop(0, ATT_NB, step, 0, unroll=ATT_UNROLL)

    def merge(c, carry):
        sl = pl.ds(pl.multiple_of(c * ATT_QBLK, ATT_QBLK), ATT_QBLK)
        lses = [lse_scr[pi, sl, :] for pi in range(3)]
        mx = jnp.maximum(jnp.maximum(lses[0], lses[1]), lses[2])
        ws = [jnp.exp(x - mx) for x in lses]
        num = ws[0] * o_scr[0, sl, :] + ws[1] * o_scr[1, sl, :] + ws[2] * o_scr[2, sl, :]
        o_ref[sl, :] = (num / (ws[0] + ws[1] + ws[2])).astype(o_ref.dtype)
        return carry

    lax.fori_loop(0, ATT_NB, merge, 0)


def _attention(qkv4, B, S):
    n_pairs = ATT_W // LANES
    kv_spec = lambda off: pl.BlockSpec((None, None, S, LANES), lambda b, p, i: (off + p, b, 0, 0))
    return pl.pallas_call(
        functools.partial(_attn_kernel, S=S),
        grid=(B, n_pairs, S // ATT_SB),
        in_specs=[pl.BlockSpec((None, None, ATT_SB, LANES), lambda b, p, i: (p, b, i, 0)),
                  kv_spec(n_pairs), kv_spec(2 * n_pairs)],
        out_specs=pl.BlockSpec((None, ATT_SB, LANES), lambda b, p, i: (b, i, p)),
        out_shape=jax.ShapeDtypeStruct((B, S, ATT_W), BF16),
        scratch_shapes=[pltpu.VMEM((3, ATT_SB, LANES), F32), pltpu.VMEM((3, ATT_SB, LANES), F32),
                        pltpu.VMEM((3, ATT_QBLK, 2 * ATT_QBLK), F32)],
        compiler_params=_cparams(("parallel", "parallel", "arbitrary")),
        name="dilated_attn",
    )(qkv4, qkv4, qkv4)


def _ret_kernel(dec_ref, q_ref, k_ref, v_ref, g_ref, gn_ref, o_ref, of_scr, ob_scr, *, S):
    C = RET_CHUNK
    nc = S // C
    h = pl.program_id(1)
    nt_dims = (((1,), (1,)), ((), ()))
    tn_dims = (((0,), (0,)), ((), ()))

    def consts(direction):
        def log_g(shape):
            return -jnp.exp(jnp.full(shape, dec_ref[direction, h], F32))

        row = lax.broadcasted_iota(jnp.int32, (C, C), 0).astype(F32)
        col = lax.broadcasted_iota(jnp.int32, (C, C), 1).astype(F32)
        n = lax.broadcasted_iota(jnp.int32, (C, RET_HD), 0).astype(F32)
        lg = log_g((C, RET_HD))
        if direction == 0:
            diff = row - col
            mask = diff >= 0.0
            k_dec, q_dec = jnp.exp(lg * (C - 1.0 - n)), jnp.exp(lg * (n + 1.0))
        else:
            diff = col - row
            mask = diff > 0.0
            k_dec, q_dec = jnp.exp(lg * n), jnp.exp(lg * (C - n))
        dmat = jnp.where(mask, jnp.exp(log_g((C, C)) * jnp.maximum(diff, 0.0)), 0.0)
        return dmat, k_dec, q_dec, jnp.exp(log_g((RET_HD, RET_HD)) * C)

    def chunk_out(c, state, dmat, k_dec, q_dec, g_chunk):
        sl = pl.ds(pl.multiple_of(c * C, C), C)
        qc, kc, vc = q_ref[sl, :], k_ref[sl, :], v_ref[sl, :]
        qk = lax.dot_general(qc, kc, nt_dims, preferred_element_type=F32)
        inner = (qk * dmat).astype(BF16)
        o = jnp.dot(inner, vc, preferred_element_type=F32)
        o = o + jnp.dot((qc.astype(F32) * q_dec).astype(BF16), state.astype(BF16),
                        preferred_element_type=F32)
        kd = (kc.astype(F32) * k_dec).astype(BF16)
        kv = lax.dot_general(kd, vc, tn_dims, preferred_element_type=F32)
        return sl, o, state * g_chunk + kv

    cf, cb = consts(0), consts(1)

    def scan(i, states):
        sl_f, o_f, st_f = chunk_out(i, states[0], *cf)
        of_scr[sl_f, :] = o_f
        sl_b, o_b, st_b = chunk_out(nc - 1 - i, states[1], *cb)
        ob_scr[sl_b, :] = o_b
        return st_f, st_b

    zero = jnp.zeros((RET_HD, RET_HD), F32)
    lax.fori_loop(0, nc, scan, (zero, zero), unroll=2)

    gn = gn_ref[...]

    def finish(c, carry):
        sl = pl.ds(pl.multiple_of(c * C, C), C)
        o = of_scr[sl, :] + ob_scr[sl, :]
        mu = jnp.mean(o, axis=-1, keepdims=True)
        var = jnp.mean(jnp.square(o - mu), axis=-1, keepdims=True)
        y = (o - mu) * lax.rsqrt(var + NORM_EPS) * gn
        g = g_ref[sl, :].astype(F32)
        o_ref[sl, :] = (y * (g * (1.0 / (1.0 + jnp.exp(-g))))).astype(o_ref.dtype)
        return carry

    lax.fori_loop(0, nc, finish, 0, unroll=2)


def _retention(qkv4, decays, gn, B, S):
    spec = lambda off: pl.BlockSpec((None, None, S, LANES), lambda b, h: (off + h, b, 0, 0))
    return pl.pallas_call(
        functools.partial(_ret_kernel, S=S),
        grid=(B, RET_HEADS),
        in_specs=[pl.BlockSpec(memory_space=pltpu.SMEM),
                  spec(0), spec(RET_HEADS), spec(2 * RET_HEADS), spec(3 * RET_HEADS),
                  pl.BlockSpec((1, LANES), lambda b, h: (0, h))],
        out_specs=pl.BlockSpec((None, S, LANES), lambda b, h: (b, 0, h)),
        out_shape=jax.ShapeDtypeStruct((B, S, RET_W), BF16),
        scratch_shapes=[pltpu.VMEM((S, LANES), F32), pltpu.VMEM((S, LANES), F32)],
        compiler_params=_cparams(("parallel", "arbitrary")),
        name="retention",
    )(decays, qkv4, qkv4, qkv4, qkv4, gn)


HI16 = -65536


def _pack_bf16_pairs(x):
    n = x.shape[1] // 2
    bits = pltpu.bitcast(x.astype(BF16).astype(F32), jnp.int32)
    return (bits[:, n:] & HI16) | lax.shift_right_logical(bits[:, :n], 16)


def _unpack_bf16_pairs(w):
    lo = pltpu.bitcast(lax.shift_left(w, 16), F32).astype(BF16)
    hi = pltpu.bitcast(w & HI16, F32).astype(BF16)
    return jnp.concatenate([lo, hi], axis=1)


def _outproj_kernel(x_ref, att_ref, ret_ref, wo_ref, g2_ref, wr_ref, x1_ref, h2_ref, aff_ref):
    y = jnp.dot(att_ref[...], wo_ref[:ATT_W, :], preferred_element_type=F32)
    y = y + jnp.dot(ret_ref[...], wo_ref[ATT_W:, :], preferred_element_type=F32)
    x1 = x_ref[...] + y
    x1_ref[...] = x1
    ms = jnp.mean(x1 * x1, axis=-1, keepdims=True)
    h2 = x1 * lax.rsqrt(ms + NORM_EPS) * g2_ref[...]
    h2_ref[...] = _pack_bf16_pairs(h2)
    logits = lax.dot_general(wr_ref[...], h2.astype(BF16), (((1,), (1,)), ((), ())),
                             preferred_element_type=F32)
    e = jnp.exp(logits - jnp.max(logits, axis=0, keepdims=True))
    aff_ref[...] = e / jnp.sum(e, axis=0, keepdims=True)


def _outproj(x2d, att, ret, wo_bf, g2, wr_t_bf, tm):
    T = x2d.shape[0]
    full = lambda shape: pl.BlockSpec(shape, lambda i: (0,) * len(shape))
    return pl.pallas_call(
        _outproj_kernel,
        grid=(T // tm,),
        in_specs=[pl.BlockSpec((tm, D_MODEL), lambda i: (i, 0)),
                  pl.BlockSpec((tm, ATT_W), lambda i: (i, 0)),
                  pl.BlockSpec((tm, RET_W), lambda i: (i, 0)),
                  full((ATT_W + RET_W, D_MODEL)), full((1, D_MODEL)), full((N_EXPERTS, D_MODEL))],
        out_specs=[pl.BlockSpec((tm, D_MODEL), lambda i: (i, 0)),
                   pl.BlockSpec((tm, D_MODEL // 2), lambda i: (i, 0)),
                   pl.BlockSpec((N_EXPERTS, tm), lambda i: (0, i))],
        out_shape=[jax.ShapeDtypeStruct((T, D_MODEL), F32),
                   jax.ShapeDtypeStruct((T, D_MODEL // 2), jnp.int32),
                   jax.ShapeDtypeStruct((N_EXPERTS, T), F32)],
        compiler_params=_cparams(("parallel",)),
        name="outproj_router",
    )(x2d, att, ret, wo_bf, g2, wr_t_bf)


def _select_kernel(aff_ref, sel_ref, *, cap):
    aff = aff_ref[...]
    E, T = aff.shape
    capf = jnp.float32(cap)

    def count(mask):
        return jnp.sum(jnp.where(mask, 1.0, 0.0), axis=1, keepdims=True)

    def thr_step(i, thr_bits):
        cand = thr_bits | jnp.left_shift(jnp.int32(1), 30 - i)
        return jnp.where(count(aff >= pltpu.bitcast(cand, F32)) >= capf, cand, thr_bits)

    thr = pltpu.bitcast(lax.fori_loop(0, 31, thr_step, jnp.zeros((E, 1), jnp.int32)), F32)
    above = aff > thr
    ties = aff == thr
    need = capf - count(above)
    idx = lax.broadcasted_iota(jnp.int32, (E, T), 1)
    n_idx_bits = max(1, (T - 1).bit_length())

    def cut_step(i, cut):
        cand = cut | jnp.left_shift(jnp.int32(1), n_idx_bits - 1 - i)
        return jnp.where(count(ties & (idx < cand)) < need, cand, cut)

    cut = lax.fori_loop(0, n_idx_bits, cut_step, jnp.zeros((E, 1), jnp.int32))
    sel_ref[...] = jnp.where(above | (ties & (idx <= cut)), 1, 0).astype(jnp.int32)


def _select(aff_t, cap):
    E, T = aff_t.shape
    return pl.pallas_call(
        functools.partial(_select_kernel, cap=cap),
        grid=(1,),
        in_specs=[pl.BlockSpec((E, T), lambda i: (0, 0))],
        out_specs=pl.BlockSpec((E, T), lambda i: (0, 0)),
        out_shape=jax.ShapeDtypeStruct((E, T), jnp.int32),
        compiler_params=_cparams(("arbitrary",)),
        name="expert_select",
    )(aff_t)


CMB_TOK = 512


def _compact_kernel(excl_ref, cnt_ref, pos_ref, idx_ref, *, cap, nblk):
    b = pl.program_id(0)
    ntiles = cap // LANES
    width = 2 * LANES

    @pl.when(b == 0)
    def _():
        idx_ref[...] = jnp.zeros(idx_ref.shape, F32)

    lane = lax.broadcasted_iota(jnp.int32, (CMB_TOK, width), 1)
    tok = (b * CMB_TOK + lax.broadcasted_iota(jnp.int32, (CMB_TOK, width), 0)).astype(F32)

    def window(e, tile_nominal):
        tile = jnp.minimum(tile_nominal, ntiles - 2)
        pos_col = pos_ref[:, e:e + 1]
        hit = ((pos_col - tile * LANES) == lane) & (pos_col >= tile_nominal * LANES)
        row = jnp.sum(jnp.where(hit, tok, 0.0), axis=0, keepdims=True)
        for j in range(2):
            idx_ref[e, pl.ds(tile + j, 1), :] += row[:, j * LANES:(j + 1) * LANES]

    for e in range(N_EXPERTS):
        first = excl_ref[e * nblk + b]
        tile0 = first // LANES
        window(e, tile0)
        need_end = first + cnt_ref[e * nblk + b]
        n_extra = jnp.maximum((need_end - tile0 * LANES + width - 1) // width - 1, 0)

        def extra(k, carry):
            window(e, tile0 + 2 * (k + 1))
            return carry

        lax.fori_loop(0, n_extra, extra, 0)


def _compact(pos_tok, excl, cnt, cap):
    T, E = pos_tok.shape
    nblk = T // CMB_TOK
    grid_spec = pltpu.PrefetchScalarGridSpec(
        num_scalar_prefetch=2,
        grid=(nblk,),
        in_specs=[pl.BlockSpec((CMB_TOK, E), lambda b, *_: (b, 0))],
        out_specs=pl.BlockSpec((E, cap // LANES, LANES), lambda b, *_: (0, 0, 0)),
    )
    idx = pl.pallas_call(
        functools.partial(_compact_kernel, cap=cap, nblk=nblk),
        grid_spec=grid_spec,
        out_shape=jax.ShapeDtypeStruct((E, cap // LANES, LANES), F32),
        compiler_params=_cparams(("arbitrary",)),
        name="expert_lists",
    )(excl, cnt, pos_tok)
    return idx.reshape(E, cap).astype(jnp.int32)


SC_CORES = 2
SC_SUBCORES = 16
SC_ROWS = 64


def _sc_gather(table, idx):
    n, width = idx.shape[0], table.shape[1]
    workers = SC_CORES * SC_SUBCORES
    per_worker = n // workers
    mesh = plsc.VectorSubcoreMesh(core_axis_name="c", subcore_axis_name="s")

    @functools.partial(
        pl.kernel, mesh=mesh,
        out_type=jax.ShapeDtypeStruct((n, width), table.dtype),
        scratch_types=[pltpu.VMEM((SC_ROWS,), jnp.int32), pltpu.VMEM((SC_ROWS, width), table.dtype),
                       pltpu.SemaphoreType.DMA],
    )
    def gather(table_hbm, idx_hbm, out_hbm, idx_v, rows_v, sem):
        base = (lax.axis_index("s") * SC_CORES + lax.axis_index("c")) * per_worker

        @pl.loop(0, per_worker // SC_ROWS)
        def _(i):
            off = pl.multiple_of(base + i * SC_ROWS, SC_ROWS)
            pltpu.sync_copy(idx_hbm.at[pl.ds(off, SC_ROWS)], idx_v)
            pltpu.async_copy(table_hbm.at[idx_v], rows_v, sem).wait()
            pltpu.sync_copy(rows_v, out_hbm.at[pl.ds(off, SC_ROWS)])

    return gather(table, idx)


FF_CHUNK = 512


def _ffn_kernel(xin_ref, wg_ref, wu_ref, wd_ref, gate_ref, y_ref):
    xin = _unpack_bf16_pairs(xin_ref[...])
    acc = jnp.zeros(y_ref.shape, F32)
    for f in range(D_FF // FF_CHUNK):
        fs = slice(f * FF_CHUNK, (f + 1) * FF_CHUNK)
        a = jnp.dot(xin, wg_ref[:, fs], preferred_element_type=F32)
        u = jnp.dot(xin, wu_ref[:, fs], preferred_element_type=F32)
        hmid = (a * (1.0 / (1.0 + jnp.exp(-a))) * u).astype(BF16)
        acc = acc + jnp.dot(hmid, wd_ref[fs, :], preferred_element_type=F32)
    y_ref[...] = (acc * gate_ref[...]).astype(y_ref.dtype)


def _ffn(xin, wg_bf, wu_bf, wd_bf, gates, tm):
    E, cap, _ = xin.shape
    return pl.pallas_call(
        _ffn_kernel,
        grid=(E, cap // tm),
        in_specs=[pl.BlockSpec((None, tm, D_MODEL // 2), lambda e, j: (e, j, 0)),
                  pl.BlockSpec((None, D_MODEL, D_FF), lambda e, j: (e, 0, 0)),
                  pl.BlockSpec((None, D_MODEL, D_FF), lambda e, j: (e, 0, 0)),
                  pl.BlockSpec((None, D_FF, D_MODEL), lambda e, j: (e, 0, 0)),
                  pl.BlockSpec((None, tm, 1), lambda e, j: (e, j, 0))],
        out_specs=pl.BlockSpec((None, tm, D_MODEL), lambda e, j: (e, j, 0)),
        out_shape=jax.ShapeDtypeStruct((E, cap, D_MODEL), BF16),
        compiler_params=_cparams(("parallel", "arbitrary")),
        name="expert_ffn",
    )(xin, wg_bf, wu_bf, wd_bf, gates)


CMB_ROWS = 128
BF16_SUBLANES = 16


def _combine_kernel(excl_ref, cnt_ref, x1_ref, pos_ref, y_hbm, o_ref, ybuf, xbuf, sem, xsem, *, cap, nblk):
    b = pl.program_id(0)
    slot = b % 2
    lane = lax.broadcasted_iota(jnp.int32, (CMB_TOK, CMB_ROWS), 1)

    def first_row(e, blk):
        return (excl_ref[e * nblk + blk] // BF16_SUBLANES) * BF16_SUBLANES

    def fetch(e, row, dst, s):
        row = pl.multiple_of(jnp.minimum(row, cap - CMB_ROWS), BF16_SUBLANES)
        return pltpu.make_async_copy(y_hbm.at[e, pl.ds(row, CMB_ROWS), :], dst, s)

    def start_block(blk, sl):
        for e in range(N_EXPERTS):
            fetch(e, first_row(e, blk), ybuf.at[sl, e], sem.at[sl, e]).start()

    @pl.when(b == 0)
    def _():
        start_block(0, 0)

    @pl.when(b + 1 < nblk)
    def _():
        start_block(b + 1, 1 - slot)

    def onehot(pos_col, nominal):
        hit = ((pos_col - jnp.minimum(nominal, cap - CMB_ROWS)) == lane) & (pos_col >= nominal)
        return jnp.where(hit, 1.0, 0.0).astype(BF16)

    parts = []
    for e in range(N_EXPERTS):
        nominal0 = first_row(e, b)
        fetch(e, nominal0, ybuf.at[slot, e], sem.at[slot, e]).wait()
        parts.append(onehot(pos_ref[:, e:e + 1], nominal0))
    rows = ybuf[slot].reshape(N_EXPERTS * CMB_ROWS, D_MODEL)
    o_ref[...] = x1_ref[...] + jnp.dot(jnp.concatenate(parts, axis=1), rows, preferred_element_type=F32)

    for e in range(N_EXPERTS):
        nominal0 = first_row(e, b)
        need_end = excl_ref[e * nblk + b] + cnt_ref[e * nblk + b]
        n_extra = jnp.maximum((need_end - nominal0 + CMB_ROWS - 1) // CMB_ROWS - 1, 0)

        def extra(k, carry):
            nominal = nominal0 + (k + 1) * CMB_ROWS
            cp = fetch(e, nominal, xbuf, xsem)
            cp.start()
            cp.wait()
            o_ref[...] += jnp.dot(onehot(pos_ref[:, e:e + 1], nominal), xbuf[...], preferred_element_type=F32)
            return carry

        lax.fori_loop(0, n_extra, extra, 0)


def _combine(x1, pos_tok, y, excl, cnt):
    T = x1.shape[0]
    E, cap, _ = y.shape
    nblk = T // CMB_TOK
    grid_spec = pltpu.PrefetchScalarGridSpec(
        num_scalar_prefetch=2,
        grid=(nblk,),
        in_specs=[pl.BlockSpec((CMB_TOK, D_MODEL), lambda b, *_: (b, 0)),
                  pl.BlockSpec((CMB_TOK, E), lambda b, *_: (b, 0)),
                  pl.BlockSpec(memory_space=pl.ANY)],
        out_specs=pl.BlockSpec((CMB_TOK, D_MODEL), lambda b, *_: (b, 0)),
        scratch_shapes=[pltpu.VMEM((2, E, CMB_ROWS, D_MODEL), BF16),
                        pltpu.VMEM((CMB_ROWS, D_MODEL), BF16),
                        pltpu.SemaphoreType.DMA((2, E)),
                        pltpu.SemaphoreType.DMA(())],
    )
    return pl.pallas_call(
        functools.partial(_combine_kernel, cap=cap, nblk=nblk),
        grid_spec=grid_spec,
        out_shape=jax.ShapeDtypeStruct((T, D_MODEL), F32),
        compiler_params=_cparams(("arbitrary",)),
        name="moe_combine",
    )(excl, cnt, x1, pos_tok, y)


def _layer(x, p):
    B, S, _ = x.shape
    T = B * S
    tm = 512
    x2d = x.reshape(T, D_MODEL)
    qkv_att, qkv_ret = _inproj(x2d, S, p["g1"], p["w_in"], p["gq"], p["gk"], _rope_tables(S), tm)
    att = _attention(qkv_att.reshape(N_ATT_SLABS, B, S, LANES), B, S).reshape(T, ATT_W)
    ret = _retention(qkv_ret.reshape(N_SLABS - N_ATT_SLABS, B, S, LANES), p["decays"], p["gn"], B, S).reshape(T, RET_W)
    x1, h2, aff_t = _outproj(x2d, att, ret, p["w_out"], p["g2"], p["w_router_t"], tm)

    cap = CAPACITY_FACTOR * T // N_EXPERTS
    sel = _select(aff_t, cap)

    csum = jnp.cumsum(sel, axis=1)
    pos_tok = jnp.where(sel > 0, csum - 1, -1).T
    ends = csum[:, CMB_TOK - 1::CMB_TOK]
    excl = jnp.concatenate([jnp.zeros((N_EXPERTS, 1), jnp.int32), ends[:, :-1]], axis=1).reshape(-1)
    cnt = ends.reshape(-1) - excl

    idx = _compact(pos_tok, excl, cnt, cap)
    xin = _sc_gather(h2, idx.reshape(-1)).reshape(N_EXPERTS, cap, D_MODEL // 2)
    gates = jnp.take_along_axis(aff_t, idx, axis=1)[..., None]
    y = _ffn(xin, p["w_gate"], p["w_up"], p["w_down"], gates, tm)
    out = _combine(x1, pos_tok, y, excl, cnt)
    return out.reshape(B, S, D_MODEL)


def kernel(x_prompt, x_sample, norm1_g, w_in, attn_qnorm_g, attn_knorm_g, ret_decay_fwd, ret_decay_bwd,
           ret_norm_g, w_out, norm2_g, w_router, w_gate_e, w_up_e, w_down_e):
    y_prompt, y_sample = x_prompt, x_sample
    for l in range(norm1_g.shape[0]):
        p = {
            "g1": norm1_g[l][None, :],
            "w_in": w_in[l].astype(BF16),
            "gq": jnp.tile(attn_qnorm_g[l], LANES // ATT_HD)[None, :],
            "gk": jnp.tile(attn_knorm_g[l], LANES // ATT_HD)[None, :],
            "decays": jnp.stack([ret_decay_fwd[l], ret_decay_bwd[l]]).astype(F32),
            "gn": ret_norm_g[l][None, :].astype(F32),
            "w_out": w_out[l].astype(BF16),
            "g2": norm2_g[l][None, :],
            "w_router_t": w_router[l].T.astype(BF16),
            "w_gate": w_gate_e[l].astype(BF16),
            "w_up": w_up_e[l].astype(BF16),
            "w_down": w_down_e[l].astype(BF16),
        }
        y_prompt = _layer(y_prompt, p)
        y_sample = _layer(y_sample, p)
    return (y_prompt, y_sample)
```

```python
import functools

import jax
import jax.numpy as jnp
import numpy as np
from jax import lax
from jax.experimental import pallas as pl
from jax.experimental.pallas import tpu as pltpu
from jax.experimental.pallas import tpu_sc as plsc

F32 = jnp.float32
BF16 = jnp.bfloat16

D_MODEL = 1024
ATT_HEADS, ATT_HD = 8, 64
RET_HEADS, RET_HD = 4, 128
ATT_W = ATT_HEADS * ATT_HD
RET_W = RET_HEADS * RET_HD
IN_W = 3 * ATT_W + 4 * RET_W
LANES = 128
N_SLABS = IN_W // LANES
GROUP_W = 512
SLABS_PER_GROUP = GROUP_W // LANES
N_ATT_SLABS = 3 * ATT_W // LANES
RET_CHUNK = 256
ROPE_THETA = 10000.0
N_EXPERTS = 16
D_FF = 2048
CAPACITY_FACTOR = 2
NORM_EPS = 1e-6
NEG_INF = -1e30
ATT_QBLK = 128
VMEM_LIMIT = 56 * 1024 * 1024


def _cparams(sem):
    return pltpu.CompilerParams(dimension_semantics=sem, vmem_limit_bytes=VMEM_LIMIT)


def _inproj_kernel(x_ref, g1_ref, w_ref, gq_ref, gk_ref, ca_ref, sa_ref, cr_ref, sr_ref, oa_ref, or_ref):
    x = x_ref[...]
    ms = jnp.mean(x * x, axis=-1, keepdims=True)
    h = (x * lax.rsqrt(ms + NORM_EPS) * g1_ref[...]).astype(BF16)
    tm = x.shape[0]
    lane = lax.broadcasted_iota(jnp.int32, (tm, LANES), 1)
    first = lane < ATT_HD
    low_half = (lane & (ATT_HD // 2)) == 0
    ca, sa, cr, sr = ca_ref[...], sa_ref[...], cr_ref[...], sr_ref[...]
    for grp in range(IN_W // GROUP_W):
        acc = jnp.dot(h, w_ref[:, grp * GROUP_W:(grp + 1) * GROUP_W], preferred_element_type=F32)
        for p in range(SLABS_PER_GROUP):
            a = acc[:, p * LANES:(p + 1) * LANES]
            if grp in (0, 1):
                sq = a * a
                s0 = jnp.sum(jnp.where(first, sq, 0.0), axis=-1, keepdims=True)
                s1 = jnp.sum(jnp.where(first, 0.0, sq), axis=-1, keepdims=True)
                ms2 = jnp.where(first, s0, s1) * (1.0 / ATT_HD)
                g = gq_ref[...] if grp == 0 else gk_ref[...]
                y = a * lax.rsqrt(ms2 + NORM_EPS) * g
                partner = jnp.where(low_half, pltpu.roll(y, LANES - ATT_HD // 2, 1),
                                    pltpu.roll(y, ATT_HD // 2, 1))
                r = y * ca + partner * sa
                if grp == 0:
                    r = r * (ATT_HD ** -0.5)
            elif grp in (3, 4):
                r = a * cr + pltpu.roll(a, RET_HD // 2, 1) * sr
                if grp == 4:
                    r = r * (RET_HD ** -0.5)
            else:
                r = a
            slab = grp * SLABS_PER_GROUP + p
            if slab < N_ATT_SLABS:
                oa_ref[slab] = r
            else:
                or_ref[slab - N_ATT_SLABS] = r.astype(BF16)


def _rope_tables(S):
    pos = jnp.arange(S, dtype=F32)

    def table(hd):
        inv_freq = ROPE_THETA ** (-jnp.arange(0, hd, 2, dtype=F32) / hd)
        ang = pos[:, None] * inv_freq[None, :]
        cos, sin = jnp.cos(ang), jnp.sin(ang)
        reps = LANES // hd
        cos_t = jnp.tile(jnp.concatenate([cos, cos], axis=-1), (1, reps))
        sin_t = jnp.tile(jnp.concatenate([-sin, sin], axis=-1), (1, reps))
        return cos_t, sin_t

    ca, sa = table(ATT_HD)
    cr, sr = table(RET_HD)
    return ca, sa, cr, sr


def _inproj(x2d, S, g1, w_in_bf, gq, gk, tables, tm):
    T = x2d.shape[0]
    n_pos_blk = S // tm
    tab_spec = pl.BlockSpec((tm, LANES), lambda i: (i % n_pos_blk, 0))
    full = lambda shape: pl.BlockSpec(shape, lambda i: (0,) * len(shape))
    return pl.pallas_call(
        _inproj_kernel,
        grid=(T // tm,),
        in_specs=[pl.BlockSpec((tm, D_MODEL), lambda i: (i, 0)), full((1, D_MODEL)),
                  full((D_MODEL, IN_W)), full((1, LANES)), full((1, LANES)),
                  tab_spec, tab_spec, tab_spec, tab_spec],
        out_specs=[pl.BlockSpec((N_ATT_SLABS, tm, LANES), lambda i: (0, i, 0)),
                   pl.BlockSpec((N_SLABS - N_ATT_SLABS, tm, LANES), lambda i: (0, i, 0))],
        out_shape=[jax.ShapeDtypeStruct((N_ATT_SLABS, T, LANES), F32),
                   jax.ShapeDtypeStruct((N_SLABS - N_ATT_SLABS, T, LANES), BF16)],
        compiler_params=_cparams(("parallel",)),
        name="inproj",
    )(x2d, g1, w_in_bf, gq, gk, *tables)


DILATED_PATTERNS = ((128, 1), (512, 4), (2048, 16))
ATT_HALF = 64
ATT_SB = 2048
ATT_NB = ATT_SB // ATT_QBLK
ATT_UNROLL = 2


def _attn_kernel(q_ref, k_ref, v_ref, o_ref, o_scr, lse_scr, bias_scr, *, S):
    t0 = pl.program_id(2) * ATT_SB
    lane = lax.broadcasted_iota(jnp.int32, (ATT_QBLK, LANES), 1)
    first = lane < ATT_HD
    nt_dims = (((1,), (1,)), ((), ()))

    rel = (lax.broadcasted_iota(jnp.int32, (ATT_QBLK, 2 * ATT_QBLK), 0)
           - lax.broadcasted_iota(jnp.int32, (ATT_QBLK, 2 * ATT_QBLK), 1))
    for oi in range(3):
        bias_scr[oi] = jnp.where(jnp.abs(rel + oi * ATT_HALF) <= ATT_HALF, 0.0, NEG_INF)

    def band(qv, kv, vv, off):
        kw = kv.shape[0]
        bias = bias_scr[off // ATT_HALF][:, :kw]
        zero = jnp.zeros_like(qv)
        q2 = jnp.concatenate([jnp.where(first, qv, zero), jnp.where(first, zero, qv)], axis=0)
        s = lax.dot_general(q2, kv, nt_dims, preferred_element_type=F32) + jnp.concatenate([bias, bias], axis=0)
        m = jnp.max(s, axis=-1, keepdims=True)
        p = jnp.exp(s - m)
        l = jnp.sum(p, axis=-1, keepdims=True)
        acc = jnp.dot(p.astype(BF16), vv, preferred_element_type=F32) * (1.0 / l)
        lse = m + jnp.log(l)
        return (jnp.where(first, acc[:ATT_QBLK], acc[ATT_QBLK:]),
                jnp.where(first, lse[:ATT_QBLK], lse[ATT_QBLK:]))

    def step(n, carry):
        for pi, (window, dil) in enumerate(DILATED_PATTERNS):
            L = S // dil
            kw = min(2 * ATT_QBLK, L)
            per_class = ATT_SB // dil // ATT_QBLK
            r = n // per_class
            bi = n % per_class
            lq0 = t0 // dil + bi * ATT_QBLK
            lk0 = jnp.clip(lq0 - ATT_HALF, 0, L - kw)
            qrow = dil * ATT_QBLK * bi + r
            krow = dil * lk0 + r
            if dil == 1:
                qs = pl.ds(pl.multiple_of(qrow, ATT_QBLK), ATT_QBLK)
                ks = pl.ds(pl.multiple_of(krow, 8), kw)
            else:
                qs = pl.ds(qrow, ATT_QBLK, stride=dil)
                ks = pl.ds(krow, kw, stride=dil)
            o, lse = band(q_ref[qs, :].astype(BF16), k_ref[ks, :].astype(BF16), v_ref[ks, :].astype(BF16),
                          lq0 - lk0)
            o_scr[pi, qs, :] = o
            lse_scr[pi, qs, :] = lse
        return carry

    lax.fori_loop(0, ATT_NB, step, 0, unroll=ATT_UNROLL)

    def merge(c, carry):
        sl = pl.ds(pl.multiple_of(c * ATT_QBLK, ATT_QBLK), ATT_QBLK)
        lses = [lse_scr[pi, sl, :] for pi in range(3)]
        mx = jnp.maximum(jnp.maximum(lses[0], lses[1]), lses[2])
        ws = [jnp.exp(x - mx) for x in lses]
        num = ws[0] * o_scr[0, sl, :] + ws[1] * o_scr[1, sl, :] + ws[2] * o_scr[2, sl, :]
        o_ref[sl, :] = (num / (ws[0] + ws[1] + ws[2])).astype(o_ref.dtype)
        return carry

    lax.fori_loop(0, ATT_NB, merge, 0)


def _attention(qkv4, B, S):
    n_pairs = ATT_W // LANES
    kv_spec = lambda off: pl.BlockSpec((None, None, S, LANES), lambda b, p, i: (off + p, b, 0, 0))
    return pl.pallas_call(
        functools.partial(_attn_kernel, S=S),
        grid=(B, n_pairs, S // ATT_SB),
        in_specs=[pl.BlockSpec((None, None, ATT_SB, LANES), lambda b, p, i: (p, b, i, 0)),
                  kv_spec(n_pairs), kv_spec(2 * n_pairs)],
        out_specs=pl.BlockSpec((None, ATT_SB, LANES), lambda b, p, i: (b, i, p)),
        out_shape=jax.ShapeDtypeStruct((B, S, ATT_W), BF16),
        scratch_shapes=[pltpu.VMEM((3, ATT_SB, LANES), F32), pltpu.VMEM((3, ATT_SB, LANES), F32),
                        pltpu.VMEM((3, ATT_QBLK, 2 * ATT_QBLK), F32)],
        compiler_params=_cparams(("parallel", "parallel", "arbitrary")),
        name="dilated_attn",
    )(qkv4, qkv4, qkv4)


def _ret_kernel(dec_ref, q_ref, k_ref, v_ref, g_ref, gn_ref, o_ref, of_scr, ob_scr, *, S):
    C = RET_CHUNK
    nc = S // C
    h = pl.program_id(1)
    nt_dims = (((1,), (1,)), ((), ()))
    tn_dims = (((0,), (0,)), ((), ()))

    def consts(direction):
        def log_g(shape):
            return -jnp.exp(jnp.full(shape, dec_ref[direction, h], F32))

        row = lax.broadcasted_iota(jnp.int32, (C, C), 0).astype(F32)
        col = lax.broadcasted_iota(jnp.int32, (C, C), 1).astype(F32)
        n = lax.broadcasted_iota(jnp.int32, (C, RET_HD), 0).astype(F32)
        lg = log_g((C, RET_HD))
        if direction == 0:
            diff = row - col
            mask = diff >= 0.0
            k_dec, q_dec = jnp.exp(lg * (C - 1.0 - n)), jnp.exp(lg * (n + 1.0))
        else:
            diff = col - row
            mask = diff > 0.0
            k_dec, q_dec = jnp.exp(lg * n), jnp.exp(lg * (C - n))
        dmat = jnp.where(mask, jnp.exp(log_g((C, C)) * jnp.maximum(diff, 0.0)), 0.0)
        return dmat, k_dec, q_dec, jnp.exp(log_g((RET_HD, RET_HD)) * C)

    def chunk_out(c, state, dmat, k_dec, q_dec, g_chunk):
        sl = pl.ds(pl.multiple_of(c * C, C), C)
        qc, kc, vc = q_ref[sl, :], k_ref[sl, :], v_ref[sl, :]
        qk = lax.dot_general(qc, kc, nt_dims, preferred_element_type=F32)
        inner = (qk * dmat).astype(BF16)
        o = jnp.dot(inner, vc, preferred_element_type=F32)
        o = o + jnp.dot((qc.astype(F32) * q_dec).astype(BF16), state.astype(BF16),
                        preferred_element_type=F32)
        kd = (kc.astype(F32) * k_dec).astype(BF16)
        kv = lax.dot_general(kd, vc, tn_dims, preferred_element_type=F32)
        return sl, o, state * g_chunk + kv

    cf, cb = consts(0), consts(1)

    def scan(i, states):
        sl_f, o_f, st_f = chunk_out(i, states[0], *cf)
        of_scr[sl_f, :] = o_f
        sl_b, o_b, st_b = chunk_out(nc - 1 - i, states[1], *cb)
        ob_scr[sl_b, :] = o_b
        return st_f, st_b

    zero = jnp.zeros((RET_HD, RET_HD), F32)
    lax.fori_loop(0, nc, scan, (zero, zero), unroll=2)

    gn = gn_ref[...]

    def finish(c, carry):
        sl = pl.ds(pl.multiple_of(c * C, C), C)
        o = of_scr[sl, :] + ob_scr[sl, :]
        mu = jnp.mean(o, axis=-1, keepdims=True)
        var = jnp.mean(jnp.square(o - mu), axis=-1, keepdims=True)
        y = (o - mu) * lax.rsqrt(var + NORM_EPS) * gn
        g = g_ref[sl, :].astype(F32)
        o_ref[sl, :] = (y * (g * (1.0 / (1.0 + jnp.exp(-g))))).astype(o_ref.dtype)
        return carry

    lax.fori_loop(0, nc, finish, 0, unroll=2)


def _retention(qkv4, decays, gn, B, S):
    spec = lambda off: pl.BlockSpec((None, None, S, LANES), lambda b, h: (off + h, b, 0, 0))
    return pl.pallas_call(
        functools.partial(_ret_kernel, S=S),
        grid=(B, RET_HEADS),
        in_specs=[pl.BlockSpec(memory_space=pltpu.SMEM),
                  spec(0), spec(RET_HEADS), spec(2 * RET_HEADS), spec(3 * RET_HEADS),
                  pl.BlockSpec((1, LANES), lambda b, h: (0, h))],
        out_specs=pl.BlockSpec((None, S, LANES), lambda b, h: (b, 0, h)),
        out_shape=jax.ShapeDtypeStruct((B, S, RET_W), BF16),
        scratch_shapes=[pltpu.VMEM((S, LANES), F32), pltpu.VMEM((S, LANES), F32)],
        compiler_params=_cparams(("parallel", "arbitrary")),
        name="retention",
    )(decays, qkv4, qkv4, qkv4, qkv4, gn)


HI16 = -65536


def _pack_bf16_pairs(x):
    n = x.shape[1] // 2
    bits = pltpu.bitcast(x.astype(BF16).astype(F32), jnp.int32)
    return (bits[:, n:] & HI16) | lax.shift_right_logical(bits[:, :n], 16)


def _unpack_bf16_pairs(w):
    lo = pltpu.bitcast(lax.shift_left(w, 16), F32).astype(BF16)
    hi = pltpu.bitcast(w & HI16, F32).astype(BF16)
    return jnp.concatenate([lo, hi], axis=1)


def _outproj_kernel(x_ref, att_ref, ret_ref, wo_ref, g2_ref, wr_ref, x1_ref, h2_ref, aff_ref):
    y = jnp.dot(att_ref[...], wo_ref[:ATT_W, :], preferred_element_type=F32)
    y = y + jnp.dot(ret_ref[...], wo_ref[ATT_W:, :], preferred_element_type=F32)
    x1 = x_ref[...] + y
    x1_ref[...] = x1
    ms = jnp.mean(x1 * x1, axis=-1, keepdims=True)
    h2 = x1 * lax.rsqrt(ms + NORM_EPS) * g2_ref[...]
    h2_ref[...] = _pack_bf16_pairs(h2)
    logits = lax.dot_general(wr_ref[...], h2.astype(BF16), (((1,), (1,)), ((), ())),
                             preferred_element_type=F32)
    e = jnp.exp(logits - jnp.max(logits, axis=0, keepdims=True))
    aff_ref[...] = e / jnp.sum(e, axis=0, keepdims=True)


def _outproj(x2d, att, ret, wo_bf, g2, wr_t_bf, tm):
    T = x2d.shape[0]
    full = lambda shape: pl.BlockSpec(shape, lambda i: (0,) * len(shape))
    return pl.pallas_call(
        _outproj_kernel,
        grid=(T // tm,),
        in_specs=[pl.BlockSpec((tm, D_MODEL), lambda i: (i, 0)),
                  pl.BlockSpec((tm, ATT_W), lambda i: (i, 0)),
                  pl.BlockSpec((tm, RET_W), lambda i: (i, 0)),
                  full((ATT_W + RET_W, D_MODEL)), full((1, D_MODEL)), full((N_EXPERTS, D_MODEL))],
        out_specs=[pl.BlockSpec((tm, D_MODEL), lambda i: (i, 0)),
                   pl.BlockSpec((tm, D_MODEL // 2), lambda i: (i, 0)),
                   pl.BlockSpec((N_EXPERTS, tm), lambda i: (0, i))],
        out_shape=[jax.ShapeDtypeStruct((T, D_MODEL), F32),
                   jax.ShapeDtypeStruct((T, D_MODEL // 2), jnp.int32),
                   jax.ShapeDtypeStruct((N_EXPERTS, T), F32)],
        compiler_params=_cparams(("parallel",)),
        name="outproj_router",
    )(x2d, att, ret, wo_bf, g2, wr_t_bf)


def _select_kernel(aff_ref, sel_ref, *, cap):
    aff = aff_ref[...]
    E, T = aff.shape
    capf = jnp.float32(cap)

    def count(mask):
        return jnp.sum(jnp.where(mask, 1.0, 0.0), axis=1, keepdims=True)

    def thr_step(i, thr_bits):
        cand = thr_bits | jnp.left_shift(jnp.int32(1), 30 - i)
        return jnp.where(count(aff >= pltpu.bitcast(cand, F32)) >= capf, cand, thr_bits)

    thr = pltpu.bitcast(lax.fori_loop(0, 31, thr_step, jnp.zeros((E, 1), jnp.int32)), F32)
    above = aff > thr
    ties = aff == thr
    need = capf - count(above)
    idx = lax.broadcasted_iota(jnp.int32, (E, T), 1)
    n_idx_bits = max(1, (T - 1).bit_length())

    def cut_step(i, cut):
        cand = cut | jnp.left_shift(jnp.int32(1), n_idx_bits - 1 - i)
        return jnp.where(count(ties & (idx < cand)) < need, cand, cut)

    cut = lax.fori_loop(0, n_idx_bits, cut_step, jnp.zeros((E, 1), jnp.int32))
    sel_ref[...] = jnp.where(above | (ties & (idx <= cut)), 1, 0).astype(jnp.int32)


def _select(aff_t, cap):
    E, T = aff_t.shape
    return pl.pallas_call(
        functools.partial(_select_kernel, cap=cap),
        grid=(1,),
        in_specs=[pl.BlockSpec((E, T), lambda i: (0, 0))],
        out_specs=pl.BlockSpec((E, T), lambda i: (0, 0)),
        out_shape=jax.ShapeDtypeStruct((E, T), jnp.int32),
        compiler_params=_cparams(("arbitrary",)),
        name="expert_select",
    )(aff_t)


CMB_TOK = 512


def _spread_slots(pos, bases, width):
    n_e, n_col = len(bases), pos.shape[1]
    col = lax.broadcasted_iota(jnp.int32, pos.shape, 1)
    base_arr = jnp.zeros(pos.shape, jnp.int32)
    for e in range(n_e):
        base_arr = jnp.where(col == e, bases[e], base_arr)
    rel = pos - base_arr
    rel = jnp.where((rel >= 0) & (rel < width), rel, -1).astype(F32).astype(BF16)
    expander = (lax.broadcasted_iota(jnp.int32, (n_col, n_e * width), 0)
                == lax.broadcasted_iota(jnp.int32, (n_col, n_e * width), 1) // width)
    return jnp.dot(rel, jnp.where(expander, 1.0, 0.0).astype(BF16), preferred_element_type=F32)


def _compact_kernel(excl_ref, cnt_ref, pos_ref, idx_ref, *, cap, nblk):
    b = pl.program_id(0)
    ntiles = cap // LANES
    width = 2 * LANES

    @pl.when(b == 0)
    def _():
        idx_ref[...] = jnp.zeros(idx_ref.shape, F32)

    lane = lax.broadcasted_iota(jnp.int32, (CMB_TOK, width), 1)
    tok = (b * CMB_TOK + lax.broadcasted_iota(jnp.int32, (CMB_TOK, width), 0)).astype(F32)

    def window(e, tile_nominal):
        tile = jnp.minimum(tile_nominal, ntiles - 2)
        pos_col = pos_ref[:, e:e + 1]
        hit = ((pos_col - tile * LANES) == lane) & (pos_col >= tile_nominal * LANES)
        row = jnp.sum(jnp.where(hit, tok, 0.0), axis=0, keepdims=True)
        for j in range(2):
            idx_ref[e, pl.ds(tile + j, 1), :] += row[:, j * LANES:(j + 1) * LANES]

    tiles = [jnp.minimum(excl_ref[e * nblk + b] // LANES, ntiles - 2) for e in range(N_EXPERTS)]
    rel = _spread_slots(pos_ref[...], [t * LANES for t in tiles], width)
    lane_f = lax.broadcasted_iota(jnp.int32, (CMB_TOK, LANES), 1).astype(F32)
    tok_f = (b * CMB_TOK + lax.broadcasted_iota(jnp.int32, (CMB_TOK, LANES), 0)).astype(F32)
    for e in range(N_EXPERTS):
        for j in range(2):
            piece = rel[:, e * width + j * LANES:e * width + (j + 1) * LANES]
            row = jnp.sum(jnp.where(piece == lane_f + float(j * LANES), tok_f, 0.0), axis=0, keepdims=True)
            idx_ref[e, pl.ds(tiles[e] + j, 1), :] += row

    for e in range(N_EXPERTS):
        first = excl_ref[e * nblk + b]
        tile0 = first // LANES
        need_end = first + cnt_ref[e * nblk + b]
        n_extra = jnp.maximum((need_end - tile0 * LANES + width - 1) // width - 1, 0)

        def extra(k, carry):
            window(e, tile0 + 2 * (k + 1))
            return carry

        lax.fori_loop(0, n_extra, extra, 0)


def _compact(pos_tok, excl, cnt, cap):
    T, E = pos_tok.shape[0], N_EXPERTS
    nblk = T // CMB_TOK
    grid_spec = pltpu.PrefetchScalarGridSpec(
        num_scalar_prefetch=2,
        grid=(nblk,),
        in_specs=[pl.BlockSpec((CMB_TOK, LANES), lambda b, *_: (b, 0))],
        out_specs=pl.BlockSpec((E, cap // LANES, LANES), lambda b, *_: (0, 0, 0)),
    )
    idx = pl.pallas_call(
        functools.partial(_compact_kernel, cap=cap, nblk=nblk),
        grid_spec=grid_spec,
        out_shape=jax.ShapeDtypeStruct((E, cap // LANES, LANES), F32),
        compiler_params=_cparams(("arbitrary",)),
        name="expert_lists",
    )(excl, cnt, pos_tok)
    return idx.reshape(E, cap).astype(jnp.int32)


SC_CORES = 2
SC_SUBCORES = 16
SC_ROWS = 64


def _sc_gather(table, idx):
    n, width = idx.shape[0], table.shape[1]
    workers = SC_CORES * SC_SUBCORES
    per_worker = n // workers
    mesh = plsc.VectorSubcoreMesh(core_axis_name="c", subcore_axis_name="s")

    @functools.partial(
        pl.kernel, mesh=mesh,
        out_type=jax.ShapeDtypeStruct((n, width), table.dtype),
        scratch_types=[pltpu.VMEM((SC_ROWS,), jnp.int32), pltpu.VMEM((SC_ROWS, width), table.dtype),
                       pltpu.SemaphoreType.DMA],
    )
    def gather(table_hbm, idx_hbm, out_hbm, idx_v, rows_v, sem):
        base = (lax.axis_index("s") * SC_CORES + lax.axis_index("c")) * per_worker

        @pl.loop(0, per_worker // SC_ROWS)
        def _(i):
            off = pl.multiple_of(base + i * SC_ROWS, SC_ROWS)
            pltpu.sync_copy(idx_hbm.at[pl.ds(off, SC_ROWS)], idx_v)
            pltpu.async_copy(table_hbm.at[idx_v], rows_v, sem).wait()
            pltpu.sync_copy(rows_v, out_hbm.at[pl.ds(off, SC_ROWS)])

    return gather(table, idx)


FF_CHUNK = 512


def _ffn_kernel(xin_ref, wg_ref, wu_ref, wd_ref, gate_ref, y_ref):
    xin = _unpack_bf16_pairs(xin_ref[...])
    acc = jnp.zeros(y_ref.shape, F32)
    for f in range(D_FF // FF_CHUNK):
        fs = slice(f * FF_CHUNK, (f + 1) * FF_CHUNK)
        a = jnp.dot(xin, wg_ref[:, fs], preferred_element_type=F32)
        u = jnp.dot(xin, wu_ref[:, fs], preferred_element_type=F32)
        hmid = (a * (1.0 / (1.0 + jnp.exp(-a))) * u).astype(BF16)
        acc = acc + jnp.dot(hmid, wd_ref[fs, :], preferred_element_type=F32)
    y_ref[...] = (acc * gate_ref[...]).astype(y_ref.dtype)


def _ffn(xin, wg_bf, wu_bf, wd_bf, gates, tm):
    E, cap, _ = xin.shape
    return pl.pallas_call(
        _ffn_kernel,
        grid=(E, cap // tm),
        in_specs=[pl.BlockSpec((None, tm, D_MODEL // 2), lambda e, j: (e, j, 0)),
                  pl.BlockSpec((None, D_MODEL, D_FF), lambda e, j: (e, 0, 0)),
                  pl.BlockSpec((None, D_MODEL, D_FF), lambda e, j: (e, 0, 0)),
                  pl.BlockSpec((None, D_FF, D_MODEL), lambda e, j: (e, 0, 0)),
                  pl.BlockSpec((None, tm, 1), lambda e, j: (e, j, 0))],
        out_specs=pl.BlockSpec((None, tm, D_MODEL), lambda e, j: (e, j, 0)),
        out_shape=jax.ShapeDtypeStruct((E, cap, D_MODEL), BF16),
        compiler_params=_cparams(("parallel", "arbitrary")),
        name="expert_ffn",
    )(xin, wg_bf, wu_bf, wd_bf, gates)


CMB_ROWS = 128
BF16_SUBLANES = 16


def _combine_kernel(excl_ref, cnt_ref, x1_ref, pos_ref, y_hbm, o_ref, ybuf, xbuf, sem, xsem, *, cap, nblk):
    b = pl.program_id(0)
    slot = b % 2
    lane = lax.broadcasted_iota(jnp.int32, (CMB_TOK, CMB_ROWS), 1)

    def first_row(e, blk):
        return (excl_ref[e * nblk + blk] // BF16_SUBLANES) * BF16_SUBLANES

    def fetch(e, row, dst, s):
        row = pl.multiple_of(jnp.minimum(row, cap - CMB_ROWS), BF16_SUBLANES)
        return pltpu.make_async_copy(y_hbm.at[e, pl.ds(row, CMB_ROWS), :], dst, s)

    def start_block(blk, sl):
        for e in range(N_EXPERTS):
            fetch(e, first_row(e, blk), ybuf.at[sl, e], sem.at[sl, e]).start()

    @pl.when(b == 0)
    def _():
        start_block(0, 0)

    @pl.when(b + 1 < nblk)
    def _():
        start_block(b + 1, 1 - slot)

    def onehot(pos_col, nominal):
        hit = ((pos_col - jnp.minimum(nominal, cap - CMB_ROWS)) == lane) & (pos_col >= nominal)
        return jnp.where(hit, 1.0, 0.0).astype(BF16)

    bases = []
    for e in range(N_EXPERTS):
        nominal0 = first_row(e, b)
        fetch(e, nominal0, ybuf.at[slot, e], sem.at[slot, e]).wait()
        bases.append(jnp.minimum(nominal0, cap - CMB_ROWS))
    rel = _spread_slots(pos_ref[...], bases, CMB_ROWS)
    lane_f = lane.astype(F32)
    parts = [jnp.where(rel[:, e * CMB_ROWS:(e + 1) * CMB_ROWS] == lane_f, 1.0, 0.0).astype(BF16)
             for e in range(N_EXPERTS)]
    rows = ybuf[slot].reshape(N_EXPERTS * CMB_ROWS, D_MODEL)
    o_ref[...] = x1_ref[...] + jnp.dot(jnp.concatenate(parts, axis=1), rows, preferred_element_type=F32)

    for e in range(N_EXPERTS):
        nominal0 = first_row(e, b)
        need_end = excl_ref[e * nblk + b] + cnt_ref[e * nblk + b]
        n_extra = jnp.maximum((need_end - nominal0 + CMB_ROWS - 1) // CMB_ROWS - 1, 0)

        def extra(k, carry):
            nominal = nominal0 + (k + 1) * CMB_ROWS
            cp = fetch(e, nominal, xbuf, xsem)
            cp.start()
            cp.wait()
            o_ref[...] += jnp.dot(onehot(pos_ref[:, e:e + 1], nominal), xbuf[...], preferred_element_type=F32)
            return carry

        lax.fori_loop(0, n_extra, extra, 0)


def _combine(x1, pos_tok, y, excl, cnt):
    T = x1.shape[0]
    E, cap, _ = y.shape
    nblk = T // CMB_TOK
    grid_spec = pltpu.PrefetchScalarGridSpec(
        num_scalar_prefetch=2,
        grid=(nblk,),
        in_specs=[pl.BlockSpec((CMB_TOK, D_MODEL), lambda b, *_: (b, 0)),
                  pl.BlockSpec((CMB_TOK, LANES), lambda b, *_: (b, 0)),
                  pl.BlockSpec(memory_space=pl.ANY)],
        out_specs=pl.BlockSpec((CMB_TOK, D_MODEL), lambda b, *_: (b, 0)),
        scratch_shapes=[pltpu.VMEM((2, E, CMB_ROWS, D_MODEL), BF16),
                        pltpu.VMEM((CMB_ROWS, D_MODEL), BF16),
                        pltpu.SemaphoreType.DMA((2, E)),
                        pltpu.SemaphoreType.DMA(())],
    )
    return pl.pallas_call(
        functools.partial(_combine_kernel, cap=cap, nblk=nblk),
        grid_spec=grid_spec,
        out_shape=jax.ShapeDtypeStruct((T, D_MODEL), F32),
        compiler_params=_cparams(("arbitrary",)),
        name="moe_combine",
    )(excl, cnt, x1, pos_tok, y)


def _layer(x, p):
    B, S, _ = x.shape
    T = B * S
    tm = 512
    x2d = x.reshape(T, D_MODEL)
    qkv_att, qkv_ret = _inproj(x2d, S, p["g1"], p["w_in"], p["gq"], p["gk"], _rope_tables(S), tm)
    att = _attention(qkv_att.reshape(N_ATT_SLABS, B, S, LANES), B, S).reshape(T, ATT_W)
    ret = _retention(qkv_ret.reshape(N_SLABS - N_ATT_SLABS, B, S, LANES), p["decays"], p["gn"], B, S).reshape(T, RET_W)
    x1, h2, aff_t = _outproj(x2d, att, ret, p["w_out"], p["g2"], p["w_router_t"], tm)

    cap = CAPACITY_FACTOR * T // N_EXPERTS
    sel = _select(aff_t, cap)

    csum = jnp.cumsum(sel, axis=1)
    pos = jnp.where(sel > 0, csum - 1, -1)
    pos_tok = jnp.concatenate([pos, jnp.full((LANES - N_EXPERTS, T), -1, jnp.int32)], axis=0).T
    ends = csum[:, CMB_TOK - 1::CMB_TOK]
    excl = jnp.concatenate([jnp.zeros((N_EXPERTS, 1), jnp.int32), ends[:, :-1]], axis=1).reshape(-1)
    cnt = ends.reshape(-1) - excl

    idx = _compact(pos_tok, excl, cnt, cap)
    xin = _sc_gather(h2, idx.reshape(-1)).reshape(N_EXPERTS, cap, D_MODEL // 2)
    gates = jnp.take_along_axis(aff_t, idx, axis=1)[..., None]
    y = _ffn(xin, p["w_gate"], p["w_up"], p["w_down"], gates, tm)
    out = _combine(x1, pos_tok, y, excl, cnt)
    return out.reshape(B, S, D_MODEL)


def kernel(x_prompt, x_sample, norm1_g, w_in, attn_qnorm_g, attn_knorm_g, ret_decay_fwd, ret_decay_bwd,
           ret_norm_g, w_out, norm2_g, w_router, w_gate_e, w_up_e, w_down_e):
    y_prompt, y_sample = x_prompt, x_sample
    for l in range(norm1_g.shape[0]):
        p = {
            "g1": norm1_g[l][None, :],
            "w_in": w_in[l].astype(BF16),
            "gq": jnp.tile(attn_qnorm_g[l], LANES // ATT_HD)[None, :],
            "gk": jnp.tile(attn_knorm_g[l], LANES // ATT_HD)[None, :],
            "decays": jnp.stack([ret_decay_fwd[l], ret_decay_bwd[l]]).astype(F32),
            "gn": ret_norm_g[l][None, :].astype(F32),
            "w_out": w_out[l].astype(BF16),
            "g2": norm2_g[l][None, :],
            "w_router_t": w_router[l].T.astype(BF16),
            "w_gate": w_gate_e[l].astype(BF16),
            "w_up": w_up_e[l].astype(BF16),
            "w_down": w_down_e[l].astype(BF16),
        }
        y_prompt = _layer(y_prompt, p)
        y_sample = _layer(y_sample, p)
    return (y_prompt, y_sample)
```

```python
import functools

import jax
import jax.numpy as jnp
import numpy as np
from jax import lax
from jax.experimental import pallas as pl
from jax.experimental.pallas import tpu as pltpu
from jax.experimental.pallas import tpu_sc as plsc

F32 = jnp.float32
BF16 = jnp.bfloat16

D_MODEL = 1024
ATT_HEADS, ATT_HD = 8, 64
RET_HEADS, RET_HD = 4, 128
ATT_W = ATT_HEADS * ATT_HD
RET_W = RET_HEADS * RET_HD
IN_W = 3 * ATT_W + 4 * RET_W
LANES = 128
N_SLABS = IN_W // LANES
GROUP_W = 512
SLABS_PER_GROUP = GROUP_W // LANES
N_ATT_SLABS = 3 * ATT_W // LANES
RET_CHUNK = 256
ROPE_THETA = 10000.0
N_EXPERTS = 16
D_FF = 2048
CAPACITY_FACTOR = 2
NORM_EPS = 1e-6
NEG_INF = -1e30
ATT_QBLK = 128
VMEM_LIMIT = 56 * 1024 * 1024


def _cparams(sem):
    return pltpu.CompilerParams(dimension_semantics=sem, vmem_limit_bytes=VMEM_LIMIT)


def _inproj_kernel(x_ref, g1_ref, w_ref, gq_ref, gk_ref, ca_ref, sa_ref, cr_ref, sr_ref, oa_ref, or_ref):
    x = x_ref[...]
    ms = jnp.mean(x * x, axis=-1, keepdims=True)
    h = (x * lax.rsqrt(ms + NORM_EPS) * g1_ref[...]).astype(BF16)
    tm = x.shape[0]
    lane = lax.broadcasted_iota(jnp.int32, (tm, LANES), 1)
    first = lane < ATT_HD
    low_half = (lane & (ATT_HD // 2)) == 0
    ca, sa, cr, sr = ca_ref[...], sa_ref[...], cr_ref[...], sr_ref[...]
    for grp in range(IN_W // GROUP_W):
        acc = jnp.dot(h, w_ref[:, grp * GROUP_W:(grp + 1) * GROUP_W], preferred_element_type=F32)
        for p in range(SLABS_PER_GROUP):
            a = acc[:, p * LANES:(p + 1) * LANES]
            if grp in (0, 1):
                sq = a * a
                s0 = jnp.sum(jnp.where(first, sq, 0.0), axis=-1, keepdims=True)
                s1 = jnp.sum(jnp.where(first, 0.0, sq), axis=-1, keepdims=True)
                ms2 = jnp.where(first, s0, s1) * (1.0 / ATT_HD)
                g = gq_ref[...] if grp == 0 else gk_ref[...]
                y = a * lax.rsqrt(ms2 + NORM_EPS) * g
                partner = jnp.where(low_half, pltpu.roll(y, LANES - ATT_HD // 2, 1),
                                    pltpu.roll(y, ATT_HD // 2, 1))
                r = y * ca + partner * sa
                if grp == 0:
                    r = r * (ATT_HD ** -0.5)
            elif grp in (3, 4):
                r = a * cr + pltpu.roll(a, RET_HD // 2, 1) * sr
                if grp == 4:
                    r = r * (RET_HD ** -0.5)
            else:
                r = a
            slab = grp * SLABS_PER_GROUP + p
            if slab < N_ATT_SLABS:
                oa_ref[slab] = r
            else:
                or_ref[slab - N_ATT_SLABS] = r.astype(BF16)


def _rope_tables(S):
    pos = jnp.arange(S, dtype=F32)

    def table(hd):
        inv_freq = ROPE_THETA ** (-jnp.arange(0, hd, 2, dtype=F32) / hd)
        ang = pos[:, None] * inv_freq[None, :]
        cos, sin = jnp.cos(ang), jnp.sin(ang)
        reps = LANES // hd
        cos_t = jnp.tile(jnp.concatenate([cos, cos], axis=-1), (1, reps))
        sin_t = jnp.tile(jnp.concatenate([-sin, sin], axis=-1), (1, reps))
        return cos_t, sin_t

    ca, sa = table(ATT_HD)
    cr, sr = table(RET_HD)
    return ca, sa, cr, sr


def _inproj(x2d, S, g1, w_in_bf, gq, gk, tables, tm):
    T = x2d.shape[0]
    n_pos_blk = S // tm
    tab_spec = pl.BlockSpec((tm, LANES), lambda i: (i % n_pos_blk, 0))
    full = lambda shape: pl.BlockSpec(shape, lambda i: (0,) * len(shape))
    return pl.pallas_call(
        _inproj_kernel,
        grid=(T // tm,),
        in_specs=[pl.BlockSpec((tm, D_MODEL), lambda i: (i, 0)), full((1, D_MODEL)),
                  full((D_MODEL, IN_W)), full((1, LANES)), full((1, LANES)),
                  tab_spec, tab_spec, tab_spec, tab_spec],
        out_specs=[pl.BlockSpec((N_ATT_SLABS, tm, LANES), lambda i: (0, i, 0)),
                   pl.BlockSpec((N_SLABS - N_ATT_SLABS, tm, LANES), lambda i: (0, i, 0))],
        out_shape=[jax.ShapeDtypeStruct((N_ATT_SLABS, T, LANES), F32),
                   jax.ShapeDtypeStruct((N_SLABS - N_ATT_SLABS, T, LANES), BF16)],
        compiler_params=_cparams(("parallel",)),
        name="inproj",
    )(x2d, g1, w_in_bf, gq, gk, *tables)


DILATED_PATTERNS = ((128, 1), (512, 4), (2048, 16))
ATT_HALF = 64
ATT_SB = 2048
ATT_NB = ATT_SB // ATT_QBLK
ATT_UNROLL = 2


def _attn_kernel(q_ref, k_ref, v_ref, o_ref, o_scr, lse_scr, bias_scr, *, S):
    t0 = pl.program_id(2) * ATT_SB
    lane = lax.broadcasted_iota(jnp.int32, (ATT_QBLK, LANES), 1)
    first = lane < ATT_HD
    nt_dims = (((1,), (1,)), ((), ()))

    rel = (lax.broadcasted_iota(jnp.int32, (ATT_QBLK, 2 * ATT_QBLK), 0)
           - lax.broadcasted_iota(jnp.int32, (ATT_QBLK, 2 * ATT_QBLK), 1))
    for oi in range(3):
        bias_scr[oi] = jnp.where(jnp.abs(rel + oi * ATT_HALF) <= ATT_HALF, 0.0, NEG_INF)

    def band(qv, kv, vv, off):
        kw = kv.shape[0]
        bias = bias_scr[off // ATT_HALF][:, :kw]
        zero = jnp.zeros_like(qv)
        q2 = jnp.concatenate([jnp.where(first, qv, zero), jnp.where(first, zero, qv)], axis=0)
        s = lax.dot_general(q2, kv, nt_dims, preferred_element_type=F32) + jnp.concatenate([bias, bias], axis=0)
        m = jnp.max(s, axis=-1, keepdims=True)
        p = jnp.exp(s - m)
        l = jnp.sum(p, axis=-1, keepdims=True)
        acc = jnp.dot(p.astype(BF16), vv, preferred_element_type=F32) * (1.0 / l)
        lse = m + jnp.log(l)
        return (jnp.where(first, acc[:ATT_QBLK], acc[ATT_QBLK:]),
                jnp.where(first, lse[:ATT_QBLK], lse[ATT_QBLK:]))

    def step(n, carry):
        for pi, (window, dil) in enumerate(DILATED_PATTERNS):
            L = S // dil
            kw = min(2 * ATT_QBLK, L)
            per_class = ATT_SB // dil // ATT_QBLK
            r = n // per_class
            bi = n % per_class
            lq0 = t0 // dil + bi * ATT_QBLK
            lk0 = jnp.clip(lq0 - ATT_HALF, 0, L - kw)
            qrow = dil * ATT_QBLK * bi + r
            krow = dil * lk0 + r
            if dil == 1:
                qs = pl.ds(pl.multiple_of(qrow, ATT_QBLK), ATT_QBLK)
                ks = pl.ds(pl.multiple_of(krow, 8), kw)
            else:
                qs = pl.ds(qrow, ATT_QBLK, stride=dil)
                ks = pl.ds(krow, kw, stride=dil)
            o, lse = band(q_ref[qs, :].astype(BF16), k_ref[ks, :].astype(BF16), v_ref[ks, :].astype(BF16),
                          lq0 - lk0)
            o_scr[pi, qs, :] = o
            lse_scr[pi, qs, :] = lse
        return carry

    lax.fori_loop(0, ATT_NB, step, 0, unroll=ATT_UNROLL)

    def merge(c, carry):
        sl = pl.ds(pl.multiple_of(c * ATT_QBLK, ATT_QBLK), ATT_QBLK)
        lses = [lse_scr[pi, sl, :] for pi in range(3)]
        mx = jnp.maximum(jnp.maximum(lses[0], lses[1]), lses[2])
        ws = [jnp.exp(x - mx) for x in lses]
        num = ws[0] * o_scr[0, sl, :] + ws[1] * o_scr[1, sl, :] + ws[2] * o_scr[2, sl, :]
        o_ref[sl, :] = (num / (ws[0] + ws[1] + ws[2])).astype(o_ref.dtype)
        return carry

    lax.fori_loop(0, ATT_NB, merge, 0)


def _attention(qkv4, B, S):
    n_pairs = ATT_W // LANES
    kv_spec = lambda off: pl.BlockSpec((None, None, S, LANES), lambda b, p, i: (off + p, b, 0, 0))
    return pl.pallas_call(
        functools.partial(_attn_kernel, S=S),
        grid=(B, n_pairs, S // ATT_SB),
        in_specs=[pl.BlockSpec((None, None, ATT_SB, LANES), lambda b, p, i: (p, b, i, 0)),
                  kv_spec(n_pairs), kv_spec(2 * n_pairs)],
        out_specs=pl.BlockSpec((None, ATT_SB, LANES), lambda b, p, i: (b, i, p)),
        out_shape=jax.ShapeDtypeStruct((B, S, ATT_W), BF16),
        scratch_shapes=[pltpu.VMEM((3, ATT_SB, LANES), F32), pltpu.VMEM((3, ATT_SB, LANES), F32),
                        pltpu.VMEM((3, ATT_QBLK, 2 * ATT_QBLK), F32)],
        compiler_params=_cparams(("parallel", "parallel", "arbitrary")),
        name="dilated_attn",
    )(qkv4, qkv4, qkv4)


def _ret_kernel(dec_ref, q_ref, k_ref, v_ref, g_ref, gn_ref, o_ref, of_scr, ob_scr, *, S):
    C = RET_CHUNK
    nc = S // C
    h = pl.program_id(1)
    nt_dims = (((1,), (1,)), ((), ()))
    tn_dims = (((0,), (0,)), ((), ()))

    def consts(direction):
        def log_g(shape):
            return -jnp.exp(jnp.full(shape, dec_ref[direction, h], F32))

        row = lax.broadcasted_iota(jnp.int32, (C, C), 0).astype(F32)
        col = lax.broadcasted_iota(jnp.int32, (C, C), 1).astype(F32)
        n = lax.broadcasted_iota(jnp.int32, (C, RET_HD), 0).astype(F32)
        lg = log_g((C, RET_HD))
        if direction == 0:
            diff = row - col
            mask = diff >= 0.0
            k_dec, q_dec = jnp.exp(lg * (C - 1.0 - n)), jnp.exp(lg * (n + 1.0))
        else:
            diff = col - row
            mask = diff > 0.0
            k_dec, q_dec = jnp.exp(lg * n), jnp.exp(lg * (C - n))
        dmat = jnp.where(mask, jnp.exp(log_g((C, C)) * jnp.maximum(diff, 0.0)), 0.0)
        return dmat, k_dec, q_dec, jnp.exp(log_g((RET_HD, RET_HD)) * C)

    def chunk_out(c, state, dmat, k_dec, q_dec, g_chunk):
        sl = pl.ds(pl.multiple_of(c * C, C), C)
        qc, kc, vc = q_ref[sl, :], k_ref[sl, :], v_ref[sl, :]
        qk = lax.dot_general(qc, kc, nt_dims, preferred_element_type=F32)
        inner = (qk * dmat).astype(BF16)
        o = jnp.dot(inner, vc, preferred_element_type=F32)
        o = o + jnp.dot((qc.astype(F32) * q_dec).astype(BF16), state.astype(BF16),
                        preferred_element_type=F32)
        kd = (kc.astype(F32) * k_dec).astype(BF16)
        kv = lax.dot_general(kd, vc, tn_dims, preferred_element_type=F32)
        return sl, o, state * g_chunk + kv

    cf, cb = consts(0), consts(1)

    def scan(i, states):
        sl_f, o_f, st_f = chunk_out(i, states[0], *cf)
        of_scr[sl_f, :] = o_f
        sl_b, o_b, st_b = chunk_out(nc - 1 - i, states[1], *cb)
        ob_scr[sl_b, :] = o_b
        return st_f, st_b

    zero = jnp.zeros((RET_HD, RET_HD), F32)
    lax.fori_loop(0, nc, scan, (zero, zero), unroll=2)

    gn = gn_ref[...]

    def finish(c, carry):
        sl = pl.ds(pl.multiple_of(c * C, C), C)
        o = of_scr[sl, :] + ob_scr[sl, :]
        mu = jnp.mean(o, axis=-1, keepdims=True)
        var = jnp.mean(jnp.square(o - mu), axis=-1, keepdims=True)
        y = (o - mu) * lax.rsqrt(var + NORM_EPS) * gn
        g = g_ref[sl, :].astype(F32)
        o_ref[sl, :] = (y * (g * (1.0 / (1.0 + jnp.exp(-g))))).astype(o_ref.dtype)
        return carry

    lax.fori_loop(0, nc, finish, 0, unroll=2)


def _retention(qkv4, decays, gn, B, S):
    spec = lambda off: pl.BlockSpec((None, None, S, LANES), lambda b, h: (off + h, b, 0, 0))
    return pl.pallas_call(
        functools.partial(_ret_kernel, S=S),
        grid=(B, RET_HEADS),
        in_specs=[pl.BlockSpec(memory_space=pltpu.SMEM),
                  spec(0), spec(RET_HEADS), spec(2 * RET_HEADS), spec(3 * RET_HEADS),
                  pl.BlockSpec((1, LANES), lambda b, h: (0, h))],
        out_specs=pl.BlockSpec((None, S, LANES), lambda b, h: (b, 0, h)),
        out_shape=jax.ShapeDtypeStruct((B, S, RET_W), BF16),
        scratch_shapes=[pltpu.VMEM((S, LANES), F32), pltpu.VMEM((S, LANES), F32)],
        compiler_params=_cparams(("parallel", "arbitrary")),
        name="retention",
    )(decays, qkv4, qkv4, qkv4, qkv4, gn)


HI16 = -65536


def _pack_bf16_pairs(x):
    n = x.shape[1] // 2
    bits = pltpu.bitcast(x.astype(BF16).astype(F32), jnp.int32)
    return (bits[:, n:] & HI16) | lax.shift_right_logical(bits[:, :n], 16)


def _unpack_bf16_pairs(w):
    lo = pltpu.bitcast(lax.shift_left(w, 16), F32).astype(BF16)
    hi = pltpu.bitcast(w & HI16, F32).astype(BF16)
    return jnp.concatenate([lo, hi], axis=1)


def _outproj_kernel(x_ref, att_ref, ret_ref, wo_ref, g2_ref, wr_ref, x1_ref, h2_ref, aff_ref):
    y = jnp.dot(att_ref[...], wo_ref[:ATT_W, :], preferred_element_type=F32)
    y = y + jnp.dot(ret_ref[...], wo_ref[ATT_W:, :], preferred_element_type=F32)
    x1 = x_ref[...] + y
    x1_ref[...] = x1
    ms = jnp.mean(x1 * x1, axis=-1, keepdims=True)
    h2 = x1 * lax.rsqrt(ms + NORM_EPS) * g2_ref[...]
    h2_ref[...] = _pack_bf16_pairs(h2)
    logits = lax.dot_general(wr_ref[...], h2.astype(BF16), (((1,), (1,)), ((), ())),
                             preferred_element_type=F32)
    e = jnp.exp(logits - jnp.max(logits, axis=0, keepdims=True))
    aff_ref[...] = e / jnp.sum(e, axis=0, keepdims=True)


def _outproj(x2d, att, ret, wo_bf, g2, wr_t_bf, tm):
    T = x2d.shape[0]
    full = lambda shape: pl.BlockSpec(shape, lambda i: (0,) * len(shape))
    return pl.pallas_call(
        _outproj_kernel,
        grid=(T // tm,),
        in_specs=[pl.BlockSpec((tm, D_MODEL), lambda i: (i, 0)),
                  pl.BlockSpec((tm, ATT_W), lambda i: (i, 0)),
                  pl.BlockSpec((tm, RET_W), lambda i: (i, 0)),
                  full((ATT_W + RET_W, D_MODEL)), full((1, D_MODEL)), full((N_EXPERTS, D_MODEL))],
        out_specs=[pl.BlockSpec((tm, D_MODEL), lambda i: (i, 0)),
                   pl.BlockSpec((tm, D_MODEL // 2), lambda i: (i, 0)),
                   pl.BlockSpec((N_EXPERTS, tm), lambda i: (0, i))],
        out_shape=[jax.ShapeDtypeStruct((T, D_MODEL), F32),
                   jax.ShapeDtypeStruct((T, D_MODEL // 2), jnp.int32),
                   jax.ShapeDtypeStruct((N_EXPERTS, T), F32)],
        compiler_params=_cparams(("parallel",)),
        name="outproj_router",
    )(x2d, att, ret, wo_bf, g2, wr_t_bf)


def _select_kernel(aff_ref, sel_ref, *, cap):
    aff = aff_ref[...]
    E, T = aff.shape
    capf = jnp.float32(cap)

    def count(mask):
        return jnp.sum(jnp.where(mask, 1.0, 0.0), axis=1, keepdims=True)

    def thr_step(i, thr_bits):
        cand = thr_bits | jnp.left_shift(jnp.int32(1), 30 - i)
        return jnp.where(count(aff >= pltpu.bitcast(cand, F32)) >= capf, cand, thr_bits)

    thr = pltpu.bitcast(lax.fori_loop(0, 31, thr_step, jnp.zeros((E, 1), jnp.int32)), F32)
    above = aff > thr
    ties = aff == thr
    need = capf - count(above)
    idx = lax.broadcasted_iota(jnp.int32, (E, T), 1)
    n_idx_bits = max(1, (T - 1).bit_length())

    def cut_step(i, cut):
        cand = cut | jnp.left_shift(jnp.int32(1), n_idx_bits - 1 - i)
        return jnp.where(count(ties & (idx < cand)) < need, cand, cut)

    cut = lax.fori_loop(0, n_idx_bits, cut_step, jnp.zeros((E, 1), jnp.int32))
    sel_ref[...] = jnp.where(above | (ties & (idx <= cut)), 1, 0).astype(jnp.int32)


def _select(aff_t, cap):
    E, T = aff_t.shape
    return pl.pallas_call(
        functools.partial(_select_kernel, cap=cap),
        grid=(1,),
        in_specs=[pl.BlockSpec((E, T), lambda i: (0, 0))],
        out_specs=pl.BlockSpec((E, T), lambda i: (0, 0)),
        out_shape=jax.ShapeDtypeStruct((E, T), jnp.int32),
        compiler_params=_cparams(("arbitrary",)),
        name="expert_select",
    )(aff_t)


CMB_TOK = 512


def _token_major(slots):
    n_e, n_tok = slots.shape
    padded = jnp.concatenate([slots, jnp.full((LANES - n_e, n_tok), -1, jnp.int32)], axis=0)
    return pltpu.bitcast(jnp.transpose(pltpu.bitcast(padded, F32)), jnp.int32)


def _spread_slots(pos, bases, width):
    n_e, n_col = len(bases), pos.shape[1]
    col = lax.broadcasted_iota(jnp.int32, pos.shape, 1)
    base_arr = jnp.zeros(pos.shape, jnp.int32)
    for e in range(n_e):
        base_arr = jnp.where(col == e, bases[e], base_arr)
    rel = pos - base_arr
    rel = jnp.where((rel >= 0) & (rel < width), rel, -1).astype(F32).astype(BF16)
    expander = (lax.broadcasted_iota(jnp.int32, (n_col, n_e * width), 0)
                == lax.broadcasted_iota(jnp.int32, (n_col, n_e * width), 1) // width)
    return jnp.dot(rel, jnp.where(expander, 1.0, 0.0).astype(BF16), preferred_element_type=F32)


def _compact_kernel(excl_ref, cnt_ref, slots_ref, idx_ref, pos_ref, *, cap, nblk):
    b = pl.program_id(0)
    ntiles = cap // LANES
    width = 2 * LANES

    @pl.when(b == 0)
    def _():
        idx_ref[...] = jnp.zeros(idx_ref.shape, F32)

    pos_ref[...] = _token_major(slots_ref[...])

    lane = lax.broadcasted_iota(jnp.int32, (CMB_TOK, width), 1)
    tok = (b * CMB_TOK + lax.broadcasted_iota(jnp.int32, (CMB_TOK, width), 0)).astype(F32)

    def window(e, tile_nominal):
        tile = jnp.minimum(tile_nominal, ntiles - 2)
        pos_col = pos_ref[:, e:e + 1]
        hit = ((pos_col - tile * LANES) == lane) & (pos_col >= tile_nominal * LANES)
        row = jnp.sum(jnp.where(hit, tok, 0.0), axis=0, keepdims=True)
        for j in range(2):
            idx_ref[e, pl.ds(tile + j, 1), :] += row[:, j * LANES:(j + 1) * LANES]

    tiles = [jnp.minimum(excl_ref[e * nblk + b] // LANES, ntiles - 2) for e in range(N_EXPERTS)]
    rel = _spread_slots(pos_ref[...], [t * LANES for t in tiles], width)
    lane_f = lax.broadcasted_iota(jnp.int32, (CMB_TOK, LANES), 1).astype(F32)
    tok_f = (b * CMB_TOK + lax.broadcasted_iota(jnp.int32, (CMB_TOK, LANES), 0)).astype(F32)
    for e in range(N_EXPERTS):
        for j in range(2):
            piece = rel[:, e * width + j * LANES:e * width + (j + 1) * LANES]
            row = jnp.sum(jnp.where(piece == lane_f + float(j * LANES), tok_f, 0.0), axis=0, keepdims=True)
            idx_ref[e, pl.ds(tiles[e] + j, 1), :] += row

    for e in range(N_EXPERTS):
        first = excl_ref[e * nblk + b]
        tile0 = first // LANES
        need_end = first + cnt_ref[e * nblk + b]
        n_extra = jnp.maximum((need_end - tile0 * LANES + width - 1) // width - 1, 0)

        def extra(k, carry):
            window(e, tile0 + 2 * (k + 1))
            return carry

        lax.fori_loop(0, n_extra, extra, 0)


def _compact(slots, excl, cnt, cap):
    E, T = slots.shape
    nblk = T // CMB_TOK
    grid_spec = pltpu.PrefetchScalarGridSpec(
        num_scalar_prefetch=2,
        grid=(nblk,),
        in_specs=[pl.BlockSpec((E, CMB_TOK), lambda b, *_: (0, b))],
        out_specs=pl.BlockSpec((E, cap // LANES, LANES), lambda b, *_: (0, 0, 0)),
        scratch_shapes=[pltpu.VMEM((CMB_TOK, LANES), jnp.int32)],
    )
    idx = pl.pallas_call(
        functools.partial(_compact_kernel, cap=cap, nblk=nblk),
        grid_spec=grid_spec,
        out_shape=jax.ShapeDtypeStruct((E, cap // LANES, LANES), F32),
        compiler_params=_cparams(("arbitrary",)),
        name="expert_lists",
    )(excl, cnt, slots)
    return idx.reshape(E, cap).astype(jnp.int32)


SC_CORES = 2
SC_SUBCORES = 16
SC_ROWS = 64


def _sc_gather(table, idx):
    n, width = idx.shape[0], table.shape[1]
    workers = SC_CORES * SC_SUBCORES
    per_worker = n // workers
    mesh = plsc.VectorSubcoreMesh(core_axis_name="c", subcore_axis_name="s")

    @functools.partial(
        pl.kernel, mesh=mesh,
        out_type=jax.ShapeDtypeStruct((n, width), table.dtype),
        scratch_types=[pltpu.VMEM((SC_ROWS,), jnp.int32), pltpu.VMEM((SC_ROWS, width), table.dtype),
                       pltpu.SemaphoreType.DMA],
    )
    def gather(table_hbm, idx_hbm, out_hbm, idx_v, rows_v, sem):
        base = (lax.axis_index("s") * SC_CORES + lax.axis_index("c")) * per_worker

        @pl.loop(0, per_worker // SC_ROWS)
        def _(i):
            off = pl.multiple_of(base + i * SC_ROWS, SC_ROWS)
            pltpu.sync_copy(idx_hbm.at[pl.ds(off, SC_ROWS)], idx_v)
            pltpu.async_copy(table_hbm.at[idx_v], rows_v, sem).wait()
            pltpu.sync_copy(rows_v, out_hbm.at[pl.ds(off, SC_ROWS)])

    return gather(table, idx)


FF_CHUNK = 512


def _ffn_kernel(xin_ref, wg_ref, wu_ref, wd_ref, gate_ref, y_ref):
    xin = _unpack_bf16_pairs(xin_ref[...])
    acc = jnp.zeros(y_ref.shape, F32)
    for f in range(D_FF // FF_CHUNK):
        fs = slice(f * FF_CHUNK, (f + 1) * FF_CHUNK)
        a = jnp.dot(xin, wg_ref[:, fs], preferred_element_type=F32)
        u = jnp.dot(xin, wu_ref[:, fs], preferred_element_type=F32)
        hmid = (a * (1.0 / (1.0 + jnp.exp(-a))) * u).astype(BF16)
        acc = acc + jnp.dot(hmid, wd_ref[fs, :], preferred_element_type=F32)
    y_ref[...] = (acc * gate_ref[...]).astype(y_ref.dtype)


def _ffn(xin, wg_bf, wu_bf, wd_bf, gates, tm):
    E, cap, _ = xin.shape
    return pl.pallas_call(
        _ffn_kernel,
        grid=(E, cap // tm),
        in_specs=[pl.BlockSpec((None, tm, D_MODEL // 2), lambda e, j: (e, j, 0)),
                  pl.BlockSpec((None, D_MODEL, D_FF), lambda e, j: (e, 0, 0)),
                  pl.BlockSpec((None, D_MODEL, D_FF), lambda e, j: (e, 0, 0)),
                  pl.BlockSpec((None, D_FF, D_MODEL), lambda e, j: (e, 0, 0)),
                  pl.BlockSpec((None, tm, 1), lambda e, j: (e, j, 0))],
        out_specs=pl.BlockSpec((None, tm, D_MODEL), lambda e, j: (e, j, 0)),
        out_shape=jax.ShapeDtypeStruct((E, cap, D_MODEL), BF16),
        compiler_params=_cparams(("parallel", "arbitrary")),
        name="expert_ffn",
    )(xin, wg_bf, wu_bf, wd_bf, gates)


CMB_ROWS = 128
BF16_SUBLANES = 16


def _combine_kernel(excl_ref, cnt_ref, x1_ref, slots_ref, y_hbm, o_ref, ybuf, xbuf, sem, xsem, pos_ref, *, cap, nblk):
    b = pl.program_id(0)
    slot = b % 2
    lane = lax.broadcasted_iota(jnp.int32, (CMB_TOK, CMB_ROWS), 1)

    def first_row(e, blk):
        return (excl_ref[e * nblk + blk] // BF16_SUBLANES) * BF16_SUBLANES

    def fetch(e, row, dst, s):
        row = pl.multiple_of(jnp.minimum(row, cap - CMB_ROWS), BF16_SUBLANES)
        return pltpu.make_async_copy(y_hbm.at[e, pl.ds(row, CMB_ROWS), :], dst, s)

    def start_block(blk, sl):
        for e in range(N_EXPERTS):
            fetch(e, first_row(e, blk), ybuf.at[sl, e], sem.at[sl, e]).start()

    @pl.when(b == 0)
    def _():
        start_block(0, 0)

    @pl.when(b + 1 < nblk)
    def _():
        start_block(b + 1, 1 - slot)

    def onehot(pos_col, nominal):
        hit = ((pos_col - jnp.minimum(nominal, cap - CMB_ROWS)) == lane) & (pos_col >= nominal)
        return jnp.where(hit, 1.0, 0.0).astype(BF16)

    pos_ref[...] = _token_major(slots_ref[...])
    bases = []
    for e in range(N_EXPERTS):
        nominal0 = first_row(e, b)
        fetch(e, nominal0, ybuf.at[slot, e], sem.at[slot, e]).wait()
        bases.append(jnp.minimum(nominal0, cap - CMB_ROWS))
    rel = _spread_slots(pos_ref[...], bases, CMB_ROWS)
    lane_f = lane.astype(F32)
    parts = [jnp.where(rel[:, e * CMB_ROWS:(e + 1) * CMB_ROWS] == lane_f, 1.0, 0.0).astype(BF16)
             for e in range(N_EXPERTS)]
    rows = ybuf[slot].reshape(N_EXPERTS * CMB_ROWS, D_MODEL)
    o_ref[...] = x1_ref[...] + jnp.dot(jnp.concatenate(parts, axis=1), rows, preferred_element_type=F32)

    for e in range(N_EXPERTS):
        nominal0 = first_row(e, b)
        need_end = excl_ref[e * nblk + b] + cnt_ref[e * nblk + b]
        n_extra = jnp.maximum((need_end - nominal0 + CMB_ROWS - 1) // CMB_ROWS - 1, 0)

        def extra(k, carry):
            nominal = nominal0 + (k + 1) * CMB_ROWS
            cp = fetch(e, nominal, xbuf, xsem)
            cp.start()
            cp.wait()
            o_ref[...] += jnp.dot(onehot(pos_ref[:, e:e + 1], nominal), xbuf[...], preferred_element_type=F32)
            return carry

        lax.fori_loop(0, n_extra, extra, 0)


def _combine(x1, slots, y, excl, cnt):
    T = x1.shape[0]
    E, cap, _ = y.shape
    nblk = T // CMB_TOK
    grid_spec = pltpu.PrefetchScalarGridSpec(
        num_scalar_prefetch=2,
        grid=(nblk,),
        in_specs=[pl.BlockSpec((CMB_TOK, D_MODEL), lambda b, *_: (b, 0)),
                  pl.BlockSpec((E, CMB_TOK), lambda b, *_: (0, b)),
                  pl.BlockSpec(memory_space=pl.ANY)],
        out_specs=pl.BlockSpec((CMB_TOK, D_MODEL), lambda b, *_: (b, 0)),
        scratch_shapes=[pltpu.VMEM((2, E, CMB_ROWS, D_MODEL), BF16),
                        pltpu.VMEM((CMB_ROWS, D_MODEL), BF16),
                        pltpu.SemaphoreType.DMA((2, E)),
                        pltpu.SemaphoreType.DMA(()),
                        pltpu.VMEM((CMB_TOK, LANES), jnp.int32)],
    )
    return pl.pallas_call(
        functools.partial(_combine_kernel, cap=cap, nblk=nblk),
        grid_spec=grid_spec,
        out_shape=jax.ShapeDtypeStruct((T, D_MODEL), F32),
        compiler_params=_cparams(("arbitrary",)),
        name="moe_combine",
    )(excl, cnt, x1, slots, y)


def _layer(x, p):
    B, S, _ = x.shape
    T = B * S
    tm = 512
    x2d = x.reshape(T, D_MODEL)
    qkv_att, qkv_ret = _inproj(x2d, S, p["g1"], p["w_in"], p["gq"], p["gk"], _rope_tables(S), tm)
    att = _attention(qkv_att.reshape(N_ATT_SLABS, B, S, LANES), B, S).reshape(T, ATT_W)
    ret = _retention(qkv_ret.reshape(N_SLABS - N_ATT_SLABS, B, S, LANES), p["decays"], p["gn"], B, S).reshape(T, RET_W)
    x1, h2, aff_t = _outproj(x2d, att, ret, p["w_out"], p["g2"], p["w_router_t"], tm)

    cap = CAPACITY_FACTOR * T // N_EXPERTS
    sel = _select(aff_t, cap)

    csum = jnp.cumsum(sel, axis=1)
    slots = jnp.where(sel > 0, csum - 1, -1)
    ends = csum[:, CMB_TOK - 1::CMB_TOK]
    excl = jnp.concatenate([jnp.zeros((N_EXPERTS, 1), jnp.int32), ends[:, :-1]], axis=1).reshape(-1)
    cnt = ends.reshape(-1) - excl

    idx = _compact(slots, excl, cnt, cap)
    xin = _sc_gather(h2, idx.reshape(-1)).reshape(N_EXPERTS, cap, D_MODEL // 2)
    gates = jnp.take_along_axis(aff_t, idx, axis=1)[..., None]
    y = _ffn(xin, p["w_gate"], p["w_up"], p["w_down"], gates, tm)
    out = _combine(x1, slots, y, excl, cnt)
    return out.reshape(B, S, D_MODEL)


def kernel(x_prompt, x_sample, norm1_g, w_in, attn_qnorm_g, attn_knorm_g, ret_decay_fwd, ret_decay_bwd,
           ret_norm_g, w_out, norm2_g, w_router, w_gate_e, w_up_e, w_down_e):
    y_prompt, y_sample = x_prompt, x_sample
    for l in range(norm1_g.shape[0]):
        p = {
            "g1": norm1_g[l][None, :],
            "w_in": w_in[l].astype(BF16),
            "gq": jnp.tile(attn_qnorm_g[l], LANES // ATT_HD)[None, :],
            "gk": jnp.tile(attn_knorm_g[l], LANES // ATT_HD)[None, :],
            "decays": jnp.stack([ret_decay_fwd[l], ret_decay_bwd[l]]).astype(F32),
            "gn": ret_norm_g[l][None, :].astype(F32),
            "w_out": w_out[l].astype(BF16),
            "g2": norm2_g[l][None, :],
            "w_router_t": w_router[l].T.astype(BF16),
            "w_gate": w_gate_e[l].astype(BF16),
            "w_up": w_up_e[l].astype(BF16),
            "w_down": w_down_e[l].astype(BF16),
        }
        y_prompt = _layer(y_prompt, p)
        y_sample = _layer(y_sample, p)
    return (y_prompt, y_sample)
```

```python
import functools

import jax
import jax.numpy as jnp
import numpy as np
from jax import lax
from jax.experimental import pallas as pl
from jax.experimental.pallas import tpu as pltpu
from jax.experimental.pallas import tpu_sc as plsc

F32 = jnp.float32
BF16 = jnp.bfloat16

D_MODEL = 1024
ATT_HEADS, ATT_HD = 8, 64
RET_HEADS, RET_HD = 4, 128
ATT_W = ATT_HEADS * ATT_HD
RET_W = RET_HEADS * RET_HD
IN_W = 3 * ATT_W + 4 * RET_W
LANES = 128
N_SLABS = IN_W // LANES
GROUP_W = 512
SLABS_PER_GROUP = GROUP_W // LANES
N_ATT_SLABS = 3 * ATT_W // LANES
RET_CHUNK = 256
ROPE_THETA = 10000.0
N_EXPERTS = 16
D_FF = 2048
CAPACITY_FACTOR = 2
NORM_EPS = 1e-6
NEG_INF = -1e30
ATT_QBLK = 128
VMEM_LIMIT = 56 * 1024 * 1024


def _cparams(sem):
    return pltpu.CompilerParams(dimension_semantics=sem, vmem_limit_bytes=VMEM_LIMIT)


def _inproj_kernel(x_ref, g1_ref, w_ref, gq_ref, gk_ref, ca_ref, sa_ref, cr_ref, sr_ref, oa_ref, or_ref):
    x = x_ref[...]
    ms = jnp.mean(x * x, axis=-1, keepdims=True)
    h = (x * lax.rsqrt(ms + NORM_EPS) * g1_ref[...]).astype(BF16)
    tm = x.shape[0]
    lane = lax.broadcasted_iota(jnp.int32, (tm, LANES), 1)
    first = lane < ATT_HD
    low_half = (lane & (ATT_HD // 2)) == 0
    ca, sa, cr, sr = ca_ref[...], sa_ref[...], cr_ref[...], sr_ref[...]
    for grp in range(IN_W // GROUP_W):
        acc = jnp.dot(h, w_ref[:, grp * GROUP_W:(grp + 1) * GROUP_W], preferred_element_type=F32)
        for p in range(SLABS_PER_GROUP):
            a = acc[:, p * LANES:(p + 1) * LANES]
            if grp in (0, 1):
                sq = a * a
                s0 = jnp.sum(jnp.where(first, sq, 0.0), axis=-1, keepdims=True)
                s1 = jnp.sum(jnp.where(first, 0.0, sq), axis=-1, keepdims=True)
                ms2 = jnp.where(first, s0, s1) * (1.0 / ATT_HD)
                g = gq_ref[...] if grp == 0 else gk_ref[...]
                y = a * lax.rsqrt(ms2 + NORM_EPS) * g
                partner = jnp.where(low_half, pltpu.roll(y, LANES - ATT_HD // 2, 1),
                                    pltpu.roll(y, ATT_HD // 2, 1))
                r = y * ca + partner * sa
                if grp == 0:
                    r = r * (ATT_HD ** -0.5)
            elif grp in (3, 4):
                r = a * cr + pltpu.roll(a, RET_HD // 2, 1) * sr
                if grp == 4:
                    r = r * (RET_HD ** -0.5)
            else:
                r = a
            slab = grp * SLABS_PER_GROUP + p
            if slab < N_ATT_SLABS:
                oa_ref[slab] = r
            else:
                or_ref[slab - N_ATT_SLABS] = r.astype(BF16)


def _rope_tables(S):
    pos = jnp.arange(S, dtype=F32)

    def table(hd):
        inv_freq = ROPE_THETA ** (-jnp.arange(0, hd, 2, dtype=F32) / hd)
        ang = pos[:, None] * inv_freq[None, :]
        cos, sin = jnp.cos(ang), jnp.sin(ang)
        reps = LANES // hd
        cos_t = jnp.tile(jnp.concatenate([cos, cos], axis=-1), (1, reps))
        sin_t = jnp.tile(jnp.concatenate([-sin, sin], axis=-1), (1, reps))
        return cos_t, sin_t

    ca, sa = table(ATT_HD)
    cr, sr = table(RET_HD)
    return ca, sa, cr, sr


def _inproj(x2d, S, g1, w_in_bf, gq, gk, tables, tm):
    T = x2d.shape[0]
    n_pos_blk = S // tm
    tab_spec = pl.BlockSpec((tm, LANES), lambda i: (i % n_pos_blk, 0))
    full = lambda shape: pl.BlockSpec(shape, lambda i: (0,) * len(shape))
    return pl.pallas_call(
        _inproj_kernel,
        grid=(T // tm,),
        in_specs=[pl.BlockSpec((tm, D_MODEL), lambda i: (i, 0)), full((1, D_MODEL)),
                  full((D_MODEL, IN_W)), full((1, LANES)), full((1, LANES)),
                  tab_spec, tab_spec, tab_spec, tab_spec],
        out_specs=[pl.BlockSpec((N_ATT_SLABS, tm, LANES), lambda i: (0, i, 0)),
                   pl.BlockSpec((N_SLABS - N_ATT_SLABS, tm, LANES), lambda i: (0, i, 0))],
        out_shape=[jax.ShapeDtypeStruct((N_ATT_SLABS, T, LANES), F32),
                   jax.ShapeDtypeStruct((N_SLABS - N_ATT_SLABS, T, LANES), BF16)],
        compiler_params=_cparams(("parallel",)),
        name="inproj",
    )(x2d, g1, w_in_bf, gq, gk, *tables)


DILATED_PATTERNS = ((128, 1), (512, 4), (2048, 16))
ATT_HALF = 64
ATT_SB = 2048
ATT_NB = ATT_SB // ATT_QBLK
ATT_UNROLL = 2


def _attn_kernel(q_ref, k_ref, v_ref, o_ref, o_scr, lse_scr, bias_scr, *, S):
    t0 = pl.program_id(2) * ATT_SB
    lane = lax.broadcasted_iota(jnp.int32, (ATT_QBLK, LANES), 1)
    first = lane < ATT_HD
    nt_dims = (((1,), (1,)), ((), ()))

    rel = (lax.broadcasted_iota(jnp.int32, (ATT_QBLK, 2 * ATT_QBLK), 0)
           - lax.broadcasted_iota(jnp.int32, (ATT_QBLK, 2 * ATT_QBLK), 1))
    for oi in range(3):
        bias_scr[oi] = jnp.where(jnp.abs(rel + oi * ATT_HALF) <= ATT_HALF, 0.0, NEG_INF)

    def band(qv, kv, vv, off):
        kw = kv.shape[0]
        bias = bias_scr[off // ATT_HALF][:, :kw]
        zero = jnp.zeros_like(qv)
        q2 = jnp.concatenate([jnp.where(first, qv, zero), jnp.where(first, zero, qv)], axis=0)
        s = lax.dot_general(q2, kv, nt_dims, preferred_element_type=F32) + jnp.concatenate([bias, bias], axis=0)
        m = jnp.max(s, axis=-1, keepdims=True)
        p = jnp.exp(s - m)
        l = jnp.sum(p, axis=-1, keepdims=True)
        acc = jnp.dot(p.astype(BF16), vv, preferred_element_type=F32) * (1.0 / l)
        lse = m + jnp.log(l)
        return (jnp.where(first, acc[:ATT_QBLK], acc[ATT_QBLK:]),
                jnp.where(first, lse[:ATT_QBLK], lse[ATT_QBLK:]))

    def step(n, carry):
        for pi, (window, dil) in enumerate(DILATED_PATTERNS):
            L = S // dil
            kw = min(2 * ATT_QBLK, L)
            per_class = ATT_SB // dil // ATT_QBLK
            r = n // per_class
            bi = n % per_class
            lq0 = t0 // dil + bi * ATT_QBLK
            lk0 = jnp.clip(lq0 - ATT_HALF, 0, L - kw)
            qrow = dil * ATT_QBLK * bi + r
            krow = dil * lk0 + r
            if dil == 1:
                qs = pl.ds(pl.multiple_of(qrow, ATT_QBLK), ATT_QBLK)
                ks = pl.ds(pl.multiple_of(krow, 8), kw)
            else:
                qs = pl.ds(qrow, ATT_QBLK, stride=dil)
                ks = pl.ds(krow, kw, stride=dil)
            o, lse = band(q_ref[qs, :].astype(BF16), k_ref[ks, :].astype(BF16), v_ref[ks, :].astype(BF16),
                          lq0 - lk0)
            o_scr[pi, qs, :] = o
            lse_scr[pi, qs, :] = lse
        return carry

    lax.fori_loop(0, ATT_NB, step, 0, unroll=ATT_UNROLL)

    def merge(c, carry):
        sl = pl.ds(pl.multiple_of(c * ATT_QBLK, ATT_QBLK), ATT_QBLK)
        lses = [lse_scr[pi, sl, :] for pi in range(3)]
        mx = jnp.maximum(jnp.maximum(lses[0], lses[1]), lses[2])
        ws = [jnp.exp(x - mx) for x in lses]
        num = ws[0] * o_scr[0, sl, :] + ws[1] * o_scr[1, sl, :] + ws[2] * o_scr[2, sl, :]
        o_ref[sl, :] = (num / (ws[0] + ws[1] + ws[2])).astype(o_ref.dtype)
        return carry

    lax.fori_loop(0, ATT_NB, merge, 0)


def _attention(qkv4, B, S):
    n_pairs = ATT_W // LANES
    kv_spec = lambda off: pl.BlockSpec((None, None, S, LANES), lambda b, p, i: (off + p, b, 0, 0))
    return pl.pallas_call(
        functools.partial(_attn_kernel, S=S),
        grid=(B, n_pairs, S // ATT_SB),
        in_specs=[pl.BlockSpec((None, None, ATT_SB, LANES), lambda b, p, i: (p, b, i, 0)),
                  kv_spec(n_pairs), kv_spec(2 * n_pairs)],
        out_specs=pl.BlockSpec((None, ATT_SB, LANES), lambda b, p, i: (b, i, p)),
        out_shape=jax.ShapeDtypeStruct((B, S, ATT_W), BF16),
        scratch_shapes=[pltpu.VMEM((3, ATT_SB, LANES), F32), pltpu.VMEM((3, ATT_SB, LANES), F32),
                        pltpu.VMEM((3, ATT_QBLK, 2 * ATT_QBLK), F32)],
        compiler_params=_cparams(("parallel", "parallel", "arbitrary")),
        name="dilated_attn",
    )(qkv4, qkv4, qkv4)


def _ret_kernel(dec_ref, q_ref, k_ref, v_ref, g_ref, gn_ref, o_ref, of_scr, ob_scr, *, S):
    C = RET_CHUNK
    nc = S // C
    h = pl.program_id(1)
    nt_dims = (((1,), (1,)), ((), ()))
    tn_dims = (((0,), (0,)), ((), ()))

    def consts(direction):
        def log_g(shape):
            return -jnp.exp(jnp.full(shape, dec_ref[direction, h], F32))

        row = lax.broadcasted_iota(jnp.int32, (C, C), 0).astype(F32)
        col = lax.broadcasted_iota(jnp.int32, (C, C), 1).astype(F32)
        n = lax.broadcasted_iota(jnp.int32, (C, RET_HD), 0).astype(F32)
        lg = log_g((C, RET_HD))
        if direction == 0:
            diff = row - col
            mask = diff >= 0.0
            k_dec, q_dec = jnp.exp(lg * (C - 1.0 - n)), jnp.exp(lg * (n + 1.0))
        else:
            diff = col - row
            mask = diff > 0.0
            k_dec, q_dec = jnp.exp(lg * n), jnp.exp(lg * (C - n))
        dmat = jnp.where(mask, jnp.exp(log_g((C, C)) * jnp.maximum(diff, 0.0)), 0.0)
        return dmat, k_dec, q_dec, jnp.exp(log_g((RET_HD, RET_HD)) * C)

    def chunk_out(c, state, dmat, k_dec, q_dec, g_chunk):
        sl = pl.ds(pl.multiple_of(c * C, C), C)
        qc, kc, vc = q_ref[sl, :], k_ref[sl, :], v_ref[sl, :]
        qk = lax.dot_general(qc, kc, nt_dims, preferred_element_type=F32)
        inner = (qk * dmat).astype(BF16)
        o = jnp.dot(inner, vc, preferred_element_type=F32)
        o = o + jnp.dot((qc.astype(F32) * q_dec).astype(BF16), state.astype(BF16),
                        preferred_element_type=F32)
        kd = (kc.astype(F32) * k_dec).astype(BF16)
        kv = lax.dot_general(kd, vc, tn_dims, preferred_element_type=F32)
        return sl, o, state * g_chunk + kv

    cf, cb = consts(0), consts(1)

    def scan(i, states):
        sl_f, o_f, st_f = chunk_out(i, states[0], *cf)
        of_scr[sl_f, :] = o_f
        sl_b, o_b, st_b = chunk_out(nc - 1 - i, states[1], *cb)
        ob_scr[sl_b, :] = o_b
        return st_f, st_b

    zero = jnp.zeros((RET_HD, RET_HD), F32)
    lax.fori_loop(0, nc, scan, (zero, zero), unroll=2)

    gn = gn_ref[...]

    def finish(c, carry):
        sl = pl.ds(pl.multiple_of(c * C, C), C)
        o = of_scr[sl, :] + ob_scr[sl, :]
        mu = jnp.mean(o, axis=-1, keepdims=True)
        var = jnp.mean(jnp.square(o - mu), axis=-1, keepdims=True)
        y = (o - mu) * lax.rsqrt(var + NORM_EPS) * gn
        g = g_ref[sl, :].astype(F32)
        o_ref[sl, :] = (y * (g * (1.0 / (1.0 + jnp.exp(-g))))).astype(o_ref.dtype)
        return carry

    lax.fori_loop(0, nc, finish, 0, unroll=2)


def _retention(qkv4, decays, gn, B, S):
    spec = lambda off: pl.BlockSpec((None, None, S, LANES), lambda b, h: (off + h, b, 0, 0))
    return pl.pallas_call(
        functools.partial(_ret_kernel, S=S),
        grid=(B, RET_HEADS),
        in_specs=[pl.BlockSpec(memory_space=pltpu.SMEM),
                  spec(0), spec(RET_HEADS), spec(2 * RET_HEADS), spec(3 * RET_HEADS),
                  pl.BlockSpec((1, LANES), lambda b, h: (0, h))],
        out_specs=pl.BlockSpec((None, S, LANES), lambda b, h: (b, 0, h)),
        out_shape=jax.ShapeDtypeStruct((B, S, RET_W), BF16),
        scratch_shapes=[pltpu.VMEM((S, LANES), F32), pltpu.VMEM((S, LANES), F32)],
        compiler_params=_cparams(("parallel", "arbitrary")),
        name="retention",
    )(decays, qkv4, qkv4, qkv4, qkv4, gn)


HI16 = -65536
ROW_WORDS = D_MODEL // 2 + LANES


def _pack_bf16_pairs(x):
    n = x.shape[1] // 2
    bits = pltpu.bitcast(x.astype(BF16).astype(F32), jnp.int32)
    return (bits[:, n:] & HI16) | lax.shift_right_logical(bits[:, :n], 16)


def _unpack_bf16_pairs(w):
    lo = pltpu.bitcast(lax.shift_left(w, 16), F32).astype(BF16)
    hi = pltpu.bitcast(w & HI16, F32).astype(BF16)
    return jnp.concatenate([lo, hi], axis=1)


def _outproj_kernel(x_ref, att_ref, ret_ref, wo_ref, g2_ref, wr_ref, x1_ref, h2_ref, aff_ref):
    y = jnp.dot(att_ref[...], wo_ref[:ATT_W, :], preferred_element_type=F32)
    y = y + jnp.dot(ret_ref[...], wo_ref[ATT_W:, :], preferred_element_type=F32)
    x1 = x_ref[...] + y
    x1_ref[...] = x1
    ms = jnp.mean(x1 * x1, axis=-1, keepdims=True)
    h2 = x1 * lax.rsqrt(ms + NORM_EPS) * g2_ref[...]
    h2_ref[:, :D_MODEL // 2] = _pack_bf16_pairs(h2)
    logits = lax.dot_general(wr_ref[...], h2.astype(BF16), (((1,), (1,)), ((), ())),
                             preferred_element_type=F32)
    e = jnp.exp(logits - jnp.max(logits, axis=0, keepdims=True))
    aff = e / jnp.sum(e, axis=0, keepdims=True)
    aff_ref[...] = aff
    pad = jnp.zeros((LANES - N_EXPERTS, aff.shape[1]), F32)
    h2_ref[:, D_MODEL // 2:] = pltpu.bitcast(jnp.transpose(jnp.concatenate([aff, pad], axis=0)), jnp.int32)


def _outproj(x2d, att, ret, wo_bf, g2, wr_t_bf, tm):
    T = x2d.shape[0]
    full = lambda shape: pl.BlockSpec(shape, lambda i: (0,) * len(shape))
    return pl.pallas_call(
        _outproj_kernel,
        grid=(T // tm,),
        in_specs=[pl.BlockSpec((tm, D_MODEL), lambda i: (i, 0)),
                  pl.BlockSpec((tm, ATT_W), lambda i: (i, 0)),
                  pl.BlockSpec((tm, RET_W), lambda i: (i, 0)),
                  full((ATT_W + RET_W, D_MODEL)), full((1, D_MODEL)), full((N_EXPERTS, D_MODEL))],
        out_specs=[pl.BlockSpec((tm, D_MODEL), lambda i: (i, 0)),
                   pl.BlockSpec((tm, ROW_WORDS), lambda i: (i, 0)),
                   pl.BlockSpec((N_EXPERTS, tm), lambda i: (0, i))],
        out_shape=[jax.ShapeDtypeStruct((T, D_MODEL), F32),
                   jax.ShapeDtypeStruct((T, ROW_WORDS), jnp.int32),
                   jax.ShapeDtypeStruct((N_EXPERTS, T), F32)],
        compiler_params=_cparams(("parallel",)),
        name="outproj_router",
    )(x2d, att, ret, wo_bf, g2, wr_t_bf)


def _select_kernel(aff_ref, sel_ref, *, cap):
    aff = aff_ref[...]
    E, T = aff.shape
    capf = jnp.float32(cap)

    def count(mask):
        return jnp.sum(jnp.where(mask, 1.0, 0.0), axis=1, keepdims=True)

    def thr_step(i, thr_bits):
        cand = thr_bits | jnp.left_shift(jnp.int32(1), 30 - i)
        return jnp.where(count(aff >= pltpu.bitcast(cand, F32)) >= capf, cand, thr_bits)

    thr = pltpu.bitcast(lax.fori_loop(0, 31, thr_step, jnp.zeros((E, 1), jnp.int32)), F32)
    above = aff > thr
    ties = aff == thr
    need = capf - count(above)
    idx = lax.broadcasted_iota(jnp.int32, (E, T), 1)
    n_idx_bits = max(1, (T - 1).bit_length())

    def cut_step(i, cut):
        cand = cut | jnp.left_shift(jnp.int32(1), n_idx_bits - 1 - i)
        return jnp.where(count(ties & (idx < cand)) < need, cand, cut)

    cut = lax.fori_loop(0, n_idx_bits, cut_step, jnp.zeros((E, 1), jnp.int32))
    sel_ref[...] = jnp.where(above | (ties & (idx <= cut)), 1, 0).astype(jnp.int32)


def _select(aff_t, cap):
    E, T = aff_t.shape
    return pl.pallas_call(
        functools.partial(_select_kernel, cap=cap),
        grid=(1,),
        in_specs=[pl.BlockSpec((E, T), lambda i: (0, 0))],
        out_specs=pl.BlockSpec((E, T), lambda i: (0, 0)),
        out_shape=jax.ShapeDtypeStruct((E, T), jnp.int32),
        compiler_params=_cparams(("arbitrary",)),
        name="expert_select",
    )(aff_t)


CMB_TOK = 512


def _token_major(slots):
    n_e, n_tok = slots.shape
    padded = jnp.concatenate([slots, jnp.full((LANES - n_e, n_tok), -1, jnp.int32)], axis=0)
    return pltpu.bitcast(jnp.transpose(pltpu.bitcast(padded, F32)), jnp.int32)


def _spread_slots(pos, bases, width):
    n_e, n_col = len(bases), pos.shape[1]
    col = lax.broadcasted_iota(jnp.int32, pos.shape, 1)
    base_arr = jnp.zeros(pos.shape, jnp.int32)
    for e in range(n_e):
        base_arr = jnp.where(col == e, bases[e], base_arr)
    rel = pos - base_arr
    rel = jnp.where((rel >= 0) & (rel < width), rel, -1).astype(F32).astype(BF16)
    expander = (lax.broadcasted_iota(jnp.int32, (n_col, n_e * width), 0)
                == lax.broadcasted_iota(jnp.int32, (n_col, n_e * width), 1) // width)
    return jnp.dot(rel, jnp.where(expander, 1.0, 0.0).astype(BF16), preferred_element_type=F32)


def _compact_kernel(excl_ref, cnt_ref, slots_ref, idx_ref, pos_ref, *, cap, nblk):
    b = pl.program_id(0)
    ntiles = cap // LANES
    width = 2 * LANES

    @pl.when(b == 0)
    def _():
        idx_ref[...] = jnp.zeros(idx_ref.shape, F32)

    pos_ref[...] = _token_major(slots_ref[...])

    lane = lax.broadcasted_iota(jnp.int32, (CMB_TOK, width), 1)
    tok = (b * CMB_TOK + lax.broadcasted_iota(jnp.int32, (CMB_TOK, width), 0)).astype(F32)

    def window(e, tile_nominal):
        tile = jnp.minimum(tile_nominal, ntiles - 2)
        pos_col = pos_ref[:, e:e + 1]
        hit = ((pos_col - tile * LANES) == lane) & (pos_col >= tile_nominal * LANES)
        row = jnp.sum(jnp.where(hit, tok, 0.0), axis=0, keepdims=True)
        for j in range(2):
            idx_ref[e, pl.ds(tile + j, 1), :] += row[:, j * LANES:(j + 1) * LANES]

    tiles = [jnp.minimum(excl_ref[e * nblk + b] // LANES, ntiles - 2) for e in range(N_EXPERTS)]
    rel = _spread_slots(pos_ref[...], [t * LANES for t in tiles], width)
    lane_f = lax.broadcasted_iota(jnp.int32, (CMB_TOK, LANES), 1).astype(F32)
    tok_f = (b * CMB_TOK + lax.broadcasted_iota(jnp.int32, (CMB_TOK, LANES), 0)).astype(F32)
    for e in range(N_EXPERTS):
        for j in range(2):
            piece = rel[:, e * width + j * LANES:e * width + (j + 1) * LANES]
            row = jnp.sum(jnp.where(piece == lane_f + float(j * LANES), tok_f, 0.0), axis=0, keepdims=True)
            idx_ref[e, pl.ds(tiles[e] + j, 1), :] += row

    for e in range(N_EXPERTS):
        first = excl_ref[e * nblk + b]
        tile0 = first // LANES
        need_end = first + cnt_ref[e * nblk + b]
        n_extra = jnp.maximum((need_end - tile0 * LANES + width - 1) // width - 1, 0)

        def extra(k, carry):
            window(e, tile0 + 2 * (k + 1))
            return carry

        lax.fori_loop(0, n_extra, extra, 0)


def _compact(slots, excl, cnt, cap):
    E, T = slots.shape
    nblk = T // CMB_TOK
    grid_spec = pltpu.PrefetchScalarGridSpec(
        num_scalar_prefetch=2,
        grid=(nblk,),
        in_specs=[pl.BlockSpec((E, CMB_TOK), lambda b, *_: (0, b))],
        out_specs=pl.BlockSpec((E, cap // LANES, LANES), lambda b, *_: (0, 0, 0)),
        scratch_shapes=[pltpu.VMEM((CMB_TOK, LANES), jnp.int32)],
    )
    idx = pl.pallas_call(
        functools.partial(_compact_kernel, cap=cap, nblk=nblk),
        grid_spec=grid_spec,
        out_shape=jax.ShapeDtypeStruct((E, cap // LANES, LANES), F32),
        compiler_params=_cparams(("arbitrary",)),
        name="expert_lists",
    )(excl, cnt, slots)
    return idx.reshape(E, cap).astype(jnp.int32)


SC_CORES = 2
SC_SUBCORES = 16
SC_ROWS = 64


def _sc_gather(table, idx):
    n, width = idx.shape[0], table.shape[1]
    workers = SC_CORES * SC_SUBCORES
    per_worker = n // workers
    mesh = plsc.VectorSubcoreMesh(core_axis_name="c", subcore_axis_name="s")

    @functools.partial(
        pl.kernel, mesh=mesh,
        out_type=jax.ShapeDtypeStruct((n, width), table.dtype),
        scratch_types=[pltpu.VMEM((SC_ROWS,), jnp.int32), pltpu.VMEM((SC_ROWS, width), table.dtype),
                       pltpu.SemaphoreType.DMA],
    )
    def gather(table_hbm, idx_hbm, out_hbm, idx_v, rows_v, sem):
        base = (lax.axis_index("s") * SC_CORES + lax.axis_index("c")) * per_worker

        @pl.loop(0, per_worker // SC_ROWS)
        def _(i):
            off = pl.multiple_of(base + i * SC_ROWS, SC_ROWS)
            pltpu.sync_copy(idx_hbm.at[pl.ds(off, SC_ROWS)], idx_v)
            pltpu.async_copy(table_hbm.at[idx_v], rows_v, sem).wait()
            pltpu.sync_copy(rows_v, out_hbm.at[pl.ds(off, SC_ROWS)])

    return gather(table, idx)


FF_CHUNK = 512


def _ffn_kernel(xin_ref, wg_ref, wu_ref, wd_ref, y_ref):
    xin = _unpack_bf16_pairs(xin_ref[:, :D_MODEL // 2])
    affs = pltpu.bitcast(xin_ref[:, D_MODEL // 2:], F32)
    lane = lax.broadcasted_iota(jnp.int32, affs.shape, 1)
    gate = jnp.sum(jnp.where(lane == pl.program_id(0), affs, 0.0), axis=-1, keepdims=True)
    acc = jnp.zeros(y_ref.shape, F32)
    for f in range(D_FF // FF_CHUNK):
        fs = slice(f * FF_CHUNK, (f + 1) * FF_CHUNK)
        a = jnp.dot(xin, wg_ref[:, fs], preferred_element_type=F32)
        u = jnp.dot(xin, wu_ref[:, fs], preferred_element_type=F32)
        hmid = (a * (1.0 / (1.0 + jnp.exp(-a))) * u).astype(BF16)
        acc = acc + jnp.dot(hmid, wd_ref[fs, :], preferred_element_type=F32)
    y_ref[...] = (acc * gate).astype(y_ref.dtype)


def _ffn(xin, wg_bf, wu_bf, wd_bf, tm):
    E, cap, _ = xin.shape
    return pl.pallas_call(
        _ffn_kernel,
        grid=(E, cap // tm),
        in_specs=[pl.BlockSpec((None, tm, ROW_WORDS), lambda e, j: (e, j, 0)),
                  pl.BlockSpec((None, D_MODEL, D_FF), lambda e, j: (e, 0, 0)),
                  pl.BlockSpec((None, D_MODEL, D_FF), lambda e, j: (e, 0, 0)),
                  pl.BlockSpec((None, D_FF, D_MODEL), lambda e, j: (e, 0, 0))],
        out_specs=pl.BlockSpec((None, tm, D_MODEL), lambda e, j: (e, j, 0)),
        out_shape=jax.ShapeDtypeStruct((E, cap, D_MODEL), BF16),
        compiler_params=_cparams(("parallel", "arbitrary")),
        name="expert_ffn",
    )(xin, wg_bf, wu_bf, wd_bf)


CMB_ROWS = 128
BF16_SUBLANES = 16


def _combine_kernel(excl_ref, cnt_ref, x1_ref, slots_ref, y_hbm, o_ref, ybuf, xbuf, sem, xsem, pos_ref, *, cap, nblk):
    b = pl.program_id(0)
    slot = b % 2
    lane = lax.broadcasted_iota(jnp.int32, (CMB_TOK, CMB_ROWS), 1)

    def first_row(e, blk):
        return (excl_ref[e * nblk + blk] // BF16_SUBLANES) * BF16_SUBLANES

    def fetch(e, row, dst, s):
        row = pl.multiple_of(jnp.minimum(row, cap - CMB_ROWS), BF16_SUBLANES)
        return pltpu.make_async_copy(y_hbm.at[e, pl.ds(row, CMB_ROWS), :], dst, s)

    def start_block(blk, sl):
        for e in range(N_EXPERTS):
            fetch(e, first_row(e, blk), ybuf.at[sl, e], sem.at[sl, e]).start()

    @pl.when(b == 0)
    def _():
        start_block(0, 0)

    @pl.when(b + 1 < nblk)
    def _():
        start_block(b + 1, 1 - slot)

    def onehot(pos_col, nominal):
        hit = ((pos_col - jnp.minimum(nominal, cap - CMB_ROWS)) == lane) & (pos_col >= nominal)
        return jnp.where(hit, 1.0, 0.0).astype(BF16)

    pos_ref[...] = _token_major(slots_ref[...])
    bases = []
    for e in range(N_EXPERTS):
        nominal0 = first_row(e, b)
        fetch(e, nominal0, ybuf.at[slot, e], sem.at[slot, e]).wait()
        bases.append(jnp.minimum(nominal0, cap - CMB_ROWS))
    rel = _spread_slots(pos_ref[...], bases, CMB_ROWS)
    lane_f = lane.astype(F32)
    parts = [jnp.where(rel[:, e * CMB_ROWS:(e + 1) * CMB_ROWS] == lane_f, 1.0, 0.0).astype(BF16)
             for e in range(N_EXPERTS)]
    rows = ybuf[slot].reshape(N_EXPERTS * CMB_ROWS, D_MODEL)
    o_ref[...] = x1_ref[...] + jnp.dot(jnp.concatenate(parts, axis=1), rows, preferred_element_type=F32)

    for e in range(N_EXPERTS):
        nominal0 = first_row(e, b)
        need_end = excl_ref[e * nblk + b] + cnt_ref[e * nblk + b]
        n_extra = jnp.maximum((need_end - nominal0 + CMB_ROWS - 1) // CMB_ROWS - 1, 0)

        def extra(k, carry):
            nominal = nominal0 + (k + 1) * CMB_ROWS
            cp = fetch(e, nominal, xbuf, xsem)
            cp.start()
            cp.wait()
            o_ref[...] += jnp.dot(onehot(pos_ref[:, e:e + 1], nominal), xbuf[...], preferred_element_type=F32)
            return carry

        lax.fori_loop(0, n_extra, extra, 0)


def _combine(x1, slots, y, excl, cnt):
    T = x1.shape[0]
    E, cap, _ = y.shape
    nblk = T // CMB_TOK
    grid_spec = pltpu.PrefetchScalarGridSpec(
        num_scalar_prefetch=2,
        grid=(nblk,),
        in_specs=[pl.BlockSpec((CMB_TOK, D_MODEL), lambda b, *_: (b, 0)),
                  pl.BlockSpec((E, CMB_TOK), lambda b, *_: (0, b)),
                  pl.BlockSpec(memory_space=pl.ANY)],
        out_specs=pl.BlockSpec((CMB_TOK, D_MODEL), lambda b, *_: (b, 0)),
        scratch_shapes=[pltpu.VMEM((2, E, CMB_ROWS, D_MODEL), BF16),
                        pltpu.VMEM((CMB_ROWS, D_MODEL), BF16),
                        pltpu.SemaphoreType.DMA((2, E)),
                        pltpu.SemaphoreType.DMA(()),
                        pltpu.VMEM((CMB_TOK, LANES), jnp.int32)],
    )
    return pl.pallas_call(
        functools.partial(_combine_kernel, cap=cap, nblk=nblk),
        grid_spec=grid_spec,
        out_shape=jax.ShapeDtypeStruct((T, D_MODEL), F32),
        compiler_params=_cparams(("arbitrary",)),
        name="moe_combine",
    )(excl, cnt, x1, slots, y)


def _layer(x, p):
    B, S, _ = x.shape
    T = B * S
    tm = 512
    x2d = x.reshape(T, D_MODEL)
    qkv_att, qkv_ret = _inproj(x2d, S, p["g1"], p["w_in"], p["gq"], p["gk"], _rope_tables(S), tm)
    att = _attention(qkv_att.reshape(N_ATT_SLABS, B, S, LANES), B, S).reshape(T, ATT_W)
    ret = _retention(qkv_ret.reshape(N_SLABS - N_ATT_SLABS, B, S, LANES), p["decays"], p["gn"], B, S).reshape(T, RET_W)
    x1, h2, aff_t = _outproj(x2d, att, ret, p["w_out"], p["g2"], p["w_router_t"], tm)

    cap = CAPACITY_FACTOR * T // N_EXPERTS
    sel = _select(aff_t, cap)

    csum = jnp.cumsum(sel, axis=1)
    slots = jnp.where(sel > 0, csum - 1, -1)
    ends = csum[:, CMB_TOK - 1::CMB_TOK]
    excl = jnp.concatenate([jnp.zeros((N_EXPERTS, 1), jnp.int32), ends[:, :-1]], axis=1).reshape(-1)
    cnt = ends.reshape(-1) - excl

    idx = _compact(slots, excl, cnt, cap)
    xin = _sc_gather(h2, idx.reshape(-1)).reshape(N_EXPERTS, cap, ROW_WORDS)
    y = _ffn(xin, p["w_gate"], p["w_up"], p["w_down"], tm)
    out = _combine(x1, slots, y, excl, cnt)
    return out.reshape(B, S, D_MODEL)


def kernel(x_prompt, x_sample, norm1_g, w_in, attn_qnorm_g, attn_knorm_g, ret_decay_fwd, ret_decay_bwd,
           ret_norm_g, w_out, norm2_g, w_router, w_gate_e, w_up_e, w_down_e):
    y_prompt, y_sample = x_prompt, x_sample
    for l in range(norm1_g.shape[0]):
        p = {
            "g1": norm1_g[l][None, :],
            "w_in": w_in[l].astype(BF16),
            "gq": jnp.tile(attn_qnorm_g[l], LANES // ATT_HD)[None, :],
            "gk": jnp.tile(attn_knorm_g[l], LANES // ATT_HD)[None, :],
            "decays": jnp.stack([ret_decay_fwd[l], ret_decay_bwd[l]]).astype(F32),
            "gn": ret_norm_g[l][None, :].astype(F32),
            "w_out": w_out[l].astype(BF16),
            "g2": norm2_g[l][None, :],
            "w_router_t": w_router[l].T.astype(BF16),
            "w_gate": w_gate_e[l].astype(BF16),
            "w_up": w_up_e[l].astype(BF16),
            "w_down": w_down_e[l].astype(BF16),
        }
        y_prompt = _layer(y_prompt, p)
        y_sample = _layer(y_sample, p)
    return (y_prompt, y_sample)
```

```python
import functools

import jax
import jax.numpy as jnp
from jax import lax
from jax.experimental import pallas as pl
from jax.experimental.pallas import tpu as pltpu
from jax.experimental.pallas import tpu_sc as plsc

F32 = jnp.float32
BF16 = jnp.bfloat16

D_MODEL = 1024
ATT_HEADS, ATT_HD = 8, 64
RET_HEADS, RET_HD = 4, 128
ATT_W = ATT_HEADS * ATT_HD
RET_W = RET_HEADS * RET_HD
IN_W = 3 * ATT_W + 4 * RET_W
LANES = 128
N_SLABS = IN_W // LANES
GROUP_W = 512
SLABS_PER_GROUP = GROUP_W // LANES
N_ATT_SLABS = 3 * ATT_W // LANES
RET_CHUNK = 256
ROPE_THETA = 10000.0
N_EXPERTS = 16
D_FF = 2048
CAPACITY_FACTOR = 2
NORM_EPS = 1e-6
NEG_INF = -1e30
ATT_QBLK = 128
VMEM_LIMIT = 56 * 1024 * 1024


def _cparams(sem):
    return pltpu.CompilerParams(dimension_semantics=sem, vmem_limit_bytes=VMEM_LIMIT)


def _inproj_kernel(x_ref, g1_ref, w_ref, gq_ref, gk_ref, ca_ref, sa_ref, cr_ref, sr_ref, oa_ref, or_ref):
    x = x_ref[...]
    ms = jnp.mean(x * x, axis=-1, keepdims=True)
    h = (x * lax.rsqrt(ms + NORM_EPS) * g1_ref[...]).astype(BF16)
    tm = x.shape[0]
    lane = lax.broadcasted_iota(jnp.int32, (tm, LANES), 1)
    first = lane < ATT_HD
    low_half = (lane & (ATT_HD // 2)) == 0
    ca, sa, cr, sr = ca_ref[...], sa_ref[...], cr_ref[...], sr_ref[...]
    for grp in range(IN_W // GROUP_W):
        acc = jnp.dot(h, w_ref[:, grp * GROUP_W:(grp + 1) * GROUP_W], preferred_element_type=F32)
        for p in range(SLABS_PER_GROUP):
            a = acc[:, p * LANES:(p + 1) * LANES]
            if grp in (0, 1):
                sq = a * a
                s0 = jnp.sum(jnp.where(first, sq, 0.0), axis=-1, keepdims=True)
                s1 = jnp.sum(jnp.where(first, 0.0, sq), axis=-1, keepdims=True)
                ms2 = jnp.where(first, s0, s1) * (1.0 / ATT_HD)
                g = gq_ref[...] if grp == 0 else gk_ref[...]
                y = a * lax.rsqrt(ms2 + NORM_EPS) * g
                partner = jnp.where(low_half, pltpu.roll(y, LANES - ATT_HD // 2, 1),
                                    pltpu.roll(y, ATT_HD // 2, 1))
                r = y * ca + partner * sa
                if grp == 0:
                    r = r * (ATT_HD ** -0.5)
            elif grp in (3, 4):
                r = a * cr + pltpu.roll(a, RET_HD // 2, 1) * sr
                if grp == 4:
                    r = r * (RET_HD ** -0.5)
            else:
                r = a
            slab = grp * SLABS_PER_GROUP + p
            if slab < N_ATT_SLABS:
                oa_ref[slab] = r
            else:
                or_ref[slab - N_ATT_SLABS] = r.astype(BF16)


def _rope_tables(S):
    pos = jnp.arange(S, dtype=F32)

    def table(hd):
        inv_freq = ROPE_THETA ** (-jnp.arange(0, hd, 2, dtype=F32) / hd)
        ang = pos[:, None] * inv_freq[None, :]
        cos, sin = jnp.cos(ang), jnp.sin(ang)
        reps = LANES // hd
        cos_t = jnp.tile(jnp.concatenate([cos, cos], axis=-1), (1, reps))
        sin_t = jnp.tile(jnp.concatenate([-sin, sin], axis=-1), (1, reps))
        return cos_t, sin_t

    ca, sa = table(ATT_HD)
    cr, sr = table(RET_HD)
    return ca, sa, cr, sr


def _inproj(x2d, S, g1, w_in_bf, gq, gk, tables, tm):
    T = x2d.shape[0]
    n_pos_blk = S // tm
    tab_spec = pl.BlockSpec((tm, LANES), lambda i: (i % n_pos_blk, 0))
    full = lambda shape: pl.BlockSpec(shape, lambda i: (0,) * len(shape))
    return pl.pallas_call(
        _inproj_kernel,
        grid=(T // tm,),
        in_specs=[pl.BlockSpec((tm, D_MODEL), lambda i: (i, 0)), full((1, D_MODEL)),
                  full((D_MODEL, IN_W)), full((1, LANES)), full((1, LANES)),
                  tab_spec, tab_spec, tab_spec, tab_spec],
        out_specs=[pl.BlockSpec((N_ATT_SLABS, tm, LANES), lambda i: (0, i, 0)),
                   pl.BlockSpec((N_SLABS - N_ATT_SLABS, tm, LANES), lambda i: (0, i, 0))],
        out_shape=[jax.ShapeDtypeStruct((N_ATT_SLABS, T, LANES), F32),
                   jax.ShapeDtypeStruct((N_SLABS - N_ATT_SLABS, T, LANES), BF16)],
        compiler_params=_cparams(("parallel",)),
        name="inproj",
    )(x2d, g1, w_in_bf, gq, gk, *tables)


DILATED_PATTERNS = ((128, 1), (512, 4), (2048, 16))
ATT_HALF = 64
ATT_SB = 2048
ATT_NB = ATT_SB // ATT_QBLK
ATT_UNROLL = 2


def _attn_kernel(q_ref, k_ref, v_ref, o_ref, o_scr, lse_scr, bias_scr, *, S):
    t0 = pl.program_id(2) * ATT_SB
    lane = lax.broadcasted_iota(jnp.int32, (ATT_QBLK, LANES), 1)
    first = lane < ATT_HD
    nt_dims = (((1,), (1,)), ((), ()))

    rel = (lax.broadcasted_iota(jnp.int32, (ATT_QBLK, 2 * ATT_QBLK), 0)
           - lax.broadcasted_iota(jnp.int32, (ATT_QBLK, 2 * ATT_QBLK), 1))
    for oi in range(3):
        bias_scr[oi] = jnp.where(jnp.abs(rel + oi * ATT_HALF) <= ATT_HALF, 0.0, NEG_INF)

    def band(qv, kv, vv, off):
        kw = kv.shape[0]
        bias = bias_scr[off // ATT_HALF][:, :kw]
        zero = jnp.zeros_like(qv)
        q2 = jnp.concatenate([jnp.where(first, qv, zero), jnp.where(first, zero, qv)], axis=0)
        s = lax.dot_general(q2, kv, nt_dims, preferred_element_type=F32) + jnp.concatenate([bias, bias], axis=0)
        m = jnp.max(s, axis=-1, keepdims=True)
        p = jnp.exp(s - m)
        l = jnp.sum(p, axis=-1, keepdims=True)
        acc = jnp.dot(p.astype(BF16), vv, preferred_element_type=F32) * (1.0 / l)
        lse = m + jnp.log(l)
        return (jnp.where(first, acc[:ATT_QBLK], acc[ATT_QBLK:]),
                jnp.where(first, lse[:ATT_QBLK], lse[ATT_QBLK:]))

    def step(n, carry):
        for pi, (window, dil) in enumerate(DILATED_PATTERNS):
            L = S // dil
            kw = min(2 * ATT_QBLK, L)
            per_class = ATT_SB // dil // ATT_QBLK
            r = n // per_class
            bi = n % per_class
            lq0 = t0 // dil + bi * ATT_QBLK
            lk0 = jnp.clip(lq0 - ATT_HALF, 0, L - kw)
            qrow = dil * ATT_QBLK * bi + r
            krow = dil * lk0 + r
            if dil == 1:
                qs = pl.ds(pl.multiple_of(qrow, ATT_QBLK), ATT_QBLK)
                ks = pl.ds(pl.multiple_of(krow, 8), kw)
            else:
                qs = pl.ds(qrow, ATT_QBLK, stride=dil)
                ks = pl.ds(krow, kw, stride=dil)
            o, lse = band(q_ref[qs, :].astype(BF16), k_ref[ks, :].astype(BF16), v_ref[ks, :].astype(BF16),
                          lq0 - lk0)
            o_scr[pi, qs, :] = o
            lse_scr[pi, qs, :] = lse
        return carry

    lax.fori_loop(0, ATT_NB, step, 0, unroll=ATT_UNROLL)

    def merge(c, carry):
        sl = pl.ds(pl.multiple_of(c * ATT_QBLK, ATT_QBLK), ATT_QBLK)
        lses = [lse_scr[pi, sl, :] for pi in range(3)]
        mx = jnp.maximum(jnp.maximum(lses[0], lses[1]), lses[2])
        ws = [jnp.exp(x - mx) for x in lses]
        num = ws[0] * o_scr[0, sl, :] + ws[1] * o_scr[1, sl, :] + ws[2] * o_scr[2, sl, :]
        o_ref[sl, :] = (num / (ws[0] + ws[1] + ws[2])).astype(o_ref.dtype)
        return carry

    lax.fori_loop(0, ATT_NB, merge, 0)


def _attention(qkv4, B, S):
    n_pairs = ATT_W // LANES
    kv_spec = lambda off: pl.BlockSpec((None, None, S, LANES), lambda b, p, i: (off + p, b, 0, 0))
    return pl.pallas_call(
        functools.partial(_attn_kernel, S=S),
        grid=(B, n_pairs, S // ATT_SB),
        in_specs=[pl.BlockSpec((None, None, ATT_SB, LANES), lambda b, p, i: (p, b, i, 0)),
                  kv_spec(n_pairs), kv_spec(2 * n_pairs)],
        out_specs=pl.BlockSpec((None, ATT_SB, LANES), lambda b, p, i: (b, i, p)),
        out_shape=jax.ShapeDtypeStruct((B, S, ATT_W), BF16),
        scratch_shapes=[pltpu.VMEM((3, ATT_SB, LANES), F32), pltpu.VMEM((3, ATT_SB, LANES), F32),
                        pltpu.VMEM((3, ATT_QBLK, 2 * ATT_QBLK), F32)],
        compiler_params=_cparams(("parallel", "parallel", "arbitrary")),
        name="dilated_attn",
    )(qkv4, qkv4, qkv4)


def _ret_kernel(dec_ref, q_ref, k_ref, v_ref, g_ref, gn_ref, o_ref, of_scr, ob_scr, *, S):
    C = RET_CHUNK
    nc = S // C
    h = pl.program_id(1)
    nt_dims = (((1,), (1,)), ((), ()))
    tn_dims = (((0,), (0,)), ((), ()))

    def consts(direction):
        def log_g(shape):
            return -jnp.exp(jnp.full(shape, dec_ref[direction, h], F32))

        row = lax.broadcasted_iota(jnp.int32, (C, C), 0).astype(F32)
        col = lax.broadcasted_iota(jnp.int32, (C, C), 1).astype(F32)
        n = lax.broadcasted_iota(jnp.int32, (C, RET_HD), 0).astype(F32)
        lg = log_g((C, RET_HD))
        if direction == 0:
            diff = row - col
            mask = diff >= 0.0
            k_dec, q_dec = jnp.exp(lg * (C - 1.0 - n)), jnp.exp(lg * (n + 1.0))
        else:
            diff = col - row
            mask = diff > 0.0
            k_dec, q_dec = jnp.exp(lg * n), jnp.exp(lg * (C - n))
        dmat = jnp.where(mask, jnp.exp(log_g((C, C)) * jnp.maximum(diff, 0.0)), 0.0)
        return dmat, k_dec, q_dec, jnp.exp(log_g((RET_HD, RET_HD)) * C)

    def chunk_out(c, state, dmat, k_dec, q_dec, g_chunk):
        sl = pl.ds(pl.multiple_of(c * C, C), C)
        qc, kc, vc = q_ref[sl, :], k_ref[sl, :], v_ref[sl, :]
        qk = lax.dot_general(qc, kc, nt_dims, preferred_element_type=F32)
        inner = (qk * dmat).astype(BF16)
        o = jnp.dot(inner, vc, preferred_element_type=F32)
        o = o + jnp.dot((qc.astype(F32) * q_dec).astype(BF16), state.astype(BF16),
                        preferred_element_type=F32)
        kd = (kc.astype(F32) * k_dec).astype(BF16)
        kv = lax.dot_general(kd, vc, tn_dims, preferred_element_type=F32)
        return sl, o, state * g_chunk + kv

    cf, cb = consts(0), consts(1)

    def scan(i, states):
        sl_f, o_f, st_f = chunk_out(i, states[0], *cf)
        of_scr[sl_f, :] = o_f
        sl_b, o_b, st_b = chunk_out(nc - 1 - i, states[1], *cb)
        ob_scr[sl_b, :] = o_b
        return st_f, st_b

    zero = jnp.zeros((RET_HD, RET_HD), F32)
    lax.fori_loop(0, nc, scan, (zero, zero), unroll=2)

    gn = gn_ref[...]

    def finish(c, carry):
        sl = pl.ds(pl.multiple_of(c * C, C), C)
        o = of_scr[sl, :] + ob_scr[sl, :]
        mu = jnp.mean(o, axis=-1, keepdims=True)
        var = jnp.mean(jnp.square(o - mu), axis=-1, keepdims=True)
        y = (o - mu) * lax.rsqrt(var + NORM_EPS) * gn
        g = g_ref[sl, :].astype(F32)
        o_ref[sl, :] = (y * (g * (1.0 / (1.0 + jnp.exp(-g))))).astype(o_ref.dtype)
        return carry

    lax.fori_loop(0, nc, finish, 0, unroll=2)


def _retention(qkv4, decays, gn, B, S):
    spec = lambda off: pl.BlockSpec((None, None, S, LANES), lambda b, h: (off + h, b, 0, 0))
    return pl.pallas_call(
        functools.partial(_ret_kernel, S=S),
        grid=(B, RET_HEADS),
        in_specs=[pl.BlockSpec(memory_space=pltpu.SMEM),
                  spec(0), spec(RET_HEADS), spec(2 * RET_HEADS), spec(3 * RET_HEADS),
                  pl.BlockSpec((1, LANES), lambda b, h: (0, h))],
        out_specs=pl.BlockSpec((None, S, LANES), lambda b, h: (b, 0, h)),
        out_shape=jax.ShapeDtypeStruct((B, S, RET_W), BF16),
        scratch_shapes=[pltpu.VMEM((S, LANES), F32), pltpu.VMEM((S, LANES), F32)],
        compiler_params=_cparams(("parallel", "arbitrary")),
        name="retention",
    )(decays, qkv4, qkv4, qkv4, qkv4, gn)


HI16 = -65536
ROW_WORDS = D_MODEL // 2 + LANES


def _pack_bf16_pairs(x):
    n = x.shape[1] // 2
    bits = pltpu.bitcast(x.astype(BF16).astype(F32), jnp.int32)
    return (bits[:, n:] & HI16) | lax.shift_right_logical(bits[:, :n], 16)


def _unpack_bf16_pairs(w):
    lo = pltpu.bitcast(lax.shift_left(w, 16), F32).astype(BF16)
    hi = pltpu.bitcast(w & HI16, F32).astype(BF16)
    return jnp.concatenate([lo, hi], axis=1)


def _outproj_kernel(x_ref, att_ref, ret_ref, wo_ref, g2_ref, wr_ref, x1_ref, h2_ref, aff_ref):
    y = jnp.dot(att_ref[...], wo_ref[:ATT_W, :], preferred_element_type=F32)
    y = y + jnp.dot(ret_ref[...], wo_ref[ATT_W:, :], preferred_element_type=F32)
    x1 = x_ref[...] + y
    x1_ref[...] = x1
    ms = jnp.mean(x1 * x1, axis=-1, keepdims=True)
    h2 = x1 * lax.rsqrt(ms + NORM_EPS) * g2_ref[...]
    h2_ref[:, :D_MODEL // 2] = _pack_bf16_pairs(h2)
    logits = lax.dot_general(wr_ref[...], h2.astype(BF16), (((1,), (1,)), ((), ())),
                             preferred_element_type=F32)
    e = jnp.exp(logits - jnp.max(logits, axis=0, keepdims=True))
    aff = e / jnp.sum(e, axis=0, keepdims=True)
    aff_ref[...] = aff
    pad = jnp.zeros((LANES - N_EXPERTS, aff.shape[1]), F32)
    h2_ref[:, D_MODEL // 2:] = pltpu.bitcast(jnp.transpose(jnp.concatenate([aff, pad], axis=0)), jnp.int32)


def _outproj(x2d, att, ret, wo_bf, g2, wr_t_bf, tm):
    T = x2d.shape[0]
    full = lambda shape: pl.BlockSpec(shape, lambda i: (0,) * len(shape))
    return pl.pallas_call(
        _outproj_kernel,
        grid=(T // tm,),
        in_specs=[pl.BlockSpec((tm, D_MODEL), lambda i: (i, 0)),
                  pl.BlockSpec((tm, ATT_W), lambda i: (i, 0)),
                  pl.BlockSpec((tm, RET_W), lambda i: (i, 0)),
                  full((ATT_W + RET_W, D_MODEL)), full((1, D_MODEL)), full((N_EXPERTS, D_MODEL))],
        out_specs=[pl.BlockSpec((tm, D_MODEL), lambda i: (i, 0)),
                   pl.BlockSpec((tm, ROW_WORDS), lambda i: (i, 0)),
                   pl.BlockSpec((N_EXPERTS, tm), lambda i: (0, i))],
        out_shape=[jax.ShapeDtypeStruct((T, D_MODEL), F32),
                   jax.ShapeDtypeStruct((T, ROW_WORDS), jnp.int32),
                   jax.ShapeDtypeStruct((N_EXPERTS, T), F32)],
        compiler_params=_cparams(("parallel",)),
        name="outproj_router",
    )(x2d, att, ret, wo_bf, g2, wr_t_bf)


def _select_kernel(aff_ref, sel_ref, *, cap):
    aff = aff_ref[...]
    E, T = aff.shape
    capf = jnp.float32(cap)

    def count(mask):
        return jnp.sum(jnp.where(mask, 1.0, 0.0), axis=1, keepdims=True)

    def thr_step(i, thr_bits):
        cand = thr_bits | jnp.left_shift(jnp.int32(1), 30 - i)
        return jnp.where(count(aff >= pltpu.bitcast(cand, F32)) >= capf, cand, thr_bits)

    thr = pltpu.bitcast(lax.fori_loop(0, 31, thr_step, jnp.zeros((E, 1), jnp.int32)), F32)
    above = aff > thr
    ties = aff == thr
    need = capf - count(above)
    idx = lax.broadcasted_iota(jnp.int32, (E, T), 1)
    n_idx_bits = max(1, (T - 1).bit_length())

    def cut_step(i, cut):
        cand = cut | jnp.left_shift(jnp.int32(1), n_idx_bits - 1 - i)
        return jnp.where(count(ties & (idx < cand)) < need, cand, cut)

    cut = lax.fori_loop(0, n_idx_bits, cut_step, jnp.zeros((E, 1), jnp.int32))
    sel_ref[...] = jnp.where(above | (ties & (idx <= cut)), 1, 0).astype(jnp.int32)


def _select(aff_t, cap):
    E, T = aff_t.shape
    return pl.pallas_call(
        functools.partial(_select_kernel, cap=cap),
        grid=(1,),
        in_specs=[pl.BlockSpec((E, T), lambda i: (0, 0))],
        out_specs=pl.BlockSpec((E, T), lambda i: (0, 0)),
        out_shape=jax.ShapeDtypeStruct((E, T), jnp.int32),
        compiler_params=_cparams(("arbitrary",)),
        name="expert_select",
    )(aff_t)


CMB_TOK = 512


def _token_major(slots):
    n_e, n_tok = slots.shape
    padded = jnp.concatenate([slots, jnp.full((LANES - n_e, n_tok), -1, jnp.int32)], axis=0)
    return pltpu.bitcast(jnp.transpose(pltpu.bitcast(padded, F32)), jnp.int32)


def _spread_slots(pos, bases, width):
    n_e, n_col = len(bases), pos.shape[1]
    col = lax.broadcasted_iota(jnp.int32, pos.shape, 1)
    base_arr = jnp.zeros(pos.shape, jnp.int32)
    for e in range(n_e):
        base_arr = jnp.where(col == e, bases[e], base_arr)
    rel = pos - base_arr
    rel = jnp.where((rel >= 0) & (rel < width), rel, -1).astype(F32).astype(BF16)
    expander = (lax.broadcasted_iota(jnp.int32, (n_col, n_e * width), 0)
                == lax.broadcasted_iota(jnp.int32, (n_col, n_e * width), 1) // width)
    return jnp.dot(rel, jnp.where(expander, 1.0, 0.0).astype(BF16), preferred_element_type=F32)


def _compact_kernel(excl_ref, cnt_ref, slots_ref, idx_ref, pos_ref, *, cap, nblk):
    b = pl.program_id(0)
    ntiles = cap // LANES
    width = 2 * LANES

    @pl.when(b == 0)
    def _():
        idx_ref[...] = jnp.zeros(idx_ref.shape, F32)

    pos_ref[...] = _token_major(slots_ref[...])

    lane = lax.broadcasted_iota(jnp.int32, (CMB_TOK, width), 1)
    tok = (b * CMB_TOK + lax.broadcasted_iota(jnp.int32, (CMB_TOK, width), 0)).astype(F32)

    def window(e, tile_nominal):
        tile = jnp.minimum(tile_nominal, ntiles - 2)
        pos_col = pos_ref[:, e:e + 1]
        hit = ((pos_col - tile * LANES) == lane) & (pos_col >= tile_nominal * LANES)
        row = jnp.sum(jnp.where(hit, tok, 0.0), axis=0, keepdims=True)
        for j in range(2):
            idx_ref[e, pl.ds(tile + j, 1), :] += row[:, j * LANES:(j + 1) * LANES]

    tiles = [jnp.minimum(excl_ref[e * nblk + b] // LANES, ntiles - 2) for e in range(N_EXPERTS)]
    rel = _spread_slots(pos_ref[...], [t * LANES for t in tiles], width)
    lane_f = lax.broadcasted_iota(jnp.int32, (CMB_TOK, LANES), 1).astype(F32)
    tok_f = (b * CMB_TOK + lax.broadcasted_iota(jnp.int32, (CMB_TOK, LANES), 0)).astype(F32)
    for e in range(N_EXPERTS):
        for j in range(2):
            piece = rel[:, e * width + j * LANES:e * width + (j + 1) * LANES]
            row = jnp.sum(jnp.where(piece == lane_f + float(j * LANES), tok_f, 0.0), axis=0, keepdims=True)
            idx_ref[e, pl.ds(tiles[e] + j, 1), :] += row

    for e in range(N_EXPERTS):
        first = excl_ref[e * nblk + b]
        tile0 = first // LANES
        need_end = first + cnt_ref[e * nblk + b]
        n_extra = jnp.maximum((need_end - tile0 * LANES + width - 1) // width - 1, 0)

        def extra(k, carry):
            window(e, tile0 + 2 * (k + 1))
            return carry

        lax.fori_loop(0, n_extra, extra, 0)


def _compact(slots, excl, cnt, cap):
    E, T = slots.shape
    nblk = T // CMB_TOK
    grid_spec = pltpu.PrefetchScalarGridSpec(
        num_scalar_prefetch=2,
        grid=(nblk,),
        in_specs=[pl.BlockSpec((E, CMB_TOK), lambda b, *_: (0, b))],
        out_specs=pl.BlockSpec((E, cap // LANES, LANES), lambda b, *_: (0, 0, 0)),
        scratch_shapes=[pltpu.VMEM((CMB_TOK, LANES), jnp.int32)],
    )
    idx = pl.pallas_call(
        functools.partial(_compact_kernel, cap=cap, nblk=nblk),
        grid_spec=grid_spec,
        out_shape=jax.ShapeDtypeStruct((E, cap // LANES, LANES), F32),
        compiler_params=_cparams(("arbitrary",)),
        name="expert_lists",
    )(excl, cnt, slots)
    return idx.reshape(E, cap).astype(jnp.int32)


SC_CORES = 2
SC_SUBCORES = 16
SC_ROWS = 64


def _sc_gather(table, idx):
    n, width = idx.shape[0], table.shape[1]
    workers = SC_CORES * SC_SUBCORES
    per_worker = n // workers
    mesh = plsc.VectorSubcoreMesh(core_axis_name="c", subcore_axis_name="s")

    @functools.partial(
        pl.kernel, mesh=mesh,
        out_type=jax.ShapeDtypeStruct((n, width), table.dtype),
        scratch_types=[pltpu.VMEM((SC_ROWS,), jnp.int32), pltpu.VMEM((SC_ROWS, width), table.dtype),
                       pltpu.SemaphoreType.DMA],
    )
    def gather(table_hbm, idx_hbm, out_hbm, idx_v, rows_v, sem):
        base = (lax.axis_index("s") * SC_CORES + lax.axis_index("c")) * per_worker

        @pl.loop(0, per_worker // SC_ROWS)
        def _(i):
            off = pl.multiple_of(base + i * SC_ROWS, SC_ROWS)
            pltpu.sync_copy(idx_hbm.at[pl.ds(off, SC_ROWS)], idx_v)
            pltpu.async_copy(table_hbm.at[idx_v], rows_v, sem).wait()
            pltpu.sync_copy(rows_v, out_hbm.at[pl.ds(off, SC_ROWS)])

    return gather(table, idx)


FF_CHUNK = 512


def _ffn_kernel(xin_ref, wg_ref, wu_ref, wd_ref, y_ref):
    xin = _unpack_bf16_pairs(xin_ref[:, :D_MODEL // 2])
    affs = pltpu.bitcast(xin_ref[:, D_MODEL // 2:], F32)
    lane = lax.broadcasted_iota(jnp.int32, affs.shape, 1)
    gate = jnp.sum(jnp.where(lane == pl.program_id(0), affs, 0.0), axis=-1, keepdims=True)
    acc = jnp.zeros(y_ref.shape, F32)
    for f in range(D_FF // FF_CHUNK):
        fs = slice(f * FF_CHUNK, (f + 1) * FF_CHUNK)
        a = jnp.dot(xin, wg_ref[:, fs], preferred_element_type=F32)
        u = jnp.dot(xin, wu_ref[:, fs], preferred_element_type=F32)
        hmid = (a * (1.0 / (1.0 + jnp.exp(-a))) * u).astype(BF16)
        acc = acc + jnp.dot(hmid, wd_ref[fs, :], preferred_element_type=F32)
    y_ref[...] = (acc * gate).astype(y_ref.dtype)


def _ffn(xin, wg_bf, wu_bf, wd_bf, tm):
    E, cap, _ = xin.shape
    return pl.pallas_call(
        _ffn_kernel,
        grid=(E, cap // tm),
        in_specs=[pl.BlockSpec((None, tm, ROW_WORDS), lambda e, j: (e, j, 0)),
                  pl.BlockSpec((None, D_MODEL, D_FF), lambda e, j: (e, 0, 0)),
                  pl.BlockSpec((None, D_MODEL, D_FF), lambda e, j: (e, 0, 0)),
                  pl.BlockSpec((None, D_FF, D_MODEL), lambda e, j: (e, 0, 0))],
        out_specs=pl.BlockSpec((None, tm, D_MODEL), lambda e, j: (e, j, 0)),
        out_shape=jax.ShapeDtypeStruct((E, cap, D_MODEL), BF16),
        compiler_params=_cparams(("parallel", "arbitrary")),
        name="expert_ffn",
    )(xin, wg_bf, wu_bf, wd_bf)


CMB_ROWS = 128
BF16_SUBLANES = 16


def _combine_kernel(excl_ref, cnt_ref, x1_ref, slots_ref, y_hbm, o_ref, ybuf, xbuf, sem, xsem, pos_ref, *, cap, nblk):
    b = pl.program_id(0)
    slot = b % 2
    lane = lax.broadcasted_iota(jnp.int32, (CMB_TOK, CMB_ROWS), 1)

    def first_row(e, blk):
        return (excl_ref[e * nblk + blk] // BF16_SUBLANES) * BF16_SUBLANES

    def fetch(e, row, dst, s):
        row = pl.multiple_of(jnp.minimum(row, cap - CMB_ROWS), BF16_SUBLANES)
        return pltpu.make_async_copy(y_hbm.at[e, pl.ds(row, CMB_ROWS), :], dst, s)

    def start_block(blk, sl):
        for e in range(N_EXPERTS):
            fetch(e, first_row(e, blk), ybuf.at[sl, e], sem.at[sl, e]).start()

    @pl.when(b == 0)
    def _():
        start_block(0, 0)

    @pl.when(b + 1 < nblk)
    def _():
        start_block(b + 1, 1 - slot)

    def onehot(pos_col, nominal):
        hit = ((pos_col - jnp.minimum(nominal, cap - CMB_ROWS)) == lane) & (pos_col >= nominal)
        return jnp.where(hit, 1.0, 0.0).astype(BF16)

    pos_ref[...] = _token_major(slots_ref[...])
    bases = []
    for e in range(N_EXPERTS):
        nominal0 = first_row(e, b)
        fetch(e, nominal0, ybuf.at[slot, e], sem.at[slot, e]).wait()
        bases.append(jnp.minimum(nominal0, cap - CMB_ROWS))
    rel = _spread_slots(pos_ref[...], bases, CMB_ROWS)
    lane_f = lane.astype(F32)
    parts = [jnp.where(rel[:, e * CMB_ROWS:(e + 1) * CMB_ROWS] == lane_f, 1.0, 0.0).astype(BF16)
             for e in range(N_EXPERTS)]
    rows = ybuf[slot].reshape(N_EXPERTS * CMB_ROWS, D_MODEL)
    o_ref[...] = x1_ref[...] + jnp.dot(jnp.concatenate(parts, axis=1), rows, preferred_element_type=F32)

    for e in range(N_EXPERTS):
        nominal0 = first_row(e, b)
        need_end = excl_ref[e * nblk + b] + cnt_ref[e * nblk + b]
        n_extra = jnp.maximum((need_end - nominal0 + CMB_ROWS - 1) // CMB_ROWS - 1, 0)

        def extra(k, carry):
            nominal = nominal0 + (k + 1) * CMB_ROWS
            cp = fetch(e, nominal, xbuf, xsem)
            cp.start()
            cp.wait()
            o_ref[...] += jnp.dot(onehot(pos_ref[:, e:e + 1], nominal), xbuf[...], preferred_element_type=F32)
            return carry

        lax.fori_loop(0, n_extra, extra, 0)


def _combine(x1, slots, y, excl, cnt):
    T = x1.shape[0]
    E, cap, _ = y.shape
    nblk = T // CMB_TOK
    grid_spec = pltpu.PrefetchScalarGridSpec(
        num_scalar_prefetch=2,
        grid=(nblk,),
        in_specs=[pl.BlockSpec((CMB_TOK, D_MODEL), lambda b, *_: (b, 0)),
                  pl.BlockSpec((E, CMB_TOK), lambda b, *_: (0, b)),
                  pl.BlockSpec(memory_space=pl.ANY)],
        out_specs=pl.BlockSpec((CMB_TOK, D_MODEL), lambda b, *_: (b, 0)),
        scratch_shapes=[pltpu.VMEM((2, E, CMB_ROWS, D_MODEL), BF16),
                        pltpu.VMEM((CMB_ROWS, D_MODEL), BF16),
                        pltpu.SemaphoreType.DMA((2, E)),
                        pltpu.SemaphoreType.DMA(()),
                        pltpu.VMEM((CMB_TOK, LANES), jnp.int32)],
    )
    return pl.pallas_call(
        functools.partial(_combine_kernel, cap=cap, nblk=nblk),
        grid_spec=grid_spec,
        out_shape=jax.ShapeDtypeStruct((T, D_MODEL), F32),
        compiler_params=_cparams(("arbitrary",)),
        name="moe_combine",
    )(excl, cnt, x1, slots, y)


def _layer(x, p):
    B, S, _ = x.shape
    T = B * S
    tm = 512
    x2d = x.reshape(T, D_MODEL)
    qkv_att, qkv_ret = _inproj(x2d, S, p["g1"], p["w_in"], p["gq"], p["gk"], _rope_tables(S), tm)
    att = _attention(qkv_att.reshape(N_ATT_SLABS, B, S, LANES), B, S).reshape(T, ATT_W)
    ret = _retention(qkv_ret.reshape(N_SLABS - N_ATT_SLABS, B, S, LANES), p["decays"], p["gn"], B, S).reshape(T, RET_W)
    x1, h2, aff_t = _outproj(x2d, att, ret, p["w_out"], p["g2"], p["w_router_t"], tm)

    cap = CAPACITY_FACTOR * T // N_EXPERTS
    sel = _select(aff_t, cap)

    csum = jnp.cumsum(sel, axis=1)
    slots = jnp.where(sel > 0, csum - 1, -1)
    ends = csum[:, CMB_TOK - 1::CMB_TOK]
    excl = jnp.concatenate([jnp.zeros((N_EXPERTS, 1), jnp.int32), ends[:, :-1]], axis=1).reshape(-1)
    cnt = ends.reshape(-1) - excl

    idx = _compact(slots, excl, cnt, cap)
    xin = _sc_gather(h2, idx.reshape(-1)).reshape(N_EXPERTS, cap, ROW_WORDS)
    y = _ffn(xin, p["w_gate"], p["w_up"], p["w_down"], tm)
    out = _combine(x1, slots, y, excl, cnt)
    return out.reshape(B, S, D_MODEL)


def kernel(x_prompt, x_sample, norm1_g, w_in, attn_qnorm_g, attn_knorm_g, ret_decay_fwd, ret_decay_bwd,
           ret_norm_g, w_out, norm2_g, w_router, w_gate_e, w_up_e, w_down_e):
    y_prompt, y_sample = x_prompt, x_sample
    for l in range(norm1_g.shape[0]):
        p = {
            "g1": norm1_g[l][None, :],
            "w_in": w_in[l].astype(BF16),
            "gq": jnp.tile(attn_qnorm_g[l], LANES // ATT_HD)[None, :],
            "gk": jnp.tile(attn_knorm_g[l], LANES // ATT_HD)[None, :],
            "decays": jnp.stack([ret_decay_fwd[l], ret_decay_bwd[l]]).astype(F32),
            "gn": ret_norm_g[l][None, :].astype(F32),
            "w_out": w_out[l].astype(BF16),
            "g2": norm2_g[l][None, :],
            "w_router_t": w_router[l].T.astype(BF16),
            "w_gate": w_gate_e[l].astype(BF16),
            "w_up": w_up_e[l].astype(BF16),
            "w_down": w_down_e[l].astype(BF16),
        }
        y_prompt = _layer(y_prompt, p)
        y_sample = _layer(y_sample, p)
    return (y_prompt, y_sample)
```

```python
import functools

import jax
import jax.numpy as jnp
from jax import lax
from jax.experimental import pallas as pl
from jax.experimental.pallas import tpu as pltpu
from jax.experimental.pallas import tpu_sc as plsc

F32 = jnp.float32
BF16 = jnp.bfloat16

D_MODEL = 1024
ATT_HEADS, ATT_HD = 8, 64
RET_HEADS, RET_HD = 4, 128
ATT_W = ATT_HEADS * ATT_HD
RET_W = RET_HEADS * RET_HD
IN_W = 3 * ATT_W + 4 * RET_W
LANES = 128
N_SLABS = IN_W // LANES
GROUP_W = 512
SLABS_PER_GROUP = GROUP_W // LANES
N_ATT_SLABS = 3 * ATT_W // LANES
RET_CHUNK = 256
ROPE_THETA = 10000.0
N_EXPERTS = 16
D_FF = 2048
CAPACITY_FACTOR = 2
NORM_EPS = 1e-6
NEG_INF = -1e30
ATT_QBLK = 128
VMEM_LIMIT = 56 * 1024 * 1024


def _cparams(sem):
    return pltpu.CompilerParams(dimension_semantics=sem, vmem_limit_bytes=VMEM_LIMIT)


def _inproj_kernel(x_ref, g1_ref, w_ref, gq_ref, gk_ref, ca_ref, sa_ref, cr_ref, sr_ref, oa_ref, or_ref):
    x = x_ref[...]
    ms = jnp.mean(x * x, axis=-1, keepdims=True)
    h = (x * lax.rsqrt(ms + NORM_EPS) * g1_ref[...]).astype(BF16)
    tm = x.shape[0]
    lane = lax.broadcasted_iota(jnp.int32, (tm, LANES), 1)
    first = lane < ATT_HD
    low_half = (lane & (ATT_HD // 2)) == 0
    ca, sa, cr, sr = ca_ref[...], sa_ref[...], cr_ref[...], sr_ref[...]
    for grp in range(IN_W // GROUP_W):
        acc = jnp.dot(h, w_ref[:, grp * GROUP_W:(grp + 1) * GROUP_W], preferred_element_type=F32)
        for p in range(SLABS_PER_GROUP):
            a = acc[:, p * LANES:(p + 1) * LANES]
            if grp in (0, 1):
                sq = a * a
                s0 = jnp.sum(jnp.where(first, sq, 0.0), axis=-1, keepdims=True)
                s1 = jnp.sum(jnp.where(first, 0.0, sq), axis=-1, keepdims=True)
                ms2 = jnp.where(first, s0, s1) * (1.0 / ATT_HD)
                g = gq_ref[...] if grp == 0 else gk_ref[...]
                y = a * lax.rsqrt(ms2 + NORM_EPS) * g
                partner = jnp.where(low_half, pltpu.roll(y, LANES - ATT_HD // 2, 1),
                                    pltpu.roll(y, ATT_HD // 2, 1))
                r = y * ca + partner * sa
                if grp == 0:
                    r = r * (ATT_HD ** -0.5)
            elif grp in (3, 4):
                r = a * cr + pltpu.roll(a, RET_HD // 2, 1) * sr
                if grp == 4:
                    r = r * (RET_HD ** -0.5)
            else:
                r = a
            slab = grp * SLABS_PER_GROUP + p
            if slab < N_ATT_SLABS:
                oa_ref[slab] = r
            else:
                or_ref[slab - N_ATT_SLABS] = r.astype(BF16)


def _rope_tables(S):
    pos = jnp.arange(S, dtype=F32)

    def table(hd):
        inv_freq = ROPE_THETA ** (-jnp.arange(0, hd, 2, dtype=F32) / hd)
        ang = pos[:, None] * inv_freq[None, :]
        cos, sin = jnp.cos(ang), jnp.sin(ang)
        reps = LANES // hd
        cos_t = jnp.tile(jnp.concatenate([cos, cos], axis=-1), (1, reps))
        sin_t = jnp.tile(jnp.concatenate([-sin, sin], axis=-1), (1, reps))
        return cos_t, sin_t

    ca, sa = table(ATT_HD)
    cr, sr = table(RET_HD)
    return ca, sa, cr, sr


def _inproj(x2d, S, g1, w_in_bf, gq, gk, tables, tm):
    T = x2d.shape[0]
    n_pos_blk = S // tm
    tab_spec = pl.BlockSpec((tm, LANES), lambda i: (i % n_pos_blk, 0))
    full = lambda shape: pl.BlockSpec(shape, lambda i: (0,) * len(shape))
    return pl.pallas_call(
        _inproj_kernel,
        grid=(T // tm,),
        in_specs=[pl.BlockSpec((tm, D_MODEL), lambda i: (i, 0)), full((1, D_MODEL)),
                  full((D_MODEL, IN_W)), full((1, LANES)), full((1, LANES)),
                  tab_spec, tab_spec, tab_spec, tab_spec],
        out_specs=[pl.BlockSpec((N_ATT_SLABS, tm, LANES), lambda i: (0, i, 0)),
                   pl.BlockSpec((N_SLABS - N_ATT_SLABS, tm, LANES), lambda i: (0, i, 0))],
        out_shape=[jax.ShapeDtypeStruct((N_ATT_SLABS, T, LANES), F32),
                   jax.ShapeDtypeStruct((N_SLABS - N_ATT_SLABS, T, LANES), BF16)],
        compiler_params=_cparams(("parallel",)),
        name="inproj",
    )(x2d, g1, w_in_bf, gq, gk, *tables)


DILATED_PATTERNS = ((128, 1), (512, 4), (2048, 16))
ATT_HALF = 64
ATT_SB = 2048
ATT_NB = ATT_SB // ATT_QBLK
ATT_UNROLL = 2


def _attn_kernel(q_ref, k_ref, v_ref, o_ref, o_scr, lse_scr, bias_scr, *, S):
    t0 = pl.program_id(2) * ATT_SB
    lane = lax.broadcasted_iota(jnp.int32, (ATT_QBLK, LANES), 1)
    first = lane < ATT_HD
    nt_dims = (((1,), (1,)), ((), ()))

    rel = (lax.broadcasted_iota(jnp.int32, (ATT_QBLK, 2 * ATT_QBLK), 0)
           - lax.broadcasted_iota(jnp.int32, (ATT_QBLK, 2 * ATT_QBLK), 1))
    for oi in range(3):
        bias_scr[oi] = jnp.where(jnp.abs(rel + oi * ATT_HALF) <= ATT_HALF, 0.0, NEG_INF)

    def band(qv, kv, vv, off):
        kw = kv.shape[0]
        bias = bias_scr[off // ATT_HALF][:, :kw]
        zero = jnp.zeros_like(qv)
        q2 = jnp.concatenate([jnp.where(first, qv, zero), jnp.where(first, zero, qv)], axis=0)
        s = lax.dot_general(q2, kv, nt_dims, preferred_element_type=F32) + jnp.concatenate([bias, bias], axis=0)
        m = jnp.max(s, axis=-1, keepdims=True)
        p = jnp.exp(s - m)
        l = jnp.sum(p, axis=-1, keepdims=True)
        acc = jnp.dot(p.astype(BF16), vv, preferred_element_type=F32) * (1.0 / l)
        lse = m + jnp.log(l)
        return (jnp.where(first, acc[:ATT_QBLK], acc[ATT_QBLK:]),
                jnp.where(first, lse[:ATT_QBLK], lse[ATT_QBLK:]))

    def step(n, carry):
        for pi, (window, dil) in enumerate(DILATED_PATTERNS):
            L = S // dil
            kw = min(2 * ATT_QBLK, L)
            per_class = ATT_SB // dil // ATT_QBLK
            r = n // per_class
            bi = n % per_class
            lq0 = t0 // dil + bi * ATT_QBLK
            lk0 = jnp.clip(lq0 - ATT_HALF, 0, L - kw)
            qrow = dil * ATT_QBLK * bi + r
            krow = dil * lk0 + r
            if dil == 1:
                qs = pl.ds(pl.multiple_of(qrow, ATT_QBLK), ATT_QBLK)
                ks = pl.ds(pl.multiple_of(krow, 8), kw)
            else:
                qs = pl.ds(qrow, ATT_QBLK, stride=dil)
                ks = pl.ds(krow, kw, stride=dil)
            o, lse = band(q_ref[qs, :].astype(BF16), k_ref[ks, :].astype(BF16), v_ref[ks, :].astype(BF16),
                          lq0 - lk0)
            o_scr[pi, qs, :] = o
            lse_scr[pi, qs, :] = lse
        return carry

    lax.fori_loop(0, ATT_NB, step, 0, unroll=ATT_UNROLL)

    def merge(c, carry):
        sl = pl.ds(pl.multiple_of(c * ATT_QBLK, ATT_QBLK), ATT_QBLK)
        lses = [lse_scr[pi, sl, :] for pi in range(3)]
        mx = jnp.maximum(jnp.maximum(lses[0], lses[1]), lses[2])
        ws = [jnp.exp(x - mx) for x in lses]
        num = ws[0] * o_scr[0, sl, :] + ws[1] * o_scr[1, sl, :] + ws[2] * o_scr[2, sl, :]
        o_ref[sl, :] = (num / (ws[0] + ws[1] + ws[2])).astype(o_ref.dtype)
        return carry

    lax.fori_loop(0, ATT_NB, merge, 0)


def _attention(qkv4, B, S):
    n_pairs = ATT_W // LANES
    kv_spec = lambda off: pl.BlockSpec((None, None, S, LANES), lambda b, p, i: (off + p, b, 0, 0))
    return pl.pallas_call(
        functools.partial(_attn_kernel, S=S),
        grid=(B, n_pairs, S // ATT_SB),
        in_specs=[pl.BlockSpec((None, None, ATT_SB, LANES), lambda b, p, i: (p, b, i, 0)),
                  kv_spec(n_pairs), kv_spec(2 * n_pairs)],
        out_specs=pl.BlockSpec((None, ATT_SB, LANES), lambda b, p, i: (b, i, p)),
        out_shape=jax.ShapeDtypeStruct((B, S, ATT_W), BF16),
        scratch_shapes=[pltpu.VMEM((3, ATT_SB, LANES), F32), pltpu.VMEM((3, ATT_SB, LANES), F32),
                        pltpu.VMEM((3, ATT_QBLK, 2 * ATT_QBLK), F32)],
        compiler_params=_cparams(("parallel", "parallel", "arbitrary")),
        name="dilated_attn",
    )(qkv4, qkv4, qkv4)


def _ret_kernel(dec_ref, q_ref, k_ref, v_ref, g_ref, gn_ref, o_ref, of_scr, ob_scr, *, S):
    C = RET_CHUNK
    nc = S // C
    h = pl.program_id(1)
    nt_dims = (((1,), (1,)), ((), ()))
    tn_dims = (((0,), (0,)), ((), ()))

    def consts(direction):
        def log_g(shape):
            return -jnp.exp(jnp.full(shape, dec_ref[direction, h], F32))

        row = lax.broadcasted_iota(jnp.int32, (C, C), 0).astype(F32)
        col = lax.broadcasted_iota(jnp.int32, (C, C), 1).astype(F32)
        n = lax.broadcasted_iota(jnp.int32, (C, RET_HD), 0).astype(F32)
        lg = log_g((C, RET_HD))
        if direction == 0:
            diff = row - col
            mask = diff >= 0.0
            k_dec, q_dec = jnp.exp(lg * (C - 1.0 - n)), jnp.exp(lg * (n + 1.0))
        else:
            diff = col - row
            mask = diff > 0.0
            k_dec, q_dec = jnp.exp(lg * n), jnp.exp(lg * (C - n))
        dmat = jnp.where(mask, jnp.exp(log_g((C, C)) * jnp.maximum(diff, 0.0)), 0.0)
        return dmat, k_dec, q_dec, jnp.exp(log_g((RET_HD, RET_HD)) * C)

    def chunk_out(c, state, dmat, k_dec, q_dec, g_chunk):
        sl = pl.ds(pl.multiple_of(c * C, C), C)
        qc, kc, vc = q_ref[sl, :], k_ref[sl, :], v_ref[sl, :]
        qk = lax.dot_general(qc, kc, nt_dims, preferred_element_type=F32)
        inner = (qk * dmat).astype(BF16)
        o = jnp.dot(inner, vc, preferred_element_type=F32)
        o = o + jnp.dot((qc.astype(F32) * q_dec).astype(BF16), state.astype(BF16),
                        preferred_element_type=F32)
        kd = (kc.astype(F32) * k_dec).astype(BF16)
        kv = lax.dot_general(kd, vc, tn_dims, preferred_element_type=F32)
        return sl, o, state * g_chunk + kv

    cf, cb = consts(0), consts(1)

    def scan(i, states):
        sl_f, o_f, st_f = chunk_out(i, states[0], *cf)
        of_scr[sl_f, :] = o_f
        sl_b, o_b, st_b = chunk_out(nc - 1 - i, states[1], *cb)
        ob_scr[sl_b, :] = o_b
        return st_f, st_b

    zero = jnp.zeros((RET_HD, RET_HD), F32)
    lax.fori_loop(0, nc, scan, (zero, zero), unroll=2)

    gn = gn_ref[...]

    def finish(c, carry):
        sl = pl.ds(pl.multiple_of(c * C, C), C)
        o = of_scr[sl, :] + ob_scr[sl, :]
        mu = jnp.mean(o, axis=-1, keepdims=True)
        var = jnp.mean(jnp.square(o - mu), axis=-1, keepdims=True)
        y = (o - mu) * lax.rsqrt(var + NORM_EPS) * gn
        g = g_ref[sl, :].astype(F32)
        o_ref[sl, :] = (y * (g * (1.0 / (1.0 + jnp.exp(-g))))).astype(o_ref.dtype)
        return carry

    lax.fori_loop(0, nc, finish, 0, unroll=2)


def _retention(qkv4, decays, gn, B, S):
    spec = lambda off: pl.BlockSpec((None, None, S, LANES), lambda b, h: (off + h, b, 0, 0))
    return pl.pallas_call(
        functools.partial(_ret_kernel, S=S),
        grid=(B, RET_HEADS),
        in_specs=[pl.BlockSpec(memory_space=pltpu.SMEM),
                  spec(0), spec(RET_HEADS), spec(2 * RET_HEADS), spec(3 * RET_HEADS),
                  pl.BlockSpec((1, LANES), lambda b, h: (0, h))],
        out_specs=pl.BlockSpec((None, S, LANES), lambda b, h: (b, 0, h)),
        out_shape=jax.ShapeDtypeStruct((B, S, RET_W), BF16),
        scratch_shapes=[pltpu.VMEM((S, LANES), F32), pltpu.VMEM((S, LANES), F32)],
        compiler_params=_cparams(("parallel", "arbitrary")),
        name="retention",
    )(decays, qkv4, qkv4, qkv4, qkv4, gn)


HI16 = -65536
ROW_WORDS = D_MODEL // 2 + LANES


def _pack_bf16_pairs(x):
    n = x.shape[1] // 2
    bits = pltpu.bitcast(x.astype(BF16).astype(F32), jnp.int32)
    return (bits[:, n:] & HI16) | lax.shift_right_logical(bits[:, :n], 16)


def _unpack_bf16_pairs(w):
    lo = pltpu.bitcast(lax.shift_left(w, 16), F32).astype(BF16)
    hi = pltpu.bitcast(w & HI16, F32).astype(BF16)
    return jnp.concatenate([lo, hi], axis=1)


def _outproj_kernel(x_ref, att_ref, ret_ref, wo_ref, g2_ref, wr_ref, x1_ref, h2_ref, aff_ref):
    y = jnp.dot(att_ref[...], wo_ref[:ATT_W, :], preferred_element_type=F32)
    y = y + jnp.dot(ret_ref[...], wo_ref[ATT_W:, :], preferred_element_type=F32)
    x1 = x_ref[...] + y
    x1_ref[...] = x1
    ms = jnp.mean(x1 * x1, axis=-1, keepdims=True)
    h2 = x1 * lax.rsqrt(ms + NORM_EPS) * g2_ref[...]
    h2_ref[:, :D_MODEL // 2] = _pack_bf16_pairs(h2)
    logits = lax.dot_general(wr_ref[...], h2.astype(BF16), (((1,), (1,)), ((), ())),
                             preferred_element_type=F32)
    e = jnp.exp(logits - jnp.max(logits, axis=0, keepdims=True))
    aff = e / jnp.sum(e, axis=0, keepdims=True)
    aff_ref[...] = aff
    pad = jnp.zeros((LANES - N_EXPERTS, aff.shape[1]), F32)
    h2_ref[:, D_MODEL // 2:] = pltpu.bitcast(jnp.transpose(jnp.concatenate([aff, pad], axis=0)), jnp.int32)


def _outproj(x2d, att, ret, wo_bf, g2, wr_t_bf, tm):
    T = x2d.shape[0]
    full = lambda shape: pl.BlockSpec(shape, lambda i: (0,) * len(shape))
    return pl.pallas_call(
        _outproj_kernel,
        grid=(T // tm,),
        in_specs=[pl.BlockSpec((tm, D_MODEL), lambda i: (i, 0)),
                  pl.BlockSpec((tm, ATT_W), lambda i: (i, 0)),
                  pl.BlockSpec((tm, RET_W), lambda i: (i, 0)),
                  full((ATT_W + RET_W, D_MODEL)), full((1, D_MODEL)), full((N_EXPERTS, D_MODEL))],
        out_specs=[pl.BlockSpec((tm, D_MODEL), lambda i: (i, 0)),
                   pl.BlockSpec((tm, ROW_WORDS), lambda i: (i, 0)),
                   pl.BlockSpec((N_EXPERTS, tm), lambda i: (0, i))],
        out_shape=[jax.ShapeDtypeStruct((T, D_MODEL), F32),
                   jax.ShapeDtypeStruct((T, ROW_WORDS), jnp.int32),
                   jax.ShapeDtypeStruct((N_EXPERTS, T), F32)],
        compiler_params=_cparams(("parallel",)),
        name="outproj_router",
    )(x2d, att, ret, wo_bf, g2, wr_t_bf)


def _select_kernel(aff_ref, sel_ref, *, cap):
    aff = aff_ref[...]
    E, T = aff.shape
    capf = jnp.float32(cap)

    def count(mask):
        return jnp.sum(jnp.where(mask, 1.0, 0.0), axis=1, keepdims=True)

    def thr_step(i, thr_bits):
        cand = thr_bits | jnp.left_shift(jnp.int32(1), 30 - i)
        return jnp.where(count(aff >= pltpu.bitcast(cand, F32)) >= capf, cand, thr_bits)

    thr = pltpu.bitcast(lax.fori_loop(0, 31, thr_step, jnp.zeros((E, 1), jnp.int32)), F32)
    above = aff > thr
    ties = aff == thr
    need = capf - count(above)
    idx = lax.broadcasted_iota(jnp.int32, (E, T), 1)
    n_idx_bits = max(1, (T - 1).bit_length())

    def cut_step(i, cut):
        cand = cut | jnp.left_shift(jnp.int32(1), n_idx_bits - 1 - i)
        return jnp.where(count(ties & (idx < cand)) < need, cand, cut)

    cut = lax.fori_loop(0, n_idx_bits, cut_step, jnp.zeros((E, 1), jnp.int32))
    sel_ref[...] = jnp.where(above | (ties & (idx <= cut)), 1, 0).astype(jnp.int32)


def _select(aff_t, cap):
    E, T = aff_t.shape
    return pl.pallas_call(
        functools.partial(_select_kernel, cap=cap),
        grid=(1,),
        in_specs=[pl.BlockSpec((E, T), lambda i: (0, 0))],
        out_specs=pl.BlockSpec((E, T), lambda i: (0, 0)),
        out_shape=jax.ShapeDtypeStruct((E, T), jnp.int32),
        compiler_params=_cparams(("arbitrary",)),
        name="expert_select",
    )(aff_t)


CMB_TOK = 512


def _token_major(slots):
    n_e, n_tok = slots.shape
    padded = jnp.concatenate([slots, jnp.full((LANES - n_e, n_tok), -1, jnp.int32)], axis=0)
    return pltpu.bitcast(jnp.transpose(pltpu.bitcast(padded, F32)), jnp.int32)


def _spread_slots(pos, bases, width):
    n_e, n_col = len(bases), pos.shape[1]
    col = lax.broadcasted_iota(jnp.int32, pos.shape, 1)
    base_arr = jnp.zeros(pos.shape, jnp.int32)
    for e in range(n_e):
        base_arr = jnp.where(col == e, bases[e], base_arr)
    rel = pos - base_arr
    rel = jnp.where((rel >= 0) & (rel < width), rel, -1).astype(F32).astype(BF16)
    expander = (lax.broadcasted_iota(jnp.int32, (n_col, n_e * width), 0)
                == lax.broadcasted_iota(jnp.int32, (n_col, n_e * width), 1) // width)
    return jnp.dot(rel, jnp.where(expander, 1.0, 0.0).astype(BF16), preferred_element_type=F32)


def _compact_kernel(excl_ref, cnt_ref, slots_ref, idx_ref, pos_ref, *, cap, nblk):
    b = pl.program_id(0)
    ntiles = cap // LANES
    width = 2 * LANES

    @pl.when(b == 0)
    def _():
        idx_ref[...] = jnp.zeros(idx_ref.shape, F32)

    pos_ref[...] = _token_major(slots_ref[...])

    lane = lax.broadcasted_iota(jnp.int32, (CMB_TOK, width), 1)
    tok = (b * CMB_TOK + lax.broadcasted_iota(jnp.int32, (CMB_TOK, width), 0)).astype(F32)

    def window(e, tile_nominal):
        tile = jnp.minimum(tile_nominal, ntiles - 2)
        pos_col = pos_ref[:, e:e + 1]
        hit = ((pos_col - tile * LANES) == lane) & (pos_col >= tile_nominal * LANES)
        row = jnp.sum(jnp.where(hit, tok, 0.0), axis=0, keepdims=True)
        for j in range(2):
            idx_ref[e, pl.ds(tile + j, 1), :] += row[:, j * LANES:(j + 1) * LANES]

    tiles = [jnp.minimum(excl_ref[e * nblk + b] // LANES, ntiles - 2) for e in range(N_EXPERTS)]
    rel = _spread_slots(pos_ref[...], [t * LANES for t in tiles], width)
    lane_f = lax.broadcasted_iota(jnp.int32, (CMB_TOK, LANES), 1).astype(F32)
    tok_f = (b * CMB_TOK + lax.broadcasted_iota(jnp.int32, (CMB_TOK, LANES), 0)).astype(F32)
    for e in range(N_EXPERTS):
        for j in range(2):
            piece = rel[:, e * width + j * LANES:e * width + (j + 1) * LANES]
            row = jnp.sum(jnp.where(piece == lane_f + float(j * LANES), tok_f, 0.0), axis=0, keepdims=True)
            idx_ref[e, pl.ds(tiles[e] + j, 1), :] += row

    for e in range(N_EXPERTS):
        first = excl_ref[e * nblk + b]
        tile0 = first // LANES
        need_end = first + cnt_ref[e * nblk + b]
        n_extra = jnp.maximum((need_end - tile0 * LANES + width - 1) // width - 1, 0)

        def extra(k, carry):
            window(e, tile0 + 2 * (k + 1))
            return carry

        lax.fori_loop(0, n_extra, extra, 0)


def _compact(slots, excl, cnt, cap):
    E, T = slots.shape
    nblk = T // CMB_TOK
    grid_spec = pltpu.PrefetchScalarGridSpec(
        num_scalar_prefetch=2,
        grid=(nblk,),
        in_specs=[pl.BlockSpec((E, CMB_TOK), lambda b, *_: (0, b))],
        out_specs=pl.BlockSpec((E, cap // LANES, LANES), lambda b, *_: (0, 0, 0)),
        scratch_shapes=[pltpu.VMEM((CMB_TOK, LANES), jnp.int32)],
    )
    idx = pl.pallas_call(
        functools.partial(_compact_kernel, cap=cap, nblk=nblk),
        grid_spec=grid_spec,
        out_shape=jax.ShapeDtypeStruct((E, cap // LANES, LANES), F32),
        compiler_params=_cparams(("arbitrary",)),
        name="expert_lists",
    )(excl, cnt, slots)
    return idx.reshape(E, cap).astype(jnp.int32)


SC_CORES = 2
SC_SUBCORES = 16
SC_ROWS = 128


def _sc_gather(table, idx):
    n, width = idx.shape[0], table.shape[1]
    workers = SC_CORES * SC_SUBCORES
    per_worker = n // workers
    mesh = plsc.VectorSubcoreMesh(core_axis_name="c", subcore_axis_name="s")

    @functools.partial(
        pl.kernel, mesh=mesh,
        out_type=jax.ShapeDtypeStruct((n, width), table.dtype),
        scratch_types=[pltpu.VMEM((SC_ROWS,), jnp.int32), pltpu.VMEM((SC_ROWS, width), table.dtype),
                       pltpu.SemaphoreType.DMA],
    )
    def gather(table_hbm, idx_hbm, out_hbm, idx_v, rows_v, sem):
        base = (lax.axis_index("s") * SC_CORES + lax.axis_index("c")) * per_worker

        @pl.loop(0, per_worker // SC_ROWS)
        def _(i):
            off = pl.multiple_of(base + i * SC_ROWS, SC_ROWS)
            pltpu.sync_copy(idx_hbm.at[pl.ds(off, SC_ROWS)], idx_v)
            pltpu.async_copy(table_hbm.at[idx_v], rows_v, sem).wait()
            pltpu.sync_copy(rows_v, out_hbm.at[pl.ds(off, SC_ROWS)])

    return gather(table, idx)


FF_CHUNK = 512


def _ffn_kernel(xin_ref, wg_ref, wu_ref, wd_ref, y_ref):
    xin = _unpack_bf16_pairs(xin_ref[:, :D_MODEL // 2])
    affs = pltpu.bitcast(xin_ref[:, D_MODEL // 2:], F32)
    lane = lax.broadcasted_iota(jnp.int32, affs.shape, 1)
    gate = jnp.sum(jnp.where(lane == pl.program_id(0), affs, 0.0), axis=-1, keepdims=True)
    acc = jnp.zeros(y_ref.shape, F32)
    for f in range(D_FF // FF_CHUNK):
        fs = slice(f * FF_CHUNK, (f + 1) * FF_CHUNK)
        a = jnp.dot(xin, wg_ref[:, fs], preferred_element_type=F32)
        u = jnp.dot(xin, wu_ref[:, fs], preferred_element_type=F32)
        hmid = (a * (1.0 / (1.0 + jnp.exp(-a))) * u).astype(BF16)
        acc = acc + jnp.dot(hmid, wd_ref[fs, :], preferred_element_type=F32)
    y_ref[...] = (acc * gate).astype(y_ref.dtype)


def _ffn(xin, wg_bf, wu_bf, wd_bf, tm):
    E, cap, _ = xin.shape
    return pl.pallas_call(
        _ffn_kernel,
        grid=(E, cap // tm),
        in_specs=[pl.BlockSpec((None, tm, ROW_WORDS), lambda e, j: (e, j, 0)),
                  pl.BlockSpec((None, D_MODEL, D_FF), lambda e, j: (e, 0, 0)),
                  pl.BlockSpec((None, D_MODEL, D_FF), lambda e, j: (e, 0, 0)),
                  pl.BlockSpec((None, D_FF, D_MODEL), lambda e, j: (e, 0, 0))],
        out_specs=pl.BlockSpec((None, tm, D_MODEL), lambda e, j: (e, j, 0)),
        out_shape=jax.ShapeDtypeStruct((E, cap, D_MODEL), BF16),
        compiler_params=_cparams(("parallel", "arbitrary")),
        name="expert_ffn",
    )(xin, wg_bf, wu_bf, wd_bf)


CMB_ROWS = 128
BF16_SUBLANES = 16


def _combine_kernel(excl_ref, cnt_ref, x1_ref, slots_ref, y_hbm, o_ref, ybuf, xbuf, sem, xsem, pos_ref, *, cap, nblk):
    b = pl.program_id(0)
    slot = b % 2
    lane = lax.broadcasted_iota(jnp.int32, (CMB_TOK, CMB_ROWS), 1)

    def first_row(e, blk):
        return (excl_ref[e * nblk + blk] // BF16_SUBLANES) * BF16_SUBLANES

    def fetch(e, row, dst, s):
        row = pl.multiple_of(jnp.minimum(row, cap - CMB_ROWS), BF16_SUBLANES)
        return pltpu.make_async_copy(y_hbm.at[e, pl.ds(row, CMB_ROWS), :], dst, s)

    def start_block(blk, sl):
        for e in range(N_EXPERTS):
            fetch(e, first_row(e, blk), ybuf.at[sl, e], sem.at[sl, e]).start()

    @pl.when(b == 0)
    def _():
        start_block(0, 0)

    @pl.when(b + 1 < nblk)
    def _():
        start_block(b + 1, 1 - slot)

    def onehot(pos_col, nominal):
        hit = ((pos_col - jnp.minimum(nominal, cap - CMB_ROWS)) == lane) & (pos_col >= nominal)
        return jnp.where(hit, 1.0, 0.0).astype(BF16)

    pos_ref[...] = _token_major(slots_ref[...])
    bases = []
    for e in range(N_EXPERTS):
        nominal0 = first_row(e, b)
        fetch(e, nominal0, ybuf.at[slot, e], sem.at[slot, e]).wait()
        bases.append(jnp.minimum(nominal0, cap - CMB_ROWS))
    rel = _spread_slots(pos_ref[...], bases, CMB_ROWS)
    lane_f = lane.astype(F32)
    parts = [jnp.where(rel[:, e * CMB_ROWS:(e + 1) * CMB_ROWS] == lane_f, 1.0, 0.0).astype(BF16)
             for e in range(N_EXPERTS)]
    rows = ybuf[slot].reshape(N_EXPERTS * CMB_ROWS, D_MODEL)
    o_ref[...] = x1_ref[...] + jnp.dot(jnp.concatenate(parts, axis=1), rows, preferred_element_type=F32)

    for e in range(N_EXPERTS):
        nominal0 = first_row(e, b)
        need_end = excl_ref[e * nblk + b] + cnt_ref[e * nblk + b]
        n_extra = jnp.maximum((need_end - nominal0 + CMB_ROWS - 1) // CMB_ROWS - 1, 0)

        def extra(k, carry):
            nominal = nominal0 + (k + 1) * CMB_ROWS
            cp = fetch(e, nominal, xbuf, xsem)
            cp.start()
            cp.wait()
            o_ref[...] += jnp.dot(onehot(pos_ref[:, e:e + 1], nominal), xbuf[...], preferred_element_type=F32)
            return carry

        lax.fori_loop(0, n_extra, extra, 0)


def _combine(x1, slots, y, excl, cnt):
    T = x1.shape[0]
    E, cap, _ = y.shape
    nblk = T // CMB_TOK
    grid_spec = pltpu.PrefetchScalarGridSpec(
        num_scalar_prefetch=2,
        grid=(nblk,),
        in_specs=[pl.BlockSpec((CMB_TOK, D_MODEL), lambda b, *_: (b, 0)),
                  pl.BlockSpec((E, CMB_TOK), lambda b, *_: (0, b)),
                  pl.BlockSpec(memory_space=pl.ANY)],
        out_specs=pl.BlockSpec((CMB_TOK, D_MODEL), lambda b, *_: (b, 0)),
        scratch_shapes=[pltpu.VMEM((2, E, CMB_ROWS, D_MODEL), BF16),
                        pltpu.VMEM((CMB_ROWS, D_MODEL), BF16),
                        pltpu.SemaphoreType.DMA((2, E)),
                        pltpu.SemaphoreType.DMA(()),
                        pltpu.VMEM((CMB_TOK, LANES), jnp.int32)],
    )
    return pl.pallas_call(
        functools.partial(_combine_kernel, cap=cap, nblk=nblk),
        grid_spec=grid_spec,
        out_shape=jax.ShapeDtypeStruct((T, D_MODEL), F32),
        compiler_params=_cparams(("arbitrary",)),
        name="moe_combine",
    )(excl, cnt, x1, slots, y)


def _layer(x, p):
    B, S, _ = x.shape
    T = B * S
    tm = 512
    x2d = x.reshape(T, D_MODEL)
    qkv_att, qkv_ret = _inproj(x2d, S, p["g1"], p["w_in"], p["gq"], p["gk"], _rope_tables(S), tm)
    att = _attention(qkv_att.reshape(N_ATT_SLABS, B, S, LANES), B, S).reshape(T, ATT_W)
    ret = _retention(qkv_ret.reshape(N_SLABS - N_ATT_SLABS, B, S, LANES), p["decays"], p["gn"], B, S).reshape(T, RET_W)
    x1, h2, aff_t = _outproj(x2d, att, ret, p["w_out"], p["g2"], p["w_router_t"], tm)

    cap = CAPACITY_FACTOR * T // N_EXPERTS
    sel = _select(aff_t, cap)

    csum = jnp.cumsum(sel, axis=1)
    slots = jnp.where(sel > 0, csum - 1, -1)
    ends = csum[:, CMB_TOK - 1::CMB_TOK]
    excl = jnp.concatenate([jnp.zeros((N_EXPERTS, 1), jnp.int32), ends[:, :-1]], axis=1).reshape(-1)
    cnt = ends.reshape(-1) - excl

    idx = _compact(slots, excl, cnt, cap)
    xin = _sc_gather(h2, idx.reshape(-1)).reshape(N_EXPERTS, cap, ROW_WORDS)
    y = _ffn(xin, p["w_gate"], p["w_up"], p["w_down"], tm)
    out = _combine(x1, slots, y, excl, cnt)
    return out.reshape(B, S, D_MODEL)


def kernel(x_prompt, x_sample, norm1_g, w_in, attn_qnorm_g, attn_knorm_g, ret_decay_fwd, ret_decay_bwd,
           ret_norm_g, w_out, norm2_g, w_router, w_gate_e, w_up_e, w_down_e):
    y_prompt, y_sample = x_prompt, x_sample
    for l in range(norm1_g.shape[0]):
        p = {
            "g1": norm1_g[l][None, :],
            "w_in": w_in[l].astype(BF16),
            "gq": jnp.tile(attn_qnorm_g[l], LANES // ATT_HD)[None, :],
            "gk": jnp.tile(attn_knorm_g[l], LANES // ATT_HD)[None, :],
            "decays": jnp.stack([ret_decay_fwd[l], ret_decay_bwd[l]]).astype(F32),
            "gn": ret_norm_g[l][None, :].astype(F32),
            "w_out": w_out[l].astype(BF16),
            "g2": norm2_g[l][None, :],
            "w_router_t": w_router[l].T.astype(BF16),
            "w_gate": w_gate_e[l].astype(BF16),
            "w_up": w_up_e[l].astype(BF16),
            "w_down": w_down_e[l].astype(BF16),
        }
        y_prompt = _layer(y_prompt, p)
        y_sample = _layer(y_sample, p)
    return (y_prompt, y_sample)
```

```python
import functools

import jax
import jax.numpy as jnp
from jax import lax
from jax.experimental import pallas as pl
from jax.experimental.pallas import tpu as pltpu
from jax.experimental.pallas import tpu_sc as plsc

F32 = jnp.float32
BF16 = jnp.bfloat16

D_MODEL = 1024
ATT_HEADS, ATT_HD = 8, 64
RET_HEADS, RET_HD = 4, 128
ATT_W = ATT_HEADS * ATT_HD
RET_W = RET_HEADS * RET_HD
IN_W = 3 * ATT_W + 4 * RET_W
LANES = 128
N_SLABS = IN_W // LANES
GROUP_W = 512
SLABS_PER_GROUP = GROUP_W // LANES
N_ATT_SLABS = 3 * ATT_W // LANES
RET_CHUNK = 256
ROPE_THETA = 10000.0
N_EXPERTS = 16
D_FF = 2048
CAPACITY_FACTOR = 2
NORM_EPS = 1e-6
NEG_INF = -1e30
ATT_QBLK = 128
VMEM_LIMIT = 56 * 1024 * 1024


def _cparams(sem):
    return pltpu.CompilerParams(dimension_semantics=sem, vmem_limit_bytes=VMEM_LIMIT)


def _inproj_kernel(x_ref, g1_ref, w_ref, gq_ref, gk_ref, ca_ref, sa_ref, cr_ref, sr_ref, oa_ref, or_ref):
    x = x_ref[...]
    ms = jnp.mean(x * x, axis=-1, keepdims=True)
    h = (x * lax.rsqrt(ms + NORM_EPS) * g1_ref[...]).astype(BF16)
    tm = x.shape[0]
    lane = lax.broadcasted_iota(jnp.int32, (tm, LANES), 1)
    first = lane < ATT_HD
    low_half = (lane & (ATT_HD // 2)) == 0
    ca, sa, cr, sr = ca_ref[...], sa_ref[...], cr_ref[...], sr_ref[...]
    for grp in range(IN_W // GROUP_W):
        acc = jnp.dot(h, w_ref[:, grp * GROUP_W:(grp + 1) * GROUP_W], preferred_element_type=F32)
        for p in range(SLABS_PER_GROUP):
            a = acc[:, p * LANES:(p + 1) * LANES]
            if grp in (0, 1):
                sq = a * a
                s0 = jnp.sum(jnp.where(first, sq, 0.0), axis=-1, keepdims=True)
                s1 = jnp.sum(jnp.where(first, 0.0, sq), axis=-1, keepdims=True)
                ms2 = jnp.where(first, s0, s1) * (1.0 / ATT_HD)
                g = gq_ref[...] if grp == 0 else gk_ref[...]
                y = a * lax.rsqrt(ms2 + NORM_EPS) * g
                partner = jnp.where(low_half, pltpu.roll(y, LANES - ATT_HD // 2, 1),
                                    pltpu.roll(y, ATT_HD // 2, 1))
                r = y * ca + partner * sa
                if grp == 0:
                    r = r * (ATT_HD ** -0.5)
            elif grp in (3, 4):
                r = a * cr + pltpu.roll(a, RET_HD // 2, 1) * sr
                if grp == 4:
                    r = r * (RET_HD ** -0.5)
            else:
                r = a
            slab = grp * SLABS_PER_GROUP + p
            if slab < N_ATT_SLABS:
                oa_ref[slab] = r
            else:
                or_ref[slab - N_ATT_SLABS] = r.astype(BF16)


def _rope_tables(S):
    pos = jnp.arange(S, dtype=F32)

    def table(hd):
        inv_freq = ROPE_THETA ** (-jnp.arange(0, hd, 2, dtype=F32) / hd)
        ang = pos[:, None] * inv_freq[None, :]
        cos, sin = jnp.cos(ang), jnp.sin(ang)
        reps = LANES // hd
        cos_t = jnp.tile(jnp.concatenate([cos, cos], axis=-1), (1, reps))
        sin_t = jnp.tile(jnp.concatenate([-sin, sin], axis=-1), (1, reps))
        return cos_t, sin_t

    ca, sa = table(ATT_HD)
    cr, sr = table(RET_HD)
    return ca, sa, cr, sr


def _inproj(x2d, S, g1, w_in_bf, gq, gk, tables, tm):
    T = x2d.shape[0]
    n_pos_blk = S // tm
    tab_spec = pl.BlockSpec((tm, LANES), lambda i: (i % n_pos_blk, 0))
    full = lambda shape: pl.BlockSpec(shape, lambda i: (0,) * len(shape))
    return pl.pallas_call(
        _inproj_kernel,
        grid=(T // tm,),
        in_specs=[pl.BlockSpec((tm, D_MODEL), lambda i: (i, 0)), full((1, D_MODEL)),
                  full((D_MODEL, IN_W)), full((1, LANES)), full((1, LANES)),
                  tab_spec, tab_spec, tab_spec, tab_spec],
        out_specs=[pl.BlockSpec((N_ATT_SLABS, tm, LANES), lambda i: (0, i, 0)),
                   pl.BlockSpec((N_SLABS - N_ATT_SLABS, tm, LANES), lambda i: (0, i, 0))],
        out_shape=[jax.ShapeDtypeStruct((N_ATT_SLABS, T, LANES), F32),
                   jax.ShapeDtypeStruct((N_SLABS - N_ATT_SLABS, T, LANES), BF16)],
        compiler_params=_cparams(("parallel",)),
        name="inproj",
    )(x2d, g1, w_in_bf, gq, gk, *tables)


DILATED_PATTERNS = ((128, 1), (512, 4), (2048, 16))
ATT_HALF = 64
ATT_SB = 2048
ATT_NB = ATT_SB // ATT_QBLK
ATT_UNROLL = 2


def _attn_kernel(q_ref, k_ref, v_ref, o_ref, o_scr, lse_scr, bias_scr, *, S):
    t0 = pl.program_id(2) * ATT_SB
    lane = lax.broadcasted_iota(jnp.int32, (ATT_QBLK, LANES), 1)
    first = lane < ATT_HD
    nt_dims = (((1,), (1,)), ((), ()))

    rel = (lax.broadcasted_iota(jnp.int32, (ATT_QBLK, 2 * ATT_QBLK), 0)
           - lax.broadcasted_iota(jnp.int32, (ATT_QBLK, 2 * ATT_QBLK), 1))
    for oi in range(3):
        bias_scr[oi] = jnp.where(jnp.abs(rel + oi * ATT_HALF) <= ATT_HALF, 0.0, NEG_INF)

    def band(qv, kv, vv, off):
        kw = kv.shape[0]
        bias = bias_scr[off // ATT_HALF][:, :kw]
        zero = jnp.zeros_like(qv)
        q2 = jnp.concatenate([jnp.where(first, qv, zero), jnp.where(first, zero, qv)], axis=0)
        s = lax.dot_general(q2, kv, nt_dims, preferred_element_type=F32) + jnp.concatenate([bias, bias], axis=0)
        m = jnp.max(s, axis=-1, keepdims=True)
        p = jnp.exp(s - m)
        l = jnp.sum(p, axis=-1, keepdims=True)
        acc = jnp.dot(p.astype(BF16), vv, preferred_element_type=F32) * (1.0 / l)
        lse = m + jnp.log(l)
        return (jnp.where(first, acc[:ATT_QBLK], acc[ATT_QBLK:]),
                jnp.where(first, lse[:ATT_QBLK], lse[ATT_QBLK:]))

    def step(n, carry):
        for pi, (window, dil) in enumerate(DILATED_PATTERNS):
            L = S // dil
            kw = min(2 * ATT_QBLK, L)
            per_class = ATT_SB // dil // ATT_QBLK
            r = n // per_class
            bi = n % per_class
            lq0 = t0 // dil + bi * ATT_QBLK
            lk0 = jnp.clip(lq0 - ATT_HALF, 0, L - kw)
            qrow = dil * ATT_QBLK * bi + r
            krow = dil * lk0 + r
            if dil == 1:
                qs = pl.ds(pl.multiple_of(qrow, ATT_QBLK), ATT_QBLK)
                ks = pl.ds(pl.multiple_of(krow, 8), kw)
            else:
                qs = pl.ds(qrow, ATT_QBLK, stride=dil)
                ks = pl.ds(krow, kw, stride=dil)
            o, lse = band(q_ref[qs, :].astype(BF16), k_ref[ks, :].astype(BF16), v_ref[ks, :].astype(BF16),
                          lq0 - lk0)
            o_scr[pi, qs, :] = o
            lse_scr[pi, qs, :] = lse
        return carry

    lax.fori_loop(0, ATT_NB, step, 0, unroll=ATT_UNROLL)

    def merge(c, carry):
        sl = pl.ds(pl.multiple_of(c * ATT_QBLK, ATT_QBLK), ATT_QBLK)
        lses = [lse_scr[pi, sl, :] for pi in range(3)]
        mx = jnp.maximum(jnp.maximum(lses[0], lses[1]), lses[2])
        ws = [jnp.exp(x - mx) for x in lses]
        num = ws[0] * o_scr[0, sl, :] + ws[1] * o_scr[1, sl, :] + ws[2] * o_scr[2, sl, :]
        o_ref[sl, :] = (num / (ws[0] + ws[1] + ws[2])).astype(o_ref.dtype)
        return carry

    lax.fori_loop(0, ATT_NB, merge, 0)


def _attention(qkv4, B, S):
    n_pairs = ATT_W // LANES
    kv_spec = lambda off: pl.BlockSpec((None, None, S, LANES), lambda b, p, i: (off + p, b, 0, 0))
    return pl.pallas_call(
        functools.partial(_attn_kernel, S=S),
        grid=(B, n_pairs, S // ATT_SB),
        in_specs=[pl.BlockSpec((None, None, ATT_SB, LANES), lambda b, p, i: (p, b, i, 0)),
                  kv_spec(n_pairs), kv_spec(2 * n_pairs)],
        out_specs=pl.BlockSpec((None, ATT_SB, LANES), lambda b, p, i: (b, i, p)),
        out_shape=jax.ShapeDtypeStruct((B, S, ATT_W), BF16),
        scratch_shapes=[pltpu.VMEM((3, ATT_SB, LANES), F32), pltpu.VMEM((3, ATT_SB, LANES), F32),
                        pltpu.VMEM((3, ATT_QBLK, 2 * ATT_QBLK), F32)],
        compiler_params=_cparams(("parallel", "parallel", "arbitrary")),
        name="dilated_attn",
    )(qkv4, qkv4, qkv4)


def _ret_kernel(dec_ref, q_ref, k_ref, v_ref, g_ref, gn_ref, o_ref, of_scr, ob_scr, *, S):
    C = RET_CHUNK
    nc = S // C
    h = pl.program_id(1)
    nt_dims = (((1,), (1,)), ((), ()))
    tn_dims = (((0,), (0,)), ((), ()))

    def consts(direction):
        def log_g(shape):
            return -jnp.exp(jnp.full(shape, dec_ref[direction, h], F32))

        row = lax.broadcasted_iota(jnp.int32, (C, C), 0).astype(F32)
        col = lax.broadcasted_iota(jnp.int32, (C, C), 1).astype(F32)
        n = lax.broadcasted_iota(jnp.int32, (C, RET_HD), 0).astype(F32)
        lg = log_g((C, RET_HD))
        if direction == 0:
            diff = row - col
            mask = diff >= 0.0
            k_dec, q_dec = jnp.exp(lg * (C - 1.0 - n)), jnp.exp(lg * (n + 1.0))
        else:
            diff = col - row
            mask = diff > 0.0
            k_dec, q_dec = jnp.exp(lg * n), jnp.exp(lg * (C - n))
        dmat = jnp.where(mask, jnp.exp(log_g((C, C)) * jnp.maximum(diff, 0.0)), 0.0)
        return dmat, k_dec, q_dec, jnp.exp(log_g((RET_HD, RET_HD)) * C)

    def chunk_out(c, state, dmat, k_dec, q_dec, g_chunk):
        sl = pl.ds(pl.multiple_of(c * C, C), C)
        qc, kc, vc = q_ref[sl, :], k_ref[sl, :], v_ref[sl, :]
        qk = lax.dot_general(qc, kc, nt_dims, preferred_element_type=F32)
        inner = (qk * dmat).astype(BF16)
        o = jnp.dot(inner, vc, preferred_element_type=F32)
        o = o + jnp.dot((qc.astype(F32) * q_dec).astype(BF16), state.astype(BF16),
                        preferred_element_type=F32)
        kd = (kc.astype(F32) * k_dec).astype(BF16)
        kv = lax.dot_general(kd, vc, tn_dims, preferred_element_type=F32)
        return sl, o, state * g_chunk + kv

    cf, cb = consts(0), consts(1)

    def scan(i, states):
        sl_f, o_f, st_f = chunk_out(i, states[0], *cf)
        of_scr[sl_f, :] = o_f
        sl_b, o_b, st_b = chunk_out(nc - 1 - i, states[1], *cb)
        ob_scr[sl_b, :] = o_b
        return st_f, st_b

    zero = jnp.zeros((RET_HD, RET_HD), F32)
    lax.fori_loop(0, nc, scan, (zero, zero), unroll=2)

    gn = gn_ref[...]

    def finish(c, carry):
        sl = pl.ds(pl.multiple_of(c * C, C), C)
        o = of_scr[sl, :] + ob_scr[sl, :]
        mu = jnp.mean(o, axis=-1, keepdims=True)
        var = jnp.mean(jnp.square(o - mu), axis=-1, keepdims=True)
        y = (o - mu) * lax.rsqrt(var + NORM_EPS) * gn
        g = g_ref[sl, :].astype(F32)
        o_ref[sl, :] = (y * (g * (1.0 / (1.0 + jnp.exp(-g))))).astype(o_ref.dtype)
        return carry

    lax.fori_loop(0, nc, finish, 0, unroll=2)


def _retention(qkv4, decays, gn, B, S):
    spec = lambda off: pl.BlockSpec((None, None, S, LANES), lambda b, h: (off + h, b, 0, 0))
    return pl.pallas_call(
        functools.partial(_ret_kernel, S=S),
        grid=(B, RET_HEADS),
        in_specs=[pl.BlockSpec(memory_space=pltpu.SMEM),
                  spec(0), spec(RET_HEADS), spec(2 * RET_HEADS), spec(3 * RET_HEADS),
                  pl.BlockSpec((1, LANES), lambda b, h: (0, h))],
        out_specs=pl.BlockSpec((None, S, LANES), lambda b, h: (b, 0, h)),
        out_shape=jax.ShapeDtypeStruct((B, S, RET_W), BF16),
        scratch_shapes=[pltpu.VMEM((S, LANES), F32), pltpu.VMEM((S, LANES), F32)],
        compiler_params=_cparams(("parallel", "arbitrary")),
        name="retention",
    )(decays, qkv4, qkv4, qkv4, qkv4, gn)


HI16 = -65536
ROW_WORDS = D_MODEL // 2 + LANES


def _pack_bf16_pairs(x):
    n = x.shape[1] // 2
    bits = pltpu.bitcast(x.astype(BF16).astype(F32), jnp.int32)
    return (bits[:, n:] & HI16) | lax.shift_right_logical(bits[:, :n], 16)


def _unpack_bf16_pairs(w):
    lo = pltpu.bitcast(lax.shift_left(w, 16), F32).astype(BF16)
    hi = pltpu.bitcast(w & HI16, F32).astype(BF16)
    return jnp.concatenate([lo, hi], axis=1)


def _outproj_kernel(x_ref, att_ref, ret_ref, wo_ref, g2_ref, wr_ref, x1_ref, h2_ref, aff_ref):
    y = jnp.dot(att_ref[...], wo_ref[:ATT_W, :], preferred_element_type=F32)
    y = y + jnp.dot(ret_ref[...], wo_ref[ATT_W:, :], preferred_element_type=F32)
    x1 = x_ref[...] + y
    x1_ref[...] = x1
    ms = jnp.mean(x1 * x1, axis=-1, keepdims=True)
    h2 = x1 * lax.rsqrt(ms + NORM_EPS) * g2_ref[...]
    h2_ref[:, :D_MODEL // 2] = _pack_bf16_pairs(h2)
    logits = lax.dot_general(wr_ref[...], h2.astype(BF16), (((1,), (1,)), ((), ())),
                             preferred_element_type=F32)
    e = jnp.exp(logits - jnp.max(logits, axis=0, keepdims=True))
    aff = e / jnp.sum(e, axis=0, keepdims=True)
    aff_ref[...] = aff
    pad = jnp.zeros((LANES - N_EXPERTS, aff.shape[1]), F32)
    h2_ref[:, D_MODEL // 2:] = pltpu.bitcast(jnp.transpose(jnp.concatenate([aff, pad], axis=0)), jnp.int32)


def _outproj(x2d, att, ret, wo_bf, g2, wr_t_bf, tm):
    T = x2d.shape[0]
    full = lambda shape: pl.BlockSpec(shape, lambda i: (0,) * len(shape))
    return pl.pallas_call(
        _outproj_kernel,
        grid=(T // tm,),
        in_specs=[pl.BlockSpec((tm, D_MODEL), lambda i: (i, 0)),
                  pl.BlockSpec((tm, ATT_W), lambda i: (i, 0)),
                  pl.BlockSpec((tm, RET_W), lambda i: (i, 0)),
                  full((ATT_W + RET_W, D_MODEL)), full((1, D_MODEL)), full((N_EXPERTS, D_MODEL))],
        out_specs=[pl.BlockSpec((tm, D_MODEL), lambda i: (i, 0)),
                   pl.BlockSpec((tm, ROW_WORDS), lambda i: (i, 0)),
                   pl.BlockSpec((N_EXPERTS, tm), lambda i: (0, i))],
        out_shape=[jax.ShapeDtypeStruct((T, D_MODEL), F32),
                   jax.ShapeDtypeStruct((T, ROW_WORDS), jnp.int32),
                   jax.ShapeDtypeStruct((N_EXPERTS, T), F32)],
        compiler_params=_cparams(("parallel",)),
        name="outproj_router",
    )(x2d, att, ret, wo_bf, g2, wr_t_bf)


def _select_kernel(aff_ref, sel_ref, *, cap):
    aff = aff_ref[...]
    E, T = aff.shape
    capf = jnp.float32(cap)

    def count(mask):
        return jnp.sum(jnp.where(mask, 1.0, 0.0), axis=1, keepdims=True)

    def thr_step(i, thr_bits):
        cand = thr_bits | jnp.left_shift(jnp.int32(1), 30 - i)
        return jnp.where(count(aff >= pltpu.bitcast(cand, F32)) >= capf, cand, thr_bits)

    thr = pltpu.bitcast(lax.fori_loop(0, 31, thr_step, jnp.zeros((E, 1), jnp.int32)), F32)
    above = aff > thr
    ties = aff == thr
    need = capf - count(above)
    idx = lax.broadcasted_iota(jnp.int32, (E, T), 1)
    n_idx_bits = max(1, (T - 1).bit_length())

    def cut_step(i, cut):
        cand = cut | jnp.left_shift(jnp.int32(1), n_idx_bits - 1 - i)
        return jnp.where(count(ties & (idx < cand)) < need, cand, cut)

    cut = lax.fori_loop(0, n_idx_bits, cut_step, jnp.zeros((E, 1), jnp.int32))
    sel_ref[...] = jnp.where(above | (ties & (idx <= cut)), 1, 0).astype(jnp.int32)


def _select(aff_t, cap):
    E, T = aff_t.shape
    return pl.pallas_call(
        functools.partial(_select_kernel, cap=cap),
        grid=(1,),
        in_specs=[pl.BlockSpec((E, T), lambda i: (0, 0))],
        out_specs=pl.BlockSpec((E, T), lambda i: (0, 0)),
        out_shape=jax.ShapeDtypeStruct((E, T), jnp.int32),
        compiler_params=_cparams(("arbitrary",)),
        name="expert_select",
    )(aff_t)


CMB_TOK = 512


def _token_major(slots):
    n_e, n_tok = slots.shape
    padded = jnp.concatenate([slots, jnp.full((LANES - n_e, n_tok), -1, jnp.int32)], axis=0)
    return pltpu.bitcast(jnp.transpose(pltpu.bitcast(padded, F32)), jnp.int32)


def _spread_slots(pos, bases, width):
    n_e, n_col = len(bases), pos.shape[1]
    col = lax.broadcasted_iota(jnp.int32, pos.shape, 1)
    base_arr = jnp.zeros(pos.shape, jnp.int32)
    for e in range(n_e):
        base_arr = jnp.where(col == e, bases[e], base_arr)
    rel = pos - base_arr
    rel = jnp.where((rel >= 0) & (rel < width), rel, -1).astype(F32).astype(BF16)
    expander = (lax.broadcasted_iota(jnp.int32, (n_col, n_e * width), 0)
                == lax.broadcasted_iota(jnp.int32, (n_col, n_e * width), 1) // width)
    return jnp.dot(rel, jnp.where(expander, 1.0, 0.0).astype(BF16), preferred_element_type=F32)


def _compact_kernel(excl_ref, cnt_ref, slots_ref, idx_ref, pos_ref, *, cap, nblk):
    b = pl.program_id(0)
    ntiles = cap // LANES
    width = 2 * LANES

    @pl.when(b == 0)
    def _():
        idx_ref[...] = jnp.zeros(idx_ref.shape, F32)

    pos_ref[...] = _token_major(slots_ref[...])

    lane = lax.broadcasted_iota(jnp.int32, (CMB_TOK, width), 1)
    tok = (b * CMB_TOK + lax.broadcasted_iota(jnp.int32, (CMB_TOK, width), 0)).astype(F32)

    def window(e, tile_nominal):
        tile = jnp.minimum(tile_nominal, ntiles - 2)
        pos_col = pos_ref[:, e:e + 1]
        hit = ((pos_col - tile * LANES) == lane) & (pos_col >= tile_nominal * LANES)
        row = jnp.sum(jnp.where(hit, tok, 0.0), axis=0, keepdims=True)
        for j in range(2):
            idx_ref[e, pl.ds(tile + j, 1), :] += row[:, j * LANES:(j + 1) * LANES]

    tiles = [jnp.minimum(excl_ref[e * nblk + b] // LANES, ntiles - 2) for e in range(N_EXPERTS)]
    rel = _spread_slots(pos_ref[...], [t * LANES for t in tiles], width)
    lane_f = lax.broadcasted_iota(jnp.int32, (CMB_TOK, LANES), 1).astype(F32)
    tok_f = (b * CMB_TOK + lax.broadcasted_iota(jnp.int32, (CMB_TOK, LANES), 0)).astype(F32)
    for e in range(N_EXPERTS):
        for j in range(2):
            piece = rel[:, e * width + j * LANES:e * width + (j + 1) * LANES]
            row = jnp.sum(jnp.where(piece == lane_f + float(j * LANES), tok_f, 0.0), axis=0, keepdims=True)
            idx_ref[e, pl.ds(tiles[e] + j, 1), :] += row

    for e in range(N_EXPERTS):
        first = excl_ref[e * nblk + b]
        tile0 = first // LANES
        need_end = first + cnt_ref[e * nblk + b]
        n_extra = jnp.maximum((need_end - tile0 * LANES + width - 1) // width - 1, 0)

        def extra(k, carry):
            window(e, tile0 + 2 * (k + 1))
            return carry

        lax.fori_loop(0, n_extra, extra, 0)


def _compact(slots, excl, cnt, cap):
    E, T = slots.shape
    nblk = T // CMB_TOK
    grid_spec = pltpu.PrefetchScalarGridSpec(
        num_scalar_prefetch=2,
        grid=(nblk,),
        in_specs=[pl.BlockSpec((E, CMB_TOK), lambda b, *_: (0, b))],
        out_specs=pl.BlockSpec((E, cap // LANES, LANES), lambda b, *_: (0, 0, 0)),
        scratch_shapes=[pltpu.VMEM((CMB_TOK, LANES), jnp.int32)],
    )
    idx = pl.pallas_call(
        functools.partial(_compact_kernel, cap=cap, nblk=nblk),
        grid_spec=grid_spec,
        out_shape=jax.ShapeDtypeStruct((E, cap // LANES, LANES), F32),
        compiler_params=_cparams(("arbitrary",)),
        name="expert_lists",
    )(excl, cnt, slots)
    return idx.reshape(E, cap).astype(jnp.int32)


SC_CORES = 2
SC_SUBCORES = 16
SC_ROWS = 64


def _sc_gather(table, idx):
    n, width = idx.shape[0], table.shape[1]
    workers = SC_CORES * SC_SUBCORES
    per_worker = n // workers
    mesh = plsc.VectorSubcoreMesh(core_axis_name="c", subcore_axis_name="s")

    @functools.partial(
        pl.kernel, mesh=mesh,
        out_type=jax.ShapeDtypeStruct((n, width), table.dtype),
        scratch_types=[pltpu.VMEM((SC_ROWS,), jnp.int32), pltpu.VMEM((SC_ROWS, width), table.dtype),
                       pltpu.SemaphoreType.DMA],
    )
    def gather(table_hbm, idx_hbm, out_hbm, idx_v, rows_v, sem):
        base = (lax.axis_index("s") * SC_CORES + lax.axis_index("c")) * per_worker

        @pl.loop(0, per_worker // SC_ROWS)
        def _(i):
            off = pl.multiple_of(base + i * SC_ROWS, SC_ROWS)
            pltpu.sync_copy(idx_hbm.at[pl.ds(off, SC_ROWS)], idx_v)
            pltpu.async_copy(table_hbm.at[idx_v], rows_v, sem).wait()
            pltpu.sync_copy(rows_v, out_hbm.at[pl.ds(off, SC_ROWS)])

    return gather(table, idx)


FF_CHUNK = 512


def _ffn_kernel(xin_ref, wg_ref, wu_ref, wd_ref, y_ref):
    xin = _unpack_bf16_pairs(xin_ref[:, :D_MODEL // 2])
    affs = pltpu.bitcast(xin_ref[:, D_MODEL // 2:], F32)
    lane = lax.broadcasted_iota(jnp.int32, affs.shape, 1)
    gate = jnp.sum(jnp.where(lane == pl.program_id(0), affs, 0.0), axis=-1, keepdims=True)
    acc = jnp.zeros(y_ref.shape, F32)
    for f in range(D_FF // FF_CHUNK):
        fs = slice(f * FF_CHUNK, (f + 1) * FF_CHUNK)
        a = jnp.dot(xin, wg_ref[:, fs], preferred_element_type=F32)
        u = jnp.dot(xin, wu_ref[:, fs], preferred_element_type=F32)
        hmid = (a * (1.0 / (1.0 + jnp.exp(-a))) * u).astype(BF16)
        acc = acc + jnp.dot(hmid, wd_ref[fs, :], preferred_element_type=F32)
    y_ref[...] = (acc * gate).astype(y_ref.dtype)


def _ffn(xin, wg_bf, wu_bf, wd_bf, tm):
    E, cap, _ = xin.shape
    return pl.pallas_call(
        _ffn_kernel,
        grid=(E, cap // tm),
        in_specs=[pl.BlockSpec((None, tm, ROW_WORDS), lambda e, j: (e, j, 0)),
                  pl.BlockSpec((None, D_MODEL, D_FF), lambda e, j: (e, 0, 0)),
                  pl.BlockSpec((None, D_MODEL, D_FF), lambda e, j: (e, 0, 0)),
                  pl.BlockSpec((None, D_FF, D_MODEL), lambda e, j: (e, 0, 0))],
        out_specs=pl.BlockSpec((None, tm, D_MODEL), lambda e, j: (e, j, 0)),
        out_shape=jax.ShapeDtypeStruct((E, cap, D_MODEL), BF16),
        compiler_params=_cparams(("parallel", "arbitrary")),
        name="expert_ffn",
    )(xin, wg_bf, wu_bf, wd_bf)


CMB_ROWS = 128
BF16_SUBLANES = 16


def _combine_kernel(excl_ref, cnt_ref, x1_ref, slots_ref, y_hbm, o_ref, ybuf, xbuf, sem, xsem, pos_ref, *, cap, nblk):
    b = pl.program_id(0)
    slot = b % 2
    lane = lax.broadcasted_iota(jnp.int32, (CMB_TOK, CMB_ROWS), 1)

    def first_row(e, blk):
        return (excl_ref[e * nblk + blk] // BF16_SUBLANES) * BF16_SUBLANES

    def fetch(e, row, dst, s):
        row = pl.multiple_of(jnp.minimum(row, cap - CMB_ROWS), BF16_SUBLANES)
        return pltpu.make_async_copy(y_hbm.at[e, pl.ds(row, CMB_ROWS), :], dst, s)

    def start_block(blk, sl):
        for e in range(N_EXPERTS):
            fetch(e, first_row(e, blk), ybuf.at[sl, e], sem.at[sl, e]).start()

    @pl.when(b == 0)
    def _():
        start_block(0, 0)

    @pl.when(b + 1 < nblk)
    def _():
        start_block(b + 1, 1 - slot)

    def onehot(pos_col, nominal):
        hit = ((pos_col - jnp.minimum(nominal, cap - CMB_ROWS)) == lane) & (pos_col >= nominal)
        return jnp.where(hit, 1.0, 0.0).astype(BF16)

    pos_ref[...] = _token_major(slots_ref[...])
    bases = []
    for e in range(N_EXPERTS):
        nominal0 = first_row(e, b)
        fetch(e, nominal0, ybuf.at[slot, e], sem.at[slot, e]).wait()
        bases.append(jnp.minimum(nominal0, cap - CMB_ROWS))
    rel = _spread_slots(pos_ref[...], bases, CMB_ROWS)
    lane_f = lane.astype(F32)
    parts = [jnp.where(rel[:, e * CMB_ROWS:(e + 1) * CMB_ROWS] == lane_f, 1.0, 0.0).astype(BF16)
             for e in range(N_EXPERTS)]
    rows = ybuf[slot].reshape(N_EXPERTS * CMB_ROWS, D_MODEL)
    o_ref[...] = x1_ref[...] + jnp.dot(jnp.concatenate(parts, axis=1), rows, preferred_element_type=F32)

    for e in range(N_EXPERTS):
        nominal0 = first_row(e, b)
        need_end = excl_ref[e * nblk + b] + cnt_ref[e * nblk + b]
        n_extra = jnp.maximum((need_end - nominal0 + CMB_ROWS - 1) // CMB_ROWS - 1, 0)

        def extra(k, carry):
            nominal = nominal0 + (k + 1) * CMB_ROWS
            cp = fetch(e, nominal, xbuf, xsem)
            cp.start()
            cp.wait()
            o_ref[...] += jnp.dot(onehot(pos_ref[:, e:e + 1], nominal), xbuf[...], preferred_element_type=F32)
            return carry

        lax.fori_loop(0, n_extra, extra, 0)


def _combine(x1, slots, y, excl, cnt):
    T = x1.shape[0]
    E, cap, _ = y.shape
    nblk = T // CMB_TOK
    grid_spec = pltpu.PrefetchScalarGridSpec(
        num_scalar_prefetch=2,
        grid=(nblk,),
        in_specs=[pl.BlockSpec((CMB_TOK, D_MODEL), lambda b, *_: (b, 0)),
                  pl.BlockSpec((E, CMB_TOK), lambda b, *_: (0, b)),
                  pl.BlockSpec(memory_space=pl.ANY)],
        out_specs=pl.BlockSpec((CMB_TOK, D_MODEL), lambda b, *_: (b, 0)),
        scratch_shapes=[pltpu.VMEM((2, E, CMB_ROWS, D_MODEL), BF16),
                        pltpu.VMEM((CMB_ROWS, D_MODEL), BF16),
                        pltpu.SemaphoreType.DMA((2, E)),
                        pltpu.SemaphoreType.DMA(()),
                        pltpu.VMEM((CMB_TOK, LANES), jnp.int32)],
    )
    return pl.pallas_call(
        functools.partial(_combine_kernel, cap=cap, nblk=nblk),
        grid_spec=grid_spec,
        out_shape=jax.ShapeDtypeStruct((T, D_MODEL), F32),
        compiler_params=_cparams(("arbitrary",)),
        name="moe_combine",
    )(excl, cnt, x1, slots, y)


def _layer(x, p):
    B, S, _ = x.shape
    T = B * S
    tm = 512
    x2d = x.reshape(T, D_MODEL)
    qkv_att, qkv_ret = _inproj(x2d, S, p["g1"], p["w_in"], p["gq"], p["gk"], _rope_tables(S), tm)
    att = _attention(qkv_att.reshape(N_ATT_SLABS, B, S, LANES), B, S).reshape(T, ATT_W)
    ret = _retention(qkv_ret.reshape(N_SLABS - N_ATT_SLABS, B, S, LANES), p["decays"], p["gn"], B, S).reshape(T, RET_W)
    x1, h2, aff_t = _outproj(x2d, att, ret, p["w_out"], p["g2"], p["w_router_t"], tm)

    cap = CAPACITY_FACTOR * T // N_EXPERTS
    sel = _select(aff_t, cap)

    csum = jnp.cumsum(sel, axis=1)
    slots = jnp.where(sel > 0, csum - 1, -1)
    ends = csum[:, CMB_TOK - 1::CMB_TOK]
    excl = jnp.concatenate([jnp.zeros((N_EXPERTS, 1), jnp.int32), ends[:, :-1]], axis=1).reshape(-1)
    cnt = ends.reshape(-1) - excl

    idx = _compact(slots, excl, cnt, cap)
    xin = _sc_gather(h2, idx.reshape(-1)).reshape(N_EXPERTS, cap, ROW_WORDS)
    y = _ffn(xin, p["w_gate"], p["w_up"], p["w_down"], tm)
    out = _combine(x1, slots, y, excl, cnt)
    return out.reshape(B, S, D_MODEL)


def kernel(x_prompt, x_sample, norm1_g, w_in, attn_qnorm_g, attn_knorm_g, ret_decay_fwd, ret_decay_bwd,
           ret_norm_g, w_out, norm2_g, w_router, w_gate_e, w_up_e, w_down_e):
    y_prompt, y_sample = x_prompt, x_sample
    for l in range(norm1_g.shape[0]):
        p = {
            "g1": norm1_g[l][None, :],
            "w_in": w_in[l].astype(BF16),
            "gq": jnp.tile(attn_qnorm_g[l], LANES // ATT_HD)[None, :],
            "gk": jnp.tile(attn_knorm_g[l], LANES // ATT_HD)[None, :],
            "decays": jnp.stack([ret_decay_fwd[l], ret_decay_bwd[l]]).astype(F32),
            "gn": ret_norm_g[l][None, :].astype(F32),
            "w_out": w_out[l].astype(BF16),
            "g2": norm2_g[l][None, :],
            "w_router_t": w_router[l].T.astype(BF16),
            "w_gate": w_gate_e[l].astype(BF16),
            "w_up": w_up_e[l].astype(BF16),
            "w_down": w_down_e[l].astype(BF16),
        }
        y_sample = _layer(y_sample, p)
        y_prompt = _layer(y_prompt, p)
    return (y_prompt, y_sample)
```

```python
import functools

import jax
import jax.numpy as jnp
from jax import lax
from jax.experimental import pallas as pl
from jax.experimental.pallas import tpu as pltpu
from jax.experimental.pallas import tpu_sc as plsc

F32 = jnp.float32
BF16 = jnp.bfloat16

D_MODEL = 1024
ATT_HEADS, ATT_HD = 8, 64
RET_HEADS, RET_HD = 4, 128
ATT_W = ATT_HEADS * ATT_HD
RET_W = RET_HEADS * RET_HD
IN_W = 3 * ATT_W + 4 * RET_W
LANES = 128
N_SLABS = IN_W // LANES
GROUP_W = 512
SLABS_PER_GROUP = GROUP_W // LANES
N_ATT_SLABS = 3 * ATT_W // LANES
RET_CHUNK = 256
ROPE_THETA = 10000.0
N_EXPERTS = 16
D_FF = 2048
CAPACITY_FACTOR = 2
NORM_EPS = 1e-6
NEG_INF = -1e30
ATT_QBLK = 128
VMEM_LIMIT = 56 * 1024 * 1024


def _cparams(sem):
    return pltpu.CompilerParams(dimension_semantics=sem, vmem_limit_bytes=VMEM_LIMIT)


def _inproj_kernel(x_ref, g1_ref, w_ref, gq_ref, gk_ref, ca_ref, sa_ref, cr_ref, sr_ref, oa_ref, or_ref):
    x = x_ref[...]
    ms = jnp.mean(x * x, axis=-1, keepdims=True)
    h = (x * lax.rsqrt(ms + NORM_EPS) * g1_ref[...]).astype(BF16)
    tm = x.shape[0]
    lane = lax.broadcasted_iota(jnp.int32, (tm, LANES), 1)
    first = lane < ATT_HD
    low_half = (lane & (ATT_HD // 2)) == 0
    ca, sa, cr, sr = ca_ref[...], sa_ref[...], cr_ref[...], sr_ref[...]
    for grp in range(IN_W // GROUP_W):
        acc = jnp.dot(h, w_ref[:, grp * GROUP_W:(grp + 1) * GROUP_W], preferred_element_type=F32)
        for p in range(SLABS_PER_GROUP):
            a = acc[:, p * LANES:(p + 1) * LANES]
            if grp in (0, 1):
                sq = a * a
                s0 = jnp.sum(jnp.where(first, sq, 0.0), axis=-1, keepdims=True)
                s1 = jnp.sum(jnp.where(first, 0.0, sq), axis=-1, keepdims=True)
                ms2 = jnp.where(first, s0, s1) * (1.0 / ATT_HD)
                g = gq_ref[...] if grp == 0 else gk_ref[...]
                y = a * lax.rsqrt(ms2 + NORM_EPS) * g
                partner = jnp.where(low_half, pltpu.roll(y, LANES - ATT_HD // 2, 1),
                                    pltpu.roll(y, ATT_HD // 2, 1))
                r = y * ca + partner * sa
                if grp == 0:
                    r = r * (ATT_HD ** -0.5)
            elif grp in (3, 4):
                r = a * cr + pltpu.roll(a, RET_HD // 2, 1) * sr
                if grp == 4:
                    r = r * (RET_HD ** -0.5)
            else:
                r = a
            slab = grp * SLABS_PER_GROUP + p
            if slab < N_ATT_SLABS:
                oa_ref[slab] = r
            else:
                or_ref[slab - N_ATT_SLABS] = r.astype(BF16)


def _rope_tables(S):
    pos = jnp.arange(S, dtype=F32)

    def table(hd):
        inv_freq = ROPE_THETA ** (-jnp.arange(0, hd, 2, dtype=F32) / hd)
        ang = pos[:, None] * inv_freq[None, :]
        cos, sin = jnp.cos(ang), jnp.sin(ang)
        reps = LANES // hd
        cos_t = jnp.tile(jnp.concatenate([cos, cos], axis=-1), (1, reps))
        sin_t = jnp.tile(jnp.concatenate([-sin, sin], axis=-1), (1, reps))
        return cos_t, sin_t

    ca, sa = table(ATT_HD)
    cr, sr = table(RET_HD)
    return ca, sa, cr, sr


def _inproj(x2d, S, g1, w_in_bf, gq, gk, tables, tm):
    T = x2d.shape[0]
    n_pos_blk = S // tm
    tab_spec = pl.BlockSpec((tm, LANES), lambda i: (i % n_pos_blk, 0))
    full = lambda shape: pl.BlockSpec(shape, lambda i: (0,) * len(shape))
    return pl.pallas_call(
        _inproj_kernel,
        grid=(T // tm,),
        in_specs=[pl.BlockSpec((tm, D_MODEL), lambda i: (i, 0)), full((1, D_MODEL)),
                  full((D_MODEL, IN_W)), full((1, LANES)), full((1, LANES)),
                  tab_spec, tab_spec, tab_spec, tab_spec],
        out_specs=[pl.BlockSpec((N_ATT_SLABS, tm, LANES), lambda i: (0, i, 0)),
                   pl.BlockSpec((N_SLABS - N_ATT_SLABS, tm, LANES), lambda i: (0, i, 0))],
        out_shape=[jax.ShapeDtypeStruct((N_ATT_SLABS, T, LANES), F32),
                   jax.ShapeDtypeStruct((N_SLABS - N_ATT_SLABS, T, LANES), BF16)],
        compiler_params=_cparams(("parallel",)),
        name="inproj",
    )(x2d, g1, w_in_bf, gq, gk, *tables)


DILATED_PATTERNS = ((128, 1), (512, 4), (2048, 16))
ATT_HALF = 64
ATT_SB = 2048
ATT_NB = ATT_SB // ATT_QBLK
ATT_UNROLL = 2


def _attn_kernel(q_ref, k_ref, v_ref, o_ref, o_scr, lse_scr, bias_scr, *, S):
    t0 = pl.program_id(2) * ATT_SB
    lane = lax.broadcasted_iota(jnp.int32, (ATT_QBLK, LANES), 1)
    first = lane < ATT_HD
    nt_dims = (((1,), (1,)), ((), ()))

    rel = (lax.broadcasted_iota(jnp.int32, (ATT_QBLK, 2 * ATT_QBLK), 0)
           - lax.broadcasted_iota(jnp.int32, (ATT_QBLK, 2 * ATT_QBLK), 1))
    for oi in range(3):
        bias_scr[oi] = jnp.where(jnp.abs(rel + oi * ATT_HALF) <= ATT_HALF, 0.0, NEG_INF)

    def band(qv, kv, vv, off):
        kw = kv.shape[0]
        bias = bias_scr[off // ATT_HALF][:, :kw]
        zero = jnp.zeros_like(qv)
        q2 = jnp.concatenate([jnp.where(first, qv, zero), jnp.where(first, zero, qv)], axis=0)
        s = lax.dot_general(q2, kv, nt_dims, preferred_element_type=F32) + jnp.concatenate([bias, bias], axis=0)
        m = jnp.max(s, axis=-1, keepdims=True)
        p = jnp.exp(s - m)
        l = jnp.sum(p, axis=-1, keepdims=True)
        acc = jnp.dot(p.astype(BF16), vv, preferred_element_type=F32) * (1.0 / l)
        lse = m + jnp.log(l)
        return (jnp.where(first, acc[:ATT_QBLK], acc[ATT_QBLK:]),
                jnp.where(first, lse[:ATT_QBLK], lse[ATT_QBLK:]))

    def step(n, carry):
        for pi, (window, dil) in enumerate(DILATED_PATTERNS):
            L = S // dil
            kw = min(2 * ATT_QBLK, L)
            per_class = ATT_SB // dil // ATT_QBLK
            r = n // per_class
            bi = n % per_class
            lq0 = t0 // dil + bi * ATT_QBLK
            lk0 = jnp.clip(lq0 - ATT_HALF, 0, L - kw)
            qrow = dil * ATT_QBLK * bi + r
            krow = dil * lk0 + r
            if dil == 1:
                qs = pl.ds(pl.multiple_of(qrow, ATT_QBLK), ATT_QBLK)
                ks = pl.ds(pl.multiple_of(krow, 8), kw)
            else:
                qs = pl.ds(qrow, ATT_QBLK, stride=dil)
                ks = pl.ds(krow, kw, stride=dil)
            o, lse = band(q_ref[qs, :].astype(BF16), k_ref[ks, :].astype(BF16), v_ref[ks, :].astype(BF16),
                          lq0 - lk0)
            o_scr[pi, qs, :] = o
            lse_scr[pi, qs, :] = lse
        return carry

    lax.fori_loop(0, ATT_NB, step, 0, unroll=ATT_UNROLL)

    def merge(c, carry):
        sl = pl.ds(pl.multiple_of(c * ATT_QBLK, ATT_QBLK), ATT_QBLK)
        lses = [lse_scr[pi, sl, :] for pi in range(3)]
        mx = jnp.maximum(jnp.maximum(lses[0], lses[1]), lses[2])
        ws = [jnp.exp(x - mx) for x in lses]
        num = ws[0] * o_scr[0, sl, :] + ws[1] * o_scr[1, sl, :] + ws[2] * o_scr[2, sl, :]
        o_ref[sl, :] = (num / (ws[0] + ws[1] + ws[2])).astype(o_ref.dtype)
        return carry

    lax.fori_loop(0, ATT_NB, merge, 0)


def _attention(qkv4, B, S):
    n_pairs = ATT_W // LANES
    kv_spec = lambda off: pl.BlockSpec((None, None, S, LANES), lambda b, p, i: (off + p, b, 0, 0))
    return pl.pallas_call(
        functools.partial(_attn_kernel, S=S),
        grid=(B, n_pairs, S // ATT_SB),
        in_specs=[pl.BlockSpec((None, None, ATT_SB, LANES), lambda b, p, i: (p, b, i, 0)),
                  kv_spec(n_pairs), kv_spec(2 * n_pairs)],
        out_specs=pl.BlockSpec((None, ATT_SB, LANES), lambda b, p, i: (b, i, p)),
        out_shape=jax.ShapeDtypeStruct((B, S, ATT_W), BF16),
        scratch_shapes=[pltpu.VMEM((3, ATT_SB, LANES), F32), pltpu.VMEM((3, ATT_SB, LANES), F32),
                        pltpu.VMEM((3, ATT_QBLK, 2 * ATT_QBLK), F32)],
        compiler_params=_cparams(("parallel", "parallel", "arbitrary")),
        name="dilated_attn",
    )(qkv4, qkv4, qkv4)


def _ret_kernel(dec_ref, q_ref, k_ref, v_ref, g_ref, gn_ref, o_ref, of_scr, ob_scr, *, S):
    C = RET_CHUNK
    nc = S // C
    h = pl.program_id(1)
    nt_dims = (((1,), (1,)), ((), ()))
    tn_dims = (((0,), (0,)), ((), ()))

    def consts(direction):
        def log_g(shape):
            return -jnp.exp(jnp.full(shape, dec_ref[direction, h], F32))

        row = lax.broadcasted_iota(jnp.int32, (C, C), 0).astype(F32)
        col = lax.broadcasted_iota(jnp.int32, (C, C), 1).astype(F32)
        n = lax.broadcasted_iota(jnp.int32, (C, RET_HD), 0).astype(F32)
        lg = log_g((C, RET_HD))
        if direction == 0:
            diff = row - col
            mask = diff >= 0.0
            k_dec, q_dec = jnp.exp(lg * (C - 1.0 - n)), jnp.exp(lg * (n + 1.0))
        else:
            diff = col - row
            mask = diff > 0.0
            k_dec, q_dec = jnp.exp(lg * n), jnp.exp(lg * (C - n))
        dmat = jnp.where(mask, jnp.exp(log_g((C, C)) * jnp.maximum(diff, 0.0)), 0.0)
        return dmat, k_dec, q_dec, jnp.exp(log_g((RET_HD, RET_HD)) * C)

    def chunk_out(c, state, dmat, k_dec, q_dec, g_chunk):
        sl = pl.ds(pl.multiple_of(c * C, C), C)
        qc, kc, vc = q_ref[sl, :], k_ref[sl, :], v_ref[sl, :]
        qk = lax.dot_general(qc, kc, nt_dims, preferred_element_type=F32)
        inner = (qk * dmat).astype(BF16)
        o = jnp.dot(inner, vc, preferred_element_type=F32)
        o = o + jnp.dot((qc.astype(F32) * q_dec).astype(BF16), state.astype(BF16),
                        preferred_element_type=F32)
        kd = (kc.astype(F32) * k_dec).astype(BF16)
        kv = lax.dot_general(kd, vc, tn_dims, preferred_element_type=F32)
        return sl, o, state * g_chunk + kv

    cf, cb = consts(0), consts(1)

    def scan(i, states):
        sl_f, o_f, st_f = chunk_out(i, states[0], *cf)
        of_scr[sl_f, :] = o_f
        sl_b, o_b, st_b = chunk_out(nc - 1 - i, states[1], *cb)
        ob_scr[sl_b, :] = o_b
        return st_f, st_b

    zero = jnp.zeros((RET_HD, RET_HD), F32)
    lax.fori_loop(0, nc, scan, (zero, zero), unroll=4)

    gn = gn_ref[...]

    def finish(c, carry):
        sl = pl.ds(pl.multiple_of(c * C, C), C)
        o = of_scr[sl, :] + ob_scr[sl, :]
        mu = jnp.mean(o, axis=-1, keepdims=True)
        var = jnp.mean(jnp.square(o - mu), axis=-1, keepdims=True)
        y = (o - mu) * lax.rsqrt(var + NORM_EPS) * gn
        g = g_ref[sl, :].astype(F32)
        o_ref[sl, :] = (y * (g * (1.0 / (1.0 + jnp.exp(-g))))).astype(o_ref.dtype)
        return carry

    lax.fori_loop(0, nc, finish, 0, unroll=2)


def _retention(qkv4, decays, gn, B, S):
    spec = lambda off: pl.BlockSpec((None, None, S, LANES), lambda b, h: (off + h, b, 0, 0))
    return pl.pallas_call(
        functools.partial(_ret_kernel, S=S),
        grid=(B, RET_HEADS),
        in_specs=[pl.BlockSpec(memory_space=pltpu.SMEM),
                  spec(0), spec(RET_HEADS), spec(2 * RET_HEADS), spec(3 * RET_HEADS),
                  pl.BlockSpec((1, LANES), lambda b, h: (0, h))],
        out_specs=pl.BlockSpec((None, S, LANES), lambda b, h: (b, 0, h)),
        out_shape=jax.ShapeDtypeStruct((B, S, RET_W), BF16),
        scratch_shapes=[pltpu.VMEM((S, LANES), F32), pltpu.VMEM((S, LANES), F32)],
        compiler_params=_cparams(("parallel", "arbitrary")),
        name="retention",
    )(decays, qkv4, qkv4, qkv4, qkv4, gn)


HI16 = -65536
ROW_WORDS = D_MODEL // 2 + LANES


def _pack_bf16_pairs(x):
    n = x.shape[1] // 2
    bits = pltpu.bitcast(x.astype(BF16).astype(F32), jnp.int32)
    return (bits[:, n:] & HI16) | lax.shift_right_logical(bits[:, :n], 16)


def _unpack_bf16_pairs(w):
    lo = pltpu.bitcast(lax.shift_left(w, 16), F32).astype(BF16)
    hi = pltpu.bitcast(w & HI16, F32).astype(BF16)
    return jnp.concatenate([lo, hi], axis=1)


def _outproj_kernel(x_ref, att_ref, ret_ref, wo_ref, g2_ref, wr_ref, x1_ref, h2_ref, aff_ref):
    y = jnp.dot(att_ref[...], wo_ref[:ATT_W, :], preferred_element_type=F32)
    y = y + jnp.dot(ret_ref[...], wo_ref[ATT_W:, :], preferred_element_type=F32)
    x1 = x_ref[...] + y
    x1_ref[...] = x1
    ms = jnp.mean(x1 * x1, axis=-1, keepdims=True)
    h2 = x1 * lax.rsqrt(ms + NORM_EPS) * g2_ref[...]
    h2_ref[:, :D_MODEL // 2] = _pack_bf16_pairs(h2)
    logits = lax.dot_general(wr_ref[...], h2.astype(BF16), (((1,), (1,)), ((), ())),
                             preferred_element_type=F32)
    e = jnp.exp(logits - jnp.max(logits, axis=0, keepdims=True))
    aff = e / jnp.sum(e, axis=0, keepdims=True)
    aff_ref[...] = aff
    pad = jnp.zeros((LANES - N_EXPERTS, aff.shape[1]), F32)
    h2_ref[:, D_MODEL // 2:] = pltpu.bitcast(jnp.transpose(jnp.concatenate([aff, pad], axis=0)), jnp.int32)


def _outproj(x2d, att, ret, wo_bf, g2, wr_t_bf, tm):
    T = x2d.shape[0]
    full = lambda shape: pl.BlockSpec(shape, lambda i: (0,) * len(shape))
    return pl.pallas_call(
        _outproj_kernel,
        grid=(T // tm,),
        in_specs=[pl.BlockSpec((tm, D_MODEL), lambda i: (i, 0)),
                  pl.BlockSpec((tm, ATT_W), lambda i: (i, 0)),
                  pl.BlockSpec((tm, RET_W), lambda i: (i, 0)),
                  full((ATT_W + RET_W, D_MODEL)), full((1, D_MODEL)), full((N_EXPERTS, D_MODEL))],
        out_specs=[pl.BlockSpec((tm, D_MODEL), lambda i: (i, 0)),
                   pl.BlockSpec((tm, ROW_WORDS), lambda i: (i, 0)),
                   pl.BlockSpec((N_EXPERTS, tm), lambda i: (0, i))],
        out_shape=[jax.ShapeDtypeStruct((T, D_MODEL), F32),
                   jax.ShapeDtypeStruct((T, ROW_WORDS), jnp.int32),
                   jax.ShapeDtypeStruct((N_EXPERTS, T), F32)],
        compiler_params=_cparams(("parallel",)),
        name="outproj_router",
    )(x2d, att, ret, wo_bf, g2, wr_t_bf)


def _select_kernel(aff_ref, sel_ref, *, cap):
    aff = aff_ref[...]
    E, T = aff.shape
    capf = jnp.float32(cap)

    def count(mask):
        return jnp.sum(jnp.where(mask, 1.0, 0.0), axis=1, keepdims=True)

    def thr_step(i, thr_bits):
        cand = thr_bits | jnp.left_shift(jnp.int32(1), 30 - i)
        return jnp.where(count(aff >= pltpu.bitcast(cand, F32)) >= capf, cand, thr_bits)

    thr = pltpu.bitcast(lax.fori_loop(0, 31, thr_step, jnp.zeros((E, 1), jnp.int32)), F32)
    above = aff > thr
    ties = aff == thr
    need = capf - count(above)
    idx = lax.broadcasted_iota(jnp.int32, (E, T), 1)
    n_idx_bits = max(1, (T - 1).bit_length())

    def cut_step(i, cut):
        cand = cut | jnp.left_shift(jnp.int32(1), n_idx_bits - 1 - i)
        return jnp.where(count(ties & (idx < cand)) < need, cand, cut)

    cut = lax.fori_loop(0, n_idx_bits, cut_step, jnp.zeros((E, 1), jnp.int32))
    sel_ref[...] = jnp.where(above | (ties & (idx <= cut)), 1, 0).astype(jnp.int32)


def _select(aff_t, cap):
    E, T = aff_t.shape
    return pl.pallas_call(
        functools.partial(_select_kernel, cap=cap),
        grid=(1,),
        in_specs=[pl.BlockSpec((E, T), lambda i: (0, 0))],
        out_specs=pl.BlockSpec((E, T), lambda i: (0, 0)),
        out_shape=jax.ShapeDtypeStruct((E, T), jnp.int32),
        compiler_params=_cparams(("arbitrary",)),
        name="expert_select",
    )(aff_t)


CMB_TOK = 512


def _token_major(slots):
    n_e, n_tok = slots.shape
    padded = jnp.concatenate([slots, jnp.full((LANES - n_e, n_tok), -1, jnp.int32)], axis=0)
    return pltpu.bitcast(jnp.transpose(pltpu.bitcast(padded, F32)), jnp.int32)


def _spread_slots(pos, bases, width):
    n_e, n_col = len(bases), pos.shape[1]
    col = lax.broadcasted_iota(jnp.int32, pos.shape, 1)
    base_arr = jnp.zeros(pos.shape, jnp.int32)
    for e in range(n_e):
        base_arr = jnp.where(col == e, bases[e], base_arr)
    rel = pos - base_arr
    rel = jnp.where((rel >= 0) & (rel < width), rel, -1).astype(F32).astype(BF16)
    expander = (lax.broadcasted_iota(jnp.int32, (n_col, n_e * width), 0)
                == lax.broadcasted_iota(jnp.int32, (n_col, n_e * width), 1) // width)
    return jnp.dot(rel, jnp.where(expander, 1.0, 0.0).astype(BF16), preferred_element_type=F32)


def _compact_kernel(excl_ref, cnt_ref, slots_ref, idx_ref, pos_ref, *, cap, nblk):
    b = pl.program_id(0)
    ntiles = cap // LANES
    width = 2 * LANES

    @pl.when(b == 0)
    def _():
        idx_ref[...] = jnp.zeros(idx_ref.shape, F32)

    pos_ref[...] = _token_major(slots_ref[...])

    lane = lax.broadcasted_iota(jnp.int32, (CMB_TOK, width), 1)
    tok = (b * CMB_TOK + lax.broadcasted_iota(jnp.int32, (CMB_TOK, width), 0)).astype(F32)

    def window(e, tile_nominal):
        tile = jnp.minimum(tile_nominal, ntiles - 2)
        pos_col = pos_ref[:, e:e + 1]
        hit = ((pos_col - tile * LANES) == lane) & (pos_col >= tile_nominal * LANES)
        row = jnp.sum(jnp.where(hit, tok, 0.0), axis=0, keepdims=True)
        for j in range(2):
            idx_ref[e, pl.ds(tile + j, 1), :] += row[:, j * LANES:(j + 1) * LANES]

    tiles = [jnp.minimum(excl_ref[e * nblk + b] // LANES, ntiles - 2) for e in range(N_EXPERTS)]
    rel = _spread_slots(pos_ref[...], [t * LANES for t in tiles], width)
    lane_f = lax.broadcasted_iota(jnp.int32, (CMB_TOK, LANES), 1).astype(F32)
    tok_f = (b * CMB_TOK + lax.broadcasted_iota(jnp.int32, (CMB_TOK, LANES), 0)).astype(F32)
    for e in range(N_EXPERTS):
        for j in range(2):
            piece = rel[:, e * width + j * LANES:e * width + (j + 1) * LANES]
            row = jnp.sum(jnp.where(piece == lane_f + float(j * LANES), tok_f, 0.0), axis=0, keepdims=True)
            idx_ref[e, pl.ds(tiles[e] + j, 1), :] += row

    for e in range(N_EXPERTS):
        first = excl_ref[e * nblk + b]
        tile0 = first // LANES
        need_end = first + cnt_ref[e * nblk + b]
        n_extra = jnp.maximum((need_end - tile0 * LANES + width - 1) // width - 1, 0)

        def extra(k, carry):
            window(e, tile0 + 2 * (k + 1))
            return carry

        lax.fori_loop(0, n_extra, extra, 0)


def _compact(slots, excl, cnt, cap):
    E, T = slots.shape
    nblk = T // CMB_TOK
    grid_spec = pltpu.PrefetchScalarGridSpec(
        num_scalar_prefetch=2,
        grid=(nblk,),
        in_specs=[pl.BlockSpec((E, CMB_TOK), lambda b, *_: (0, b))],
        out_specs=pl.BlockSpec((E, cap // LANES, LANES), lambda b, *_: (0, 0, 0)),
        scratch_shapes=[pltpu.VMEM((CMB_TOK, LANES), jnp.int32)],
    )
    idx = pl.pallas_call(
        functools.partial(_compact_kernel, cap=cap, nblk=nblk),
        grid_spec=grid_spec,
        out_shape=jax.ShapeDtypeStruct((E, cap // LANES, LANES), F32),
        compiler_params=_cparams(("arbitrary",)),
        name="expert_lists",
    )(excl, cnt, slots)
    return idx.reshape(E, cap).astype(jnp.int32)


SC_CORES = 2
SC_SUBCORES = 16
SC_ROWS = 64


def _sc_gather(table, idx):
    n, width = idx.shape[0], table.shape[1]
    workers = SC_CORES * SC_SUBCORES
    per_worker = n // workers
    mesh = plsc.VectorSubcoreMesh(core_axis_name="c", subcore_axis_name="s")

    @functools.partial(
        pl.kernel, mesh=mesh,
        out_type=jax.ShapeDtypeStruct((n, width), table.dtype),
        scratch_types=[pltpu.VMEM((SC_ROWS,), jnp.int32), pltpu.VMEM((SC_ROWS, width), table.dtype),
                       pltpu.SemaphoreType.DMA],
    )
    def gather(table_hbm, idx_hbm, out_hbm, idx_v, rows_v, sem):
        base = (lax.axis_index("s") * SC_CORES + lax.axis_index("c")) * per_worker

        @pl.loop(0, per_worker // SC_ROWS)
        def _(i):
            off = pl.multiple_of(base + i * SC_ROWS, SC_ROWS)
            pltpu.sync_copy(idx_hbm.at[pl.ds(off, SC_ROWS)], idx_v)
            pltpu.async_copy(table_hbm.at[idx_v], rows_v, sem).wait()
            pltpu.sync_copy(rows_v, out_hbm.at[pl.ds(off, SC_ROWS)])

    return gather(table, idx)


FF_CHUNK = 512


def _ffn_kernel(xin_ref, wg_ref, wu_ref, wd_ref, y_ref):
    xin = _unpack_bf16_pairs(xin_ref[:, :D_MODEL // 2])
    affs = pltpu.bitcast(xin_ref[:, D_MODEL // 2:], F32)
    lane = lax.broadcasted_iota(jnp.int32, affs.shape, 1)
    gate = jnp.sum(jnp.where(lane == pl.program_id(0), affs, 0.0), axis=-1, keepdims=True)
    acc = jnp.zeros(y_ref.shape, F32)
    for f in range(D_FF // FF_CHUNK):
        fs = slice(f * FF_CHUNK, (f + 1) * FF_CHUNK)
        a = jnp.dot(xin, wg_ref[:, fs], preferred_element_type=F32)
        u = jnp.dot(xin, wu_ref[:, fs], preferred_element_type=F32)
        hmid = (a * (1.0 / (1.0 + jnp.exp(-a))) * u).astype(BF16)
        acc = acc + jnp.dot(hmid, wd_ref[fs, :], preferred_element_type=F32)
    y_ref[...] = (acc * gate).astype(y_ref.dtype)


def _ffn(xin, wg_bf, wu_bf, wd_bf, tm):
    E, cap, _ = xin.shape
    return pl.pallas_call(
        _ffn_kernel,
        grid=(E, cap // tm),
        in_specs=[pl.BlockSpec((None, tm, ROW_WORDS), lambda e, j: (e, j, 0)),
                  pl.BlockSpec((None, D_MODEL, D_FF), lambda e, j: (e, 0, 0)),
                  pl.BlockSpec((None, D_MODEL, D_FF), lambda e, j: (e, 0, 0)),
                  pl.BlockSpec((None, D_FF, D_MODEL), lambda e, j: (e, 0, 0))],
        out_specs=pl.BlockSpec((None, tm, D_MODEL), lambda e, j: (e, j, 0)),
        out_shape=jax.ShapeDtypeStruct((E, cap, D_MODEL), BF16),
        compiler_params=_cparams(("parallel", "arbitrary")),
        name="expert_ffn",
    )(xin, wg_bf, wu_bf, wd_bf)


CMB_ROWS = 128
BF16_SUBLANES = 16


def _combine_kernel(excl_ref, cnt_ref, x1_ref, slots_ref, y_hbm, o_ref, ybuf, xbuf, sem, xsem, pos_ref, *, cap, nblk):
    b = pl.program_id(0)
    slot = b % 2
    lane = lax.broadcasted_iota(jnp.int32, (CMB_TOK, CMB_ROWS), 1)

    def first_row(e, blk):
        return (excl_ref[e * nblk + blk] // BF16_SUBLANES) * BF16_SUBLANES

    def fetch(e, row, dst, s):
        row = pl.multiple_of(jnp.minimum(row, cap - CMB_ROWS), BF16_SUBLANES)
        return pltpu.make_async_copy(y_hbm.at[e, pl.ds(row, CMB_ROWS), :], dst, s)

    def start_block(blk, sl):
        for e in range(N_EXPERTS):
            fetch(e, first_row(e, blk), ybuf.at[sl, e], sem.at[sl, e]).start()

    @pl.when(b == 0)
    def _():
        start_block(0, 0)

    @pl.when(b + 1 < nblk)
    def _():
        start_block(b + 1, 1 - slot)

    def onehot(pos_col, nominal):
        hit = ((pos_col - jnp.minimum(nominal, cap - CMB_ROWS)) == lane) & (pos_col >= nominal)
        return jnp.where(hit, 1.0, 0.0).astype(BF16)

    pos_ref[...] = _token_major(slots_ref[...])
    bases = []
    for e in range(N_EXPERTS):
        nominal0 = first_row(e, b)
        fetch(e, nominal0, ybuf.at[slot, e], sem.at[slot, e]).wait()
        bases.append(jnp.minimum(nominal0, cap - CMB_ROWS))
    rel = _spread_slots(pos_ref[...], bases, CMB_ROWS)
    lane_f = lane.astype(F32)
    parts = [jnp.where(rel[:, e * CMB_ROWS:(e + 1) * CMB_ROWS] == lane_f, 1.0, 0.0).astype(BF16)
             for e in range(N_EXPERTS)]
    rows = ybuf[slot].reshape(N_EXPERTS * CMB_ROWS, D_MODEL)
    o_ref[...] = x1_ref[...] + jnp.dot(jnp.concatenate(parts, axis=1), rows, preferred_element_type=F32)

    for e in range(N_EXPERTS):
        nominal0 = first_row(e, b)
        need_end = excl_ref[e * nblk + b] + cnt_ref[e * nblk + b]
        n_extra = jnp.maximum((need_end - nominal0 + CMB_ROWS - 1) // CMB_ROWS - 1, 0)

        def extra(k, carry):
            nominal = nominal0 + (k + 1) * CMB_ROWS
            cp = fetch(e, nominal, xbuf, xsem)
            cp.start()
            cp.wait()
            o_ref[...] += jnp.dot(onehot(pos_ref[:, e:e + 1], nominal), xbuf[...], preferred_element_type=F32)
            return carry

        lax.fori_loop(0, n_extra, extra, 0)


def _combine(x1, slots, y, excl, cnt):
    T = x1.shape[0]
    E, cap, _ = y.shape
    nblk = T // CMB_TOK
    grid_spec = pltpu.PrefetchScalarGridSpec(
        num_scalar_prefetch=2,
        grid=(nblk,),
        in_specs=[pl.BlockSpec((CMB_TOK, D_MODEL), lambda b, *_: (b, 0)),
                  pl.BlockSpec((E, CMB_TOK), lambda b, *_: (0, b)),
                  pl.BlockSpec(memory_space=pl.ANY)],
        out_specs=pl.BlockSpec((CMB_TOK, D_MODEL), lambda b, *_: (b, 0)),
        scratch_shapes=[pltpu.VMEM((2, E, CMB_ROWS, D_MODEL), BF16),
                        pltpu.VMEM((CMB_ROWS, D_MODEL), BF16),
                        pltpu.SemaphoreType.DMA((2, E)),
                        pltpu.SemaphoreType.DMA(()),
                        pltpu.VMEM((CMB_TOK, LANES), jnp.int32)],
    )
    return pl.pallas_call(
        functools.partial(_combine_kernel, cap=cap, nblk=nblk),
        grid_spec=grid_spec,
        out_shape=jax.ShapeDtypeStruct((T, D_MODEL), F32),
        compiler_params=_cparams(("arbitrary",)),
        name="moe_combine",
    )(excl, cnt, x1, slots, y)


def _layer(x, p):
    B, S, _ = x.shape
    T = B * S
    tm = 512
    x2d = x.reshape(T, D_MODEL)
    qkv_att, qkv_ret = _inproj(x2d, S, p["g1"], p["w_in"], p["gq"], p["gk"], _rope_tables(S), tm)
    att = _attention(qkv_att.reshape(N_ATT_SLABS, B, S, LANES), B, S).reshape(T, ATT_W)
    ret = _retention(qkv_ret.reshape(N_SLABS - N_ATT_SLABS, B, S, LANES), p["decays"], p["gn"], B, S).reshape(T, RET_W)
    x1, h2, aff_t = _outproj(x2d, att, ret, p["w_out"], p["g2"], p["w_router_t"], tm)

    cap = CAPACITY_FACTOR * T // N_EXPERTS
    sel = _select(aff_t, cap)

    csum = jnp.cumsum(sel, axis=1)
    slots = jnp.where(sel > 0, csum - 1, -1)
    ends = csum[:, CMB_TOK - 1::CMB_TOK]
    excl = jnp.concatenate([jnp.zeros((N_EXPERTS, 1), jnp.int32), ends[:, :-1]], axis=1).reshape(-1)
    cnt = ends.reshape(-1) - excl

    idx = _compact(slots, excl, cnt, cap)
    xin = _sc_gather(h2, idx.reshape(-1)).reshape(N_EXPERTS, cap, ROW_WORDS)
    y = _ffn(xin, p["w_gate"], p["w_up"], p["w_down"], tm)
    out = _combine(x1, slots, y, excl, cnt)
    return out.reshape(B, S, D_MODEL)


def kernel(x_prompt, x_sample, norm1_g, w_in, attn_qnorm_g, attn_knorm_g, ret_decay_fwd, ret_decay_bwd,
           ret_norm_g, w_out, norm2_g, w_router, w_gate_e, w_up_e, w_down_e):
    y_prompt, y_sample = x_prompt, x_sample
    for l in range(norm1_g.shape[0]):
        p = {
            "g1": norm1_g[l][None, :],
            "w_in": w_in[l].astype(BF16),
            "gq": jnp.tile(attn_qnorm_g[l], LANES // ATT_HD)[None, :],
            "gk": jnp.tile(attn_knorm_g[l], LANES // ATT_HD)[None, :],
            "decays": jnp.stack([ret_decay_fwd[l], ret_decay_bwd[l]]).astype(F32),
            "gn": ret_norm_g[l][None, :].astype(F32),
            "w_out": w_out[l].astype(BF16),
            "g2": norm2_g[l][None, :],
            "w_router_t": w_router[l].T.astype(BF16),
            "w_gate": w_gate_e[l].astype(BF16),
            "w_up": w_up_e[l].astype(BF16),
            "w_down": w_down_e[l].astype(BF16),
        }
        y_prompt = _layer(y_prompt, p)
        y_sample = _layer(y_sample, p)
    return (y_prompt, y_sample)
```
